```python
import jax, jax.numpy as jnp
from jax import lax
import numpy as np

D_MODEL = 2048
BATCH = 2
SEQ = 4096
DEPTH = 2

PLE_DIM = 256
MIX_WIDTH = D_MODEL
NSA_HEADS = 16
NSA_KV_HEADS = 4
NSA_WIDTH = MIX_WIDTH // 2
NSA_HEAD_DIM = NSA_WIDTH // NSA_HEADS
NSA_KV_WIDTH = NSA_KV_HEADS * NSA_HEAD_DIM
CMP_LEN = 32
CMP_STRIDE = 16
SLC_BLOCK = 64
N_SELECT = 16
WINDOW = 512
Q_BLOCK = 128
LRU_WIDTH = MIX_WIDTH // 4
LRU_BLOCKS = 8
LRU_BLOCK_DIM = LRU_WIDTH // LRU_BLOCKS
CONV_WIDTH = 4
LRU_C = 8.0
GLA_WIDTH = MIX_WIDTH // 4
GLA_HEADS = 4
GLA_HEAD_DIM = GLA_WIDTH // GLA_HEADS
GLA_GATE_RANK = 16
GLA_GATE_TAU = 16.0
GLA_CHUNK = 64
D_FF = 4 * D_MODEL
EPS = 1e-6
NEG_INF = -1e30

IN_SIZES = (NSA_WIDTH, 6 * NSA_KV_WIDTH, 3 * NSA_HEADS,
            LRU_WIDTH, LRU_WIDTH,
            GLA_WIDTH, GLA_WIDTH, GLA_WIDTH, GLA_WIDTH, GLA_GATE_RANK)
D_IN = NSA_WIDTH + 6 * NSA_KV_WIDTH + 3 * NSA_HEADS + 2 * LRU_WIDTH + 4 * GLA_WIDTH + GLA_GATE_RANK

kernel_name = 'hybrid_nsa_rglru_gla_trunk'


def rms_norm(x, g):
    x32 = x.astype(jnp.float32)
    y = x32 * lax.rsqrt(jnp.mean(x32 * x32, axis=-1, keepdims=True) + EPS)
    return (y * g.astype(jnp.float32)).astype(x.dtype)


def alibi_slopes(n):
    return jnp.exp2(-8.0 * jnp.arange(1, n + 1, dtype=jnp.float32) / n)


def nsa_mixer(q, kv, gates, cmp_w, cmp_pe):
    B, T, _ = q.shape
    G, HPG, dk = NSA_KV_HEADS, NSA_HEADS // NSA_KV_HEADS, NSA_HEAD_DIM
    q = q.reshape(B, T, G, HPG, dk) * dk ** -0.5
    kc, vc, ks, vs, kw, vw = [a.reshape(B, T, G, dk) for a in jnp.split(kv, 6, axis=-1)]
    slopes = alibi_slopes(NSA_HEADS).reshape(G, HPG)
    pos = jnp.arange(T)

    n_cmp = (T - CMP_LEN) // CMP_STRIDE + 1
    cmp_start = jnp.arange(n_cmp) * CMP_STRIDE
    cmp_idx = cmp_start[:, None] + jnp.arange(CMP_LEN)[None, :]

    def compress(a, w, pe):
        blk = a[:, cmp_idx] + pe[None, None, :, None, :]
        blk = blk.transpose(0, 1, 3, 2, 4).reshape(B, n_cmp, G, CMP_LEN * dk)
        return blk @ w

    k_cmp = compress(kc, cmp_w[0], cmp_pe[0])
    v_cmp = compress(vc, cmp_w[1], cmp_pe[1])
    d_cmp = (pos[:, None] - (cmp_start + CMP_LEN - 1)[None, :]).astype(jnp.float32)
    valid_cmp = d_cmp >= 0
    s = jnp.einsum('btghd,bngd->bghtn', q, k_cmp).astype(jnp.float32) - slopes[:, :, None, None] * d_cmp
    s = jnp.where(valid_cmp, s, NEG_INF)
    p_cmp = jax.nn.softmax(s, axis=-1) * valid_cmp
    o_cmp = jnp.einsum('bghtn,bngd->btghd', p_cmp, v_cmp)

    n_slc = T // SLC_BLOCK
    n_sel = min(N_SELECT, n_slc)
    slc_start = jnp.arange(n_slc) * SLC_BLOCK
    overlap = ((cmp_start[:, None] < slc_start[None, :] + SLC_BLOCK)
               & (slc_start[None, :] < cmp_start[:, None] + CMP_LEN)).astype(jnp.float32)
    imp = jnp.einsum('bghtn,nj->bgtj', p_cmp, overlap)
    blk = jnp.arange(n_slc)[None, :]
    t_blk = (pos // SLC_BLOCK)[:, None]
    forced = (blk == 0) | (blk == t_blk) | (blk == t_blk - 1)
    score = jnp.where(blk > t_blk, -1.0, jnp.where(forced, 1e4, imp))
    _, sel_idx = lax.top_k(score, n_sel)

    ks_blk = ks.reshape(B, n_slc, SLC_BLOCK, G, dk).transpose(0, 3, 1, 2, 4)
    vs_blk = vs.reshape(B, n_slc, SLC_BLOCK, G, dk).transpose(0, 3, 1, 2, 4)
    kw_pad = jnp.pad(kw, ((0, 0), (WINDOW, 0), (0, 0), (0, 0)))
    vw_pad = jnp.pad(vw, ((0, 0), (WINDOW, 0), (0, 0), (0, 0)))

    n_qb = T // Q_BLOCK
    q_blocks = q.reshape(B, n_qb, Q_BLOCK, G, HPG, dk).transpose(1, 0, 2, 3, 4, 5)
    idx_blocks = sel_idx.reshape(B, G, n_qb, Q_BLOCK, n_sel).transpose(2, 0, 1, 3, 4)
    t_blocks = pos.reshape(n_qb, Q_BLOCK)
    b_ix = jnp.arange(B)[:, None, None, None]
    g_ix = jnp.arange(G)[None, :, None, None]
    blk_off = jnp.arange(SLC_BLOCK)
    win_off = jnp.arange(WINDOW + Q_BLOCK)

    def block_fn(args):
        q_b, idx_b, t_b = args
        k_sel = ks_blk[b_ix, g_ix, idx_b].reshape(B, G, Q_BLOCK, n_sel * SLC_BLOCK, dk)
        v_sel = vs_blk[b_ix, g_ix, idx_b].reshape(B, G, Q_BLOCK, n_sel * SLC_BLOCK, dk)
        k_pos = (idx_b[..., None] * SLC_BLOCK + blk_off).reshape(B, G, Q_BLOCK, n_sel * SLC_BLOCK)
        d_sel = (t_b[None, None, :, None] - k_pos).astype(jnp.float32)[:, :, None]
        s1 = jnp.einsum('bqghd,bgqkd->bghqk', q_b, k_sel).astype(jnp.float32) - slopes[None, :, :, None, None] * d_sel
        s1 = jnp.where(d_sel >= 0, s1, NEG_INF)
        o_sel = jnp.einsum('bghqk,bgqkd->bqghd', jax.nn.softmax(s1, axis=-1), v_sel)
        w_idx = t_b[0] + win_off
        k_win = kw_pad[:, w_idx]
        v_win = vw_pad[:, w_idx]
        d_win = (t_b[:, None] - (w_idx - WINDOW)[None, :]).astype(jnp.float32)
        valid = (d_win >= 0) & (d_win < WINDOW) & (w_idx >= WINDOW)[None, :]
        s2 = jnp.einsum('bqghd,bkgd->bghqk', q_b, k_win).astype(jnp.float32) - slopes[:, :, None, None] * d_win
        s2 = jnp.where(valid, s2, NEG_INF)
        o_win = jnp.einsum('bghqk,bkgd->bqghd', jax.nn.softmax(s2, axis=-1), v_win)
        return o_sel, o_win

    o_sel, o_win = lax.map(block_fn, (q_blocks, idx_blocks, t_blocks))
    o_sel = o_sel.transpose(1, 0, 2, 3, 4, 5).reshape(B, T, G, HPG, dk)
    o_win = o_win.transpose(1, 0, 2, 3, 4, 5).reshape(B, T, G, HPG, dk)
    g = jax.nn.sigmoid(gates.astype(jnp.float32)).reshape(B, T, G, HPG, 3)
    o = g[..., 0:1] * o_cmp + g[..., 1:2] * o_sel + g[..., 2:3] * o_win
    return o.reshape(B, T, NSA_WIDTH)


def _lin_combine(e1, e2):
    a1, b1 = e1
    a2, b2 = e2
    return a1 * a2, a2 * b1 + b2


def rglru_mixer(xb, yb, conv_w, conv_b, wa, ba, wi, bi, lam):
    B, T, _ = xb.shape
    xp = jnp.pad(xb, ((0, 0), (CONV_WIDTH - 1, 0), (0, 0)))
    xc = conv_b + sum(xp[:, k:k + T] * conv_w[k] for k in range(CONV_WIDTH))
    xr = xc.reshape(B, T, LRU_BLOCKS, LRU_BLOCK_DIM)
    r = jax.nn.sigmoid(jnp.einsum('btnc,ncd->btnd', xr, wa).reshape(B, T, LRU_WIDTH) + ba)
    i = jax.nn.sigmoid(jnp.einsum('btnc,ncd->btnd', xr, wi).reshape(B, T, LRU_WIDTH) + bi)
    log_a = -LRU_C * r.astype(jnp.float32) * jax.nn.softplus(-lam.astype(jnp.float32))
    a = jnp.exp(log_a)
    u = jnp.sqrt(-jnp.expm1(2.0 * log_a)) * (i * xc).astype(jnp.float32)
    _, h = lax.associative_scan(_lin_combine, (a, u), axis=1)
    return h * jax.nn.gelu(yb.astype(jnp.float32))


def gla_mixer(q, k, v, r, z_lr, w_gate2, b_gate, norm_g):
    B, T, _ = q.shape
    H, d, C = GLA_HEADS, GLA_HEAD_DIM, GLA_CHUNK
    N = T // C
    log_alpha = jax.nn.log_sigmoid((z_lr @ w_gate2 + b_gate).astype(jnp.float32)) / GLA_GATE_TAU

    def chunks(a):
        return a.astype(jnp.float32).reshape(B, N, C, H, d)

    qc = chunks(q) * d ** -0.5
    kc = chunks(k)
    vc = chunks(v)
    bcum = jnp.cumsum(chunks(log_alpha), axis=2)
    b_last = bcum[:, :, -1:]
    q_t = qc * jnp.exp(bcum)
    k_t = kc * jnp.exp(-bcum)
    k_end = kc * jnp.exp(b_last - bcum)
    causal = jnp.tril(jnp.ones((C, C), dtype=bool))
    att = jnp.where(causal, jnp.einsum('bnthd,bnshd->bnhts', q_t, k_t), 0.0)
    o_intra = jnp.einsum('bnhts,bnshv->bnthv', att, vc)
    upd = jnp.einsum('bnshk,bnshv->nbhkv', k_end, vc)
    decay = jnp.exp(b_last[:, :, 0]).transpose(1, 0, 2, 3)

    def step(state, xs):
        d_n, u_n = xs
        return d_n[..., None] * state + u_n, state

    _, s_in = lax.scan(step, jnp.zeros((B, H, d, d), jnp.float32), (decay, upd))
    o_inter = jnp.einsum('bnthk,nbhkv->bnthv', q_t, s_in)
    o = rms_norm((o_intra + o_inter).reshape(B, T, H, d), norm_g.reshape(H, d))
    return o.reshape(B, T, GLA_WIDTH) * jax.nn.silu(r.astype(jnp.float32))


def hybrid_layer(h, p_i, norm_mix_pre, w_in, nsa_cmp_w, nsa_cmp_pe, lru_conv_w, lru_conv_b,
                 lru_wa, lru_ba, lru_wi, lru_bi, lru_lambda, gla_w_gate2, gla_b_gate, gla_norm,
                 w_out, norm_mix_post, norm_mlp_pre, w_up, w_down, norm_mlp_post, w_ple_gate, w_ple):
    u = rms_norm(h, norm_mix_pre)
    proj = u @ w_in
    offsets = tuple(int(o) for o in np.cumsum(IN_SIZES)[:-1])
    (nsa_q, nsa_kv, nsa_g, lru_x, lru_y,
     gla_q, gla_k, gla_v, gla_r, gla_z) = jnp.split(proj, offsets, axis=-1)
    o_a = nsa_mixer(nsa_q, nsa_kv, nsa_g, nsa_cmp_w, nsa_cmp_pe)
    o_b = rglru_mixer(lru_x, lru_y, lru_conv_w, lru_conv_b, lru_wa, lru_ba, lru_wi, lru_bi, lru_lambda)
    o_c = gla_mixer(gla_q, gla_k, gla_v, gla_r, gla_z, gla_w_gate2, gla_b_gate, gla_norm)
    mix = jnp.concatenate([o_a.astype(h.dtype), o_b.astype(h.dtype), o_c.astype(h.dtype)], axis=-1)
    h = h + rms_norm(mix @ w_out, norm_mix_post)
    u = rms_norm(h, norm_mlp_pre)
    h = h + rms_norm(jnp.square(jax.nn.relu(u @ w_up)) @ w_down, norm_mlp_post)
    h = h + jax.nn.sigmoid(h @ w_ple_gate) * (p_i @ w_ple)
    return h


def setup_inputs(seed: int = 0) -> dict:
    key = jax.random.key(seed)
    ks = jax.random.split(key, 24)
    dk = NSA_HEAD_DIM
    nrm = jax.random.normal

    def gain(k):
        return 1.0 + 0.1 * nrm(k, (DEPTH, D_MODEL), jnp.float32)

    u = jax.random.uniform(ks[12], (DEPTH, LRU_WIDTH), jnp.float32, minval=0.9, maxval=0.999)
    s = u ** (1.0 / LRU_C)
    return {
        'x': nrm(ks[0], (BATCH, SEQ, D_MODEL), jnp.float32),
        'p': nrm(ks[1], (DEPTH, BATCH, SEQ, PLE_DIM), jnp.float32),
        'norm_mix_pre': gain(ks[2]),
        'w_in': nrm(ks[3], (DEPTH, D_MODEL, D_IN), jnp.float32) * D_MODEL ** -0.5,
        'nsa_cmp_w': nrm(ks[4], (DEPTH, 2, CMP_LEN * dk, dk), jnp.float32) * (CMP_LEN * dk) ** -0.5,
        'nsa_cmp_pe': 0.1 * nrm(ks[5], (DEPTH, 2, CMP_LEN, dk), jnp.float32),
        'lru_conv_w': nrm(ks[6], (DEPTH, CONV_WIDTH, LRU_WIDTH), jnp.float32) * CONV_WIDTH ** -0.5,
        'lru_conv_b': 0.01 * nrm(ks[7], (DEPTH, LRU_WIDTH), jnp.float32),
        'lru_wa': nrm(ks[8], (DEPTH, LRU_BLOCKS, LRU_BLOCK_DIM, LRU_BLOCK_DIM), jnp.float32) * LRU_BLOCK_DIM ** -0.5,
        'lru_ba': 0.01 * nrm(ks[9], (DEPTH, LRU_WIDTH), jnp.float32),
        'lru_wi': nrm(ks[10], (DEPTH, LRU_BLOCKS, LRU_BLOCK_DIM, LRU_BLOCK_DIM), jnp.float32) * LRU_BLOCK_DIM ** -0.5,
        'lru_bi': 0.01 * nrm(ks[11], (DEPTH, LRU_WIDTH), jnp.float32),
        'lru_lambda': jnp.log(s) - jnp.log1p(-s),
        'gla_w_gate2': nrm(ks[13], (DEPTH, GLA_GATE_RANK, GLA_WIDTH), jnp.float32) * GLA_GATE_RANK ** -0.5,
        'gla_b_gate': 0.01 * nrm(ks[14], (DEPTH, GLA_WIDTH), jnp.float32),
        'gla_norm': 1.0 + 0.1 * nrm(ks[15], (DEPTH, GLA_WIDTH), jnp.float32),
        'w_out': nrm(ks[16], (DEPTH, MIX_WIDTH, D_MODEL), jnp.float32) * MIX_WIDTH ** -0.5,
        'norm_mix_post': gain(ks[17]),
        'norm_mlp_pre': gain(ks[18]),
        'w_up': nrm(ks[19], (DEPTH, D_MODEL, D_FF), jnp.float32) * D_MODEL ** -0.5,
        'w_down': nrm(ks[20], (DEPTH, D_FF, D_MODEL), jnp.float32) * D_FF ** -0.5,
        'norm_mlp_post': gain(ks[21]),
        'w_ple_gate': nrm(ks[22], (DEPTH, D_MODEL, D_MODEL), jnp.float32) * D_MODEL ** -0.5,
        'w_ple': nrm(ks[23], (DEPTH, PLE_DIM, D_MODEL), jnp.float32) * PLE_DIM ** -0.5,
    }


def reference(x, p, norm_mix_pre, w_in, nsa_cmp_w, nsa_cmp_pe, lru_conv_w, lru_conv_b,
              lru_wa, lru_ba, lru_wi, lru_bi, lru_lambda, gla_w_gate2, gla_b_gate, gla_norm,
              w_out, norm_mix_post, norm_mlp_pre, w_up, w_down, norm_mlp_post, w_ple_gate, w_ple):
    h = x
    for i in range(DEPTH):
        h = hybrid_layer(h, p[i], norm_mix_pre[i], w_in[i], nsa_cmp_w[i], nsa_cmp_pe[i],
                         lru_conv_w[i], lru_conv_b[i], lru_wa[i], lru_ba[i], lru_wi[i], lru_bi[i],
                         lru_lambda[i], gla_w_gate2[i], gla_b_gate[i], gla_norm[i], w_out[i],
                         norm_mix_post[i], norm_mlp_pre[i], w_up[i], w_down[i], norm_mlp_post[i],
                         w_ple_gate[i], w_ple[i])
    return h
```

```python
import functools

import numpy as np
import jax
import jax.numpy as jnp
from jax import lax
from jax.experimental import pallas as pl
from jax.experimental.pallas import tpu as pltpu

F32 = jnp.float32
BF16 = jnp.bfloat16

D_MODEL = 2048
PLE_DIM = 256
NSA_HEADS = 16
NSA_KV_HEADS = 4
HPG = NSA_HEADS // NSA_KV_HEADS
NSA_WIDTH = 1024
DK = 64
NSA_KV_WIDTH = NSA_KV_HEADS * DK
CMP_LEN = 32
CMP_STRIDE = 16
SLC_BLOCK = 64
SLC_SHIFT = 6
N_SELECT = 16
WINDOW = 512
LRU_WIDTH = 512
LRU_BLOCKS = 8
LRU_BLOCK_DIM = 64
CONV_WIDTH = 4
LRU_C = 8.0
GLA_WIDTH = 512
GLA_HEADS = 4
GLA_HEAD_DIM = 128
GLA_GATE_RANK = 16
GLA_GATE_TAU = 16.0
GLA_CHUNK = 64
CHUNK_SHIFT = 6
D_FF = 4 * D_MODEL
EPS = 1e-6
NEG_INF = -1e30

COL_Q = 0
COL_KV = 1024
COL_LRU_X = 2560
COL_LRU_Y = 3072
COL_GLA_Q = 3584
COL_GLA_K = 4096
COL_GLA_V = 4608
COL_GLA_R = 5120
COL_GZ = 5632
D_IN_PAD = 5760
N_GATES = 3 * NSA_HEADS

SLOPES = tuple(2.0 ** (-8.0 * i / NSA_HEADS) for i in range(1, NSA_HEADS + 1))

VMEM_LIMIT = 56 * 1024 * 1024

NT_DIMS = (((1,), (1,)), ((), ()))


def _params(sem):
    return pltpu.CompilerParams(dimension_semantics=sem, vmem_limit_bytes=VMEM_LIMIT)


def _dot(a, b, **kw):
    return jnp.dot(a, b, preferred_element_type=F32, **kw)


def _dot_nt(a, b, **kw):
    return lax.dot_general(a, b, NT_DIMS, preferred_element_type=F32, **kw)


def _sigmoid(x):
    return 1.0 / (1.0 + jnp.exp(-x))


def _rms_scale(x):
    return x * lax.rsqrt(jnp.mean(x * x, axis=-1, keepdims=True) + EPS)


def _in_proj_kernel(x_ref, g_ref, w_ref, o_ref, xn_ref):
    @pl.when(pl.program_id(1) == 0)
    def _():
        xn_ref[...] = (_rms_scale(x_ref[...]) * g_ref[...]).astype(BF16)

    o_ref[...] = _dot(xn_ref[...], w_ref[...])


def _in_proj(h2d, g, w):
    m, k = h2d.shape
    n = w.shape[1]
    tm = min(1024, m)
    tn = 640
    return pl.pallas_call(
        _in_proj_kernel,
        grid=(m // tm, n // tn),
        in_specs=[pl.BlockSpec((tm, k), lambda i, j: (i, 0)),
                  pl.BlockSpec((1, k), lambda i, j: (0, 0)),
                  pl.BlockSpec((k, tn), lambda i, j: (0, j))],
        out_specs=pl.BlockSpec((tm, tn), lambda i, j: (i, j)),
        out_shape=jax.ShapeDtypeStruct((m, n), F32),
        scratch_shapes=[pltpu.VMEM((tm, k), BF16)],
        compiler_params=_params(("parallel", "arbitrary")),
        name="in_proj",
    )(h2d, g, w)


def _compress_kernel(x_ref, pea_ref, peb_ref, wa_ref, wb_ref, o_ref, acca_ref, accb_ref, *, nc):
    k = pl.program_id(1)

    @pl.when(k == 0)
    def _():
        acca_ref[...] = jnp.zeros_like(acca_ref)
        accb_ref[...] = jnp.zeros_like(accb_ref)

    x = x_ref[...]
    acca_ref[...] += _dot((x + pea_ref[...]).astype(BF16), wa_ref[...])
    accb_ref[...] += _dot((x + peb_ref[...]).astype(BF16), wb_ref[...])

    @pl.when(k == pl.num_programs(1) - 1)
    def _():
        res = acca_ref[...] + pltpu.roll(accb_ref[...], nc - 1, 0)
        for idx in range(2 * NSA_KV_HEADS):
            o_ref[idx] = res[:, idx * DK:(idx + 1) * DK]


def _compress(xr, pea, peb, wa, wb):
    b, nc, kk = xr.shape
    tk = 2048
    return pl.pallas_call(
        functools.partial(_compress_kernel, nc=nc),
        grid=(b, kk // tk),
        in_specs=[pl.BlockSpec((None, nc, tk), lambda i, k: (i, 0, k)),
                  pl.BlockSpec((1, tk), lambda i, k: (0, k)),
                  pl.BlockSpec((1, tk), lambda i, k: (0, k)),
                  pl.BlockSpec((tk, 2 * NSA_KV_WIDTH), lambda i, k: (k, 0)),
                  pl.BlockSpec((tk, 2 * NSA_KV_WIDTH), lambda i, k: (k, 0))],
        out_specs=pl.BlockSpec((None, 2 * NSA_KV_HEADS, nc, DK), lambda i, k: (i, 0, 0, 0)),
        out_shape=jax.ShapeDtypeStruct((b, 2 * NSA_KV_HEADS, nc, DK), F32),
        scratch_shapes=[pltpu.VMEM((nc, 2 * NSA_KV_WIDTH), F32),
                        pltpu.VMEM((nc, 2 * NSA_KV_WIDTH), F32)],
        compiler_params=_params(("parallel", "arbitrary")),
        name="nsa_compress",
    )(xr, pea, peb, wa, wb)


def _cmp_attn_kernel(q_ref, kv_ref, ocmp_ref, sel_ref, score_ref, *, tq, nc, nslc, nsel):
    t0 = pl.program_id(1) * tq
    row = lax.broadcasted_iota(jnp.int32, (tq, nc), 0) + t0
    col = lax.broadcasted_iota(jnp.int32, (tq, nc), 1)
    d = (row - (col * CMP_STRIDE + (CMP_LEN - 1))).astype(F32)
    valid = d >= 0.0
    jj = lax.broadcasted_iota(jnp.int32, (nslc, nc), 0) * SLC_BLOCK
    nn = lax.broadcasted_iota(jnp.int32, (nslc, nc), 1) * CMP_STRIDE
    overlap_t = jnp.where((nn < jj + SLC_BLOCK) & (jj < nn + CMP_LEN), 1.0, 0.0)
    blk = lax.broadcasted_iota(jnp.int32, (nslc, tq), 0)
    t_blk = (lax.broadcasted_iota(jnp.int32, (nslc, tq), 1) + t0) >> SLC_SHIFT
    forced = (blk == 0) | (blk == t_blk) | (blk == t_blk - 1)
    future = blk > t_blk

    for g in range(NSA_KV_HEADS):
        kc = kv_ref[g].astype(BF16)
        vc = kv_ref[NSA_KV_HEADS + g].astype(BF16)
        psum = jnp.zeros((tq, nc), F32)
        for h in range(HPG):
            hh = g * HPG + h
            qh = (q_ref[:, hh * DK:(hh + 1) * DK] * DK ** -0.5).astype(BF16)
            s = _dot_nt(qh, kc) - SLOPES[hh] * d
            s = jnp.where(valid, s, NEG_INF)
            e = jnp.exp(s - jnp.max(s, axis=-1, keepdims=True))
            p = jnp.where(valid, e / jnp.sum(e, axis=-1, keepdims=True), 0.0)
            ocmp_ref[:, hh * DK:(hh + 1) * DK] = _dot(p.astype(BF16), vc)
            psum = psum + p
        imp_t = _dot_nt(overlap_t, psum, precision=lax.Precision.HIGHEST)
        score = jnp.where(future, -1.0, jnp.where(forced, 1e4, imp_t))
        score_ref[...] = score

        def rank_body(i, rank):
            si = score_ref[pl.ds(i, 1), :]
            before = (si > score) | ((blk > i) & (si >= score))
            return rank + jnp.where(before, 1.0, 0.0)

        rank = lax.fori_loop(0, nslc, rank_body, jnp.zeros((nslc, tq), F32))
        sel_ref[g] = jnp.where(rank < float(nsel), 1.0, 0.0).T


def _cmp_attn(proj, cmp_kv, bsz, seq):
    tq = 128
    nc = cmp_kv.shape[2]
    nslc = seq // SLC_BLOCK
    nsel = min(N_SELECT, nslc)
    nq = seq // tq
    return pl.pallas_call(
        functools.partial(_cmp_attn_kernel, tq=tq, nc=nc, nslc=nslc, nsel=nsel),
        grid=(bsz, nq),
        in_specs=[pl.BlockSpec((tq, NSA_WIDTH), lambda b, i: (b * nq + i, 0)),
                  pl.BlockSpec((None, 2 * NSA_KV_HEADS, nc, DK), lambda b, i: (b, 0, 0, 0))],
        out_specs=[pl.BlockSpec((tq, NSA_WIDTH), lambda b, i: (b * nq + i, 0)),
                   pl.BlockSpec((NSA_KV_HEADS, tq, nslc), lambda b, i: (0, b * nq + i, 0))],
        out_shape=[jax.ShapeDtypeStruct((bsz * seq, NSA_WIDTH), F32),
                   jax.ShapeDtypeStruct((NSA_KV_HEADS, bsz * seq, nslc), F32)],
        scratch_shapes=[pltpu.VMEM((nslc, tq), F32)],
        compiler_params=_params(("parallel", "arbitrary")),
        name="nsa_cmp_attn",
    )(proj, cmp_kv)


def _flash_kernel(*refs, mode, tq, tk, nslc):
    if mode == "sel":
        slopes_ref, q_ref, k_ref, v_ref, sel_ref, o_ref, qs_ref, m_ref, l_ref, acc_ref = refs
    else:
        slopes_ref, q_ref, k_ref, v_ref, o_ref, qs_ref, m_ref, l_ref, acc_ref = refs
    gp = pl.program_id(1)
    qi = pl.program_id(2)
    t0 = qi * tq
    row_t = lax.broadcasted_iota(jnp.int32, (tq, tk), 0) + t0
    col_i = lax.broadcasted_iota(jnp.int32, (tq, tk), 1)

    for gi in range(2):
        for h in range(HPG):
            c0 = (gi * HPG + h) * DK
            qs_ref[h * tq:(h + 1) * tq, :] = (q_ref[:, c0:c0 + DK] * DK ** -0.5).astype(BF16)
        m_ref[...] = jnp.full(m_ref.shape, NEG_INF, F32)
        l_ref[...] = jnp.zeros(l_ref.shape, F32)
        acc_ref[...] = jnp.zeros(acc_ref.shape, F32)
        if mode == "sel":
            sel_g = sel_ref[gi].astype(BF16)

        def body(kj, carry, gi=gi):
            j0 = pl.multiple_of(kj * tk, tk)
            kb = k_ref[pl.ds(j0, tk), gi * DK:(gi + 1) * DK].astype(BF16)
            vb = v_ref[pl.ds(j0, tk), gi * DK:(gi + 1) * DK].astype(BF16)
            s = _dot_nt(qs_ref[...], kb)
            col_j = col_i + j0
            if mode == "sel":
                eb = lax.broadcasted_iota(jnp.int32, (nslc, tk), 0)
                ek = (lax.broadcasted_iota(jnp.int32, (nslc, tk), 1) + j0) >> SLC_SHIFT
                expand = jnp.where(eb == ek, 1.0, 0.0).astype(BF16)
                mask = (_dot(sel_g, expand) > 0.5) & (col_j <= row_t)
            else:
                dd = row_t - col_j
                mask = (dd >= 0) & (dd < WINDOW)
            rel = (lax.broadcasted_iota(jnp.int32, (1, tk), 1) + (j0 - t0)).astype(F32)
            for h in range(HPG):
                slope = slopes_ref[(gp * 2 + gi) * HPG + h]
                rows = slice(h * tq, (h + 1) * tq)
                sh = jnp.where(mask, s[rows] + slope * rel, NEG_INF)
                m_old = m_ref[rows]
                m_new = jnp.maximum(m_old, jnp.max(sh, axis=-1, keepdims=True))
                p = jnp.where(mask, jnp.exp(sh - m_new), 0.0)
                alpha = jnp.exp(m_old - m_new)
                l_ref[rows] = alpha * l_ref[rows] + jnp.sum(p, axis=-1, keepdims=True)
                acc_ref[rows] = alpha * acc_ref[rows] + _dot(p.astype(BF16), vb)
                m_ref[rows] = m_new
            return carry

        lo = 0 if mode == "sel" else jnp.maximum(qi - WINDOW // tk, 0)
        lax.fori_loop(lo, qi + 1, body, 0)
        for h in range(HPG):
            c0 = (gi * HPG + h) * DK
            rows = slice(h * tq, (h + 1) * tq)
            o_ref[:, c0:c0 + DK] = acc_ref[rows] / l_ref[rows]


def _flash(proj, sel, slopes, bsz, seq, mode):
    tq = tk = 128
    nq = seq // tq
    nslc = seq // SLC_BLOCK
    ncol = 128
    kcol = (COL_KV + (2 if mode == "sel" else 4) * NSA_KV_WIDTH) // ncol
    vcol = kcol + NSA_KV_WIDTH // ncol
    in_specs = [pl.BlockSpec(memory_space=pltpu.SMEM),
                pl.BlockSpec((tq, 2 * HPG * DK), lambda b, g, i: (b * nq + i, g)),
                pl.BlockSpec((seq, ncol), lambda b, g, i: (b, kcol + g)),
                pl.BlockSpec((seq, ncol), lambda b, g, i: (b, vcol + g))]
    args = [slopes, proj, proj, proj]
    if mode == "sel":
        in_specs.append(pl.BlockSpec((2, tq, nslc), lambda b, g, i: (g, b * nq + i, 0)))
        args.append(sel)
    return pl.pallas_call(
        functools.partial(_flash_kernel, mode=mode, tq=tq, tk=tk, nslc=nslc),
        grid=(bsz, 2, nq),
        in_specs=in_specs,
        out_specs=pl.BlockSpec((tq, 2 * HPG * DK), lambda b, g, i: (b * nq + i, g)),
        out_shape=jax.ShapeDtypeStruct((bsz * seq, NSA_WIDTH), F32),
        scratch_shapes=[pltpu.VMEM((HPG * tq, DK), BF16),
                        pltpu.VMEM((HPG * tq, 1), F32),
                        pltpu.VMEM((HPG * tq, 1), F32),
                        pltpu.VMEM((HPG * tq, DK), F32)],
        compiler_params=_params(("parallel", "parallel", "arbitrary")),
        name="nsa_" + mode,
    )(*args)


def _lru_kernel(x_ref, y_ref, cw_ref, cb_ref, wa_ref, ba_ref, wi_ref, bi_ref, lam_ref, o_ref,
                tail_ref, h_ref, *, tt):
    @pl.when(pl.program_id(1) == 0)
    def _():
        tail_ref[...] = jnp.zeros_like(tail_ref)
        h_ref[...] = jnp.zeros_like(h_ref)

    x = x_ref[...]
    ext = jnp.concatenate([tail_ref[...], x], axis=0)
    xc = cb_ref[...] + x * cw_ref[CONV_WIDTH - 1:CONV_WIDTH, :]
    for k in range(CONV_WIDTH - 1):
        back = CONV_WIDTH - 1 - k
        xc = xc + ext[8 - back:8 - back + tt, :] * cw_ref[k:k + 1, :]
    tail_ref[...] = x[tt - 8:, :]

    xcb = xc.astype(BF16)
    r = _sigmoid(_dot(xcb, wa_ref[...]) + ba_ref[...])
    gate_i = _sigmoid(_dot(xcb, wi_ref[...]) + bi_ref[...])
    neg_lam = -lam_ref[...]
    softplus = jnp.maximum(neg_lam, 0.0) + jnp.log1p(jnp.exp(-jnp.abs(neg_lam)))
    log_a = -LRU_C * r * softplus
    a = jnp.exp(log_a)
    u = jnp.sqrt(jnp.tanh(-log_a) * (a * a + 1.0)) * (gate_i * xc)

    rows = lax.broadcasted_iota(jnp.int32, (tt, 1), 0)
    step = 1
    while step < tt:
        keep = rows >= step
        a_prev = jnp.where(keep, pltpu.roll(a, step, 0), 1.0)
        u_prev = jnp.where(keep, pltpu.roll(u, step, 0), 0.0)
        u = u + a * u_prev
        a = a * a_prev
        step *= 2
    hs = u + a * h_ref[...]
    h_ref[...] = hs[tt - 1:tt, :]
    o_ref[...] = hs * jax.nn.gelu(y_ref[...], approximate=True)


def _lru(proj, cw, cb, wa, ba, wi, bi, lam, bsz, seq):
    tt = min(512, seq)
    nt = seq // tt
    w = LRU_WIDTH
    xcol = COL_LRU_X // w
    ycol = COL_LRU_Y // w
    vec = pl.BlockSpec((1, w), lambda b, i: (0, 0))
    mat = pl.BlockSpec((w, w), lambda b, i: (0, 0))
    return pl.pallas_call(
        functools.partial(_lru_kernel, tt=tt),
        grid=(bsz, nt),
        in_specs=[pl.BlockSpec((tt, w), lambda b, i: (b * nt + i, xcol)),
                  pl.BlockSpec((tt, w), lambda b, i: (b * nt + i, ycol)),
                  pl.BlockSpec((CONV_WIDTH, w), lambda b, i: (0, 0)),
                  vec, mat, vec, mat, vec, vec],
        out_specs=pl.BlockSpec((tt, w), lambda b, i: (b * nt + i, 0)),
        out_shape=jax.ShapeDtypeStruct((bsz * seq, w), F32),
        scratch_shapes=[pltpu.VMEM((8, w), F32), pltpu.VMEM((1, w), F32)],
        compiler_params=_params(("parallel", "arbitrary")),
        name="rglru",
    )(proj, proj, cw, cb, wa, ba, wi, bi, lam)


def _gla_kernel(q_ref, k_ref, v_ref, r_ref, gz_ref, w2_ref, bg_ref, ng_ref, o_ref, st_ref, *, tt):
    c = GLA_CHUNK
    dh = GLA_HEAD_DIM

    @pl.when(pl.program_id(1) == 0)
    def _():
        st_ref[...] = jnp.zeros_like(st_ref)

    gate = _dot(gz_ref[...].astype(BF16), w2_ref[...]) + bg_ref[...]
    log_alpha = (jnp.minimum(gate, 0.0) - jnp.log1p(jnp.exp(-jnp.abs(gate)))) / GLA_GATE_TAU
    ri = lax.broadcasted_iota(jnp.int32, (tt, tt), 0)
    ci = lax.broadcasted_iota(jnp.int32, (tt, tt), 1)
    tri = jnp.where((ri >> CHUNK_SHIFT == ci >> CHUNK_SHIFT) & (ci <= ri), 1.0, 0.0)
    bcum_all = _dot(tri, log_alpha, precision=lax.Precision.HIGHEST)
    causal = (lax.broadcasted_iota(jnp.int32, (c, c), 1) <= lax.broadcasted_iota(jnp.int32, (c, c), 0))

    for n in range(tt // c):
        rs = slice(n * c, (n + 1) * c)
        bcum = bcum_all[rs]
        b_last = bcum[c - 1:c, :]
        e_pos = jnp.exp(bcum)
        q_t = (q_ref[rs, :] * dh ** -0.5) * e_pos
        kk = k_ref[rs, :]
        k_t = kk * jnp.exp(-bcum)
        k_end = kk * jnp.exp(b_last - bcum)
        decay = jnp.exp(b_last)
        vv = v_ref[rs, :]
        for hh in range(GLA_HEADS):
            cs = slice(hh * dh, (hh + 1) * dh)
            qh = q_t[:, cs].astype(BF16)
            vh = vv[:, cs]
            att = jnp.where(causal, _dot_nt(qh, k_t[:, cs].astype(BF16)), 0.0)
            st = st_ref[hh]
            o = _dot(att.astype(BF16), vh.astype(BF16)) + _dot_nt(qh, st.astype(BF16))
            st_ref[hh] = st * decay[:, cs] + _dot(vh.T.astype(BF16), k_end[:, cs].astype(BF16))
            o = _rms_scale(o) * ng_ref[:, cs]
            rr = r_ref[rs, cs]
            o_ref[rs, cs] = o * (rr * _sigmoid(rr))


def _gla(proj, w2pad, bg, ng, bsz, seq):
    tt = min(256, seq)
    nt = seq // tt
    w = GLA_WIDTH
    cols = [COL_GLA_Q // w, COL_GLA_K // w, COL_GLA_V // w, COL_GLA_R // w]
    seg = [pl.BlockSpec((tt, w), functools.partial(lambda b, i, cc: (b * nt + i, cc), cc=cc)) for cc in cols]
    vec = pl.BlockSpec((1, w), lambda b, i: (0, 0))
    return pl.pallas_call(
        functools.partial(_gla_kernel, tt=tt),
        grid=(bsz, nt),
        in_specs=seg + [pl.BlockSpec((tt, 128), lambda b, i: (b * nt + i, COL_GZ // 128)),
                        pl.BlockSpec((128, w), lambda b, i: (0, 0)), vec, vec],
        out_specs=pl.BlockSpec((tt, w), lambda b, i: (b * nt + i, 0)),
        out_shape=jax.ShapeDtypeStruct((bsz * seq, w), F32),
        scratch_shapes=[pltpu.VMEM((GLA_HEADS, GLA_HEAD_DIM, GLA_HEAD_DIM), F32)],
        compiler_params=_params(("parallel", "arbitrary")),
        name="gla",
    )(proj, proj, proj, proj, proj, w2pad, bg, ng)


def _out_proj_kernel(ocmp_ref, osel_ref, owin_ref, gz_ref, olru_ref, ogla_ref, h_ref, e_ref, w_ref,
                     g_ref, o_ref):
    sg = _sigmoid(gz_ref[...])
    nsa = None
    for c, o_c in enumerate((ocmp_ref, osel_ref, owin_ref)):
        term = _dot(sg, e_ref[c], precision=lax.Precision.HIGHEST) * o_c[...]
        nsa = term if nsa is None else nsa + term
    y = _dot(nsa.astype(BF16), w_ref[0:NSA_WIDTH, :])
    y = y + _dot(olru_ref[...].astype(BF16), w_ref[NSA_WIDTH:NSA_WIDTH + LRU_WIDTH, :])
    y = y + _dot(ogla_ref[...].astype(BF16), w_ref[NSA_WIDTH + LRU_WIDTH:, :])
    o_ref[...] = h_ref[...] + _rms_scale(y) * g_ref[...]


def _out_proj(ocmp, osel, owin, proj, olru, ogla, h2d, expand, w, g):
    m = h2d.shape[0]
    tm = min(256, m)
    row = lambda width, col=0: pl.BlockSpec((tm, width), lambda i: (i, col))
    return pl.pallas_call(
        _out_proj_kernel,
        grid=(m // tm,),
        in_specs=[row(NSA_WIDTH), row(NSA_WIDTH), row(NSA_WIDTH), row(128, COL_GZ // 128),
                  row(LRU_WIDTH), row(GLA_WIDTH), row(D_MODEL),
                  pl.BlockSpec((3, 128, NSA_WIDTH), lambda i: (0, 0, 0)),
                  pl.BlockSpec((D_MODEL, D_MODEL), lambda i: (0, 0)),
                  pl.BlockSpec((1, D_MODEL), lambda i: (0, 0))],
        out_specs=row(D_MODEL),
        out_shape=jax.ShapeDtypeStruct((m, D_MODEL), F32),
        compiler_params=_params(("parallel",)),
        name="out_proj",
    )(ocmp, osel, owin, proj, olru, ogla, h2d, expand, w, g)


def _mlp_kernel(h_ref, gpre_ref, wu_ref, wd_ref, gpost_ref, o_ref, un_ref, acc_ref):
    f = pl.program_id(1)

    @pl.when(f == 0)
    def _():
        un_ref[...] = (_rms_scale(h_ref[...]) * gpre_ref[...]).astype(BF16)
        acc_ref[...] = jnp.zeros_like(acc_ref)

    a = jnp.maximum(_dot(un_ref[...], wu_ref[...]), 0.0)
    acc_ref[...] += _dot((a * a).astype(BF16), wd_ref[...])

    @pl.when(f == pl.num_programs(1) - 1)
    def _():
        o_ref[...] = h_ref[...] + _rms_scale(acc_ref[...]) * gpost_ref[...]


def _mlp(h2d, gpre, wu, wd, gpost):
    m = h2d.shape[0]
    tm = min(512, m)
    tf = 512
    vec = pl.BlockSpec((1, D_MODEL), lambda i, f: (0, 0))
    return pl.pallas_call(
        _mlp_kernel,
        grid=(m // tm, D_FF // tf),
        in_specs=[pl.BlockSpec((tm, D_MODEL), lambda i, f: (i, 0)), vec,
                  pl.BlockSpec((D_MODEL, tf), lambda i, f: (0, f)),
                  pl.BlockSpec((tf, D_MODEL), lambda i, f: (f, 0)), vec],
        out_specs=pl.BlockSpec((tm, D_MODEL), lambda i, f: (i, 0)),
        out_shape=jax.ShapeDtypeStruct((m, D_MODEL), F32),
        scratch_shapes=[pltpu.VMEM((tm, D_MODEL), BF16), pltpu.VMEM((tm, D_MODEL), F32)],
        compiler_params=_params(("parallel", "arbitrary")),
        name="mlp",
    )(h2d, gpre, wu, wd, gpost)


def _ple_kernel(h_ref, p_ref, wg_ref, wp_ref, o_ref):
    h = h_ref[...]
    gate = _sigmoid(_dot(h.astype(BF16), wg_ref[...]))
    o_ref[...] = h + gate * _dot(p_ref[...].astype(BF16), wp_ref[...])


def _ple(h2d, p2d, wg, wp):
    m = h2d.shape[0]
    tm = min(512, m)
    return pl.pallas_call(
        _ple_kernel,
        grid=(m // tm,),
        in_specs=[pl.BlockSpec((tm, D_MODEL), lambda i: (i, 0)),
                  pl.BlockSpec((tm, PLE_DIM), lambda i: (i, 0)),
                  pl.BlockSpec((D_MODEL, D_MODEL), lambda i: (0, 0)),
                  pl.BlockSpec((PLE_DIM, D_MODEL), lambda i: (0, 0))],
        out_specs=pl.BlockSpec((tm, D_MODEL), lambda i: (i, 0)),
        out_shape=jax.ShapeDtypeStruct((m, D_MODEL), F32),
        compiler_params=_params(("parallel",)),
        name="ple",
    )(h2d, p2d, wg, wp)


def _permute_w_in(w):
    g0 = NSA_WIDTH + 6 * NSA_KV_WIDTH
    g1 = g0 + N_GATES
    z0 = g1 + 2 * LRU_WIDTH + 4 * GLA_WIDTH
    pad = jnp.zeros((w.shape[0], D_IN_PAD - w.shape[1]), w.dtype)
    return jnp.concatenate([w[:, :g0], w[:, g1:z0], w[:, g0:g1], w[:, z0:], pad], axis=1).astype(BF16)


def _compress_weights(cmp_w, cmp_pe):
    w4 = cmp_w.reshape(2, CMP_LEN, DK, DK)
    eye_c = jnp.eye(2, dtype=cmp_w.dtype)
    eye_g = jnp.eye(NSA_KV_HEADS, dtype=cmp_w.dtype)
    big = jnp.einsum("clde,cx,gy->lcgdxye", w4, eye_c, eye_g)
    big = big.reshape(CMP_LEN * 2 * NSA_KV_WIDTH, 2 * NSA_KV_WIDTH).astype(BF16)
    half = CMP_STRIDE * 2 * NSA_KV_WIDTH
    pe = jnp.broadcast_to(cmp_pe.transpose(1, 0, 2)[:, :, None, :], (CMP_LEN, 2, NSA_KV_HEADS, DK))
    pe = pe.reshape(1, CMP_LEN * 2 * NSA_KV_WIDTH)
    return big[:half], big[half:], pe[:, :half], pe[:, half:]


def _block_diag(w):
    eye = jnp.eye(LRU_BLOCKS, dtype=w.dtype)
    return jnp.einsum("ncd,nm->ncmd", w, eye).reshape(LRU_WIDTH, LRU_WIDTH).astype(BF16)


def _gate_expand():
    e = np.zeros((3, 128, NSA_WIDTH), np.float32)
    for hh in range(NSA_HEADS):
        for c in range(3):
            e[c, hh * 3 + c, hh * DK:(hh + 1) * DK] = 1.0
    return jnp.asarray(e)


def _layer(h2d, p2d, bsz, seq, norm_mix_pre, w_in, nsa_cmp_w, nsa_cmp_pe, lru_conv_w, lru_conv_b,
           lru_wa, lru_ba, lru_wi, lru_bi, lru_lambda, gla_w_gate2, gla_b_gate, gla_norm,
           w_out, norm_mix_post, norm_mlp_pre, w_up, w_down, norm_mlp_post, w_ple_gate, w_ple):
    row = lambda v: v.reshape(1, -1)
    proj = _in_proj(h2d, row(norm_mix_pre), _permute_w_in(w_in))

    nc = seq // CMP_STRIDE
    cmp_in = proj[:, COL_KV:COL_KV + 2 * NSA_KV_WIDTH].reshape(bsz, nc, CMP_STRIDE * 2 * NSA_KV_WIDTH)
    wa, wb, pea, peb = _compress_weights(nsa_cmp_w, nsa_cmp_pe)
    cmp_kv = _compress(cmp_in, pea, peb, wa, wb)
    o_cmp, sel = _cmp_attn(proj, cmp_kv, bsz, seq)
    slopes = jnp.asarray(SLOPES, F32)
    o_sel = _flash(proj, sel, slopes, bsz, seq, "sel")
    o_win = _flash(proj, None, slopes, bsz, seq, "win")

    o_lru = _lru(proj, lru_conv_w, row(lru_conv_b), _block_diag(lru_wa), row(lru_ba),
                 _block_diag(lru_wi), row(lru_bi), row(lru_lambda), bsz, seq)

    w2pad = jnp.zeros((128, GLA_WIDTH), F32).at[N_GATES:N_GATES + GLA_GATE_RANK].set(gla_w_gate2)
    o_gla = _gla(proj, w2pad.astype(BF16), row(gla_b_gate), row(gla_norm), bsz, seq)

    h2d = _out_proj(o_cmp, o_sel, o_win, proj, o_lru, o_gla, h2d, _gate_expand(),
                    w_out.astype(BF16), row(norm_mix_post))
    h2d = _mlp(h2d, row(norm_mlp_pre), w_up.astype(BF16), w_down.astype(BF16), row(norm_mlp_post))
    return _ple(h2d, p2d, w_ple_gate.astype(BF16), w_ple.astype(BF16))


def kernel(x, p, norm_mix_pre, w_in, nsa_cmp_w, nsa_cmp_pe, lru_conv_w, lru_conv_b, lru_wa, lru_ba,
           lru_wi, lru_bi, lru_lambda, gla_w_gate2, gla_b_gate, gla_norm, w_out, norm_mix_post,
           norm_mlp_pre, w_up, w_down, norm_mlp_post, w_ple_gate, w_ple):
    bsz, seq, _ = x.shape
    h2d = x.reshape(bsz * seq, D_MODEL)
    weights = (norm_mix_pre, w_in, nsa_cmp_w, nsa_cmp_pe, lru_conv_w, lru_conv_b, lru_wa, lru_ba,
               lru_wi, lru_bi, lru_lambda, gla_w_gate2, gla_b_gate, gla_norm, w_out, norm_mix_post,
               norm_mlp_pre, w_up, w_down, norm_mlp_post, w_ple_gate, w_ple)
    for i in range(p.shape[0]):
        h2d = _layer(h2d, p[i].reshape(bsz * seq, PLE_DIM), bsz, seq, *(w[i] for w in weights))
    return h2d.reshape(bsz, seq, D_MODEL)
```

```python
import functools

import numpy as np
import jax
import jax.numpy as jnp
from jax import lax
from jax.experimental import pallas as pl
from jax.experimental.pallas import tpu as pltpu

F32 = jnp.float32
BF16 = jnp.bfloat16

D_MODEL = 2048
PLE_DIM = 256
NSA_HEADS = 16
NSA_KV_HEADS = 4
HPG = NSA_HEADS // NSA_KV_HEADS
NSA_WIDTH = 1024
DK = 64
NSA_KV_WIDTH = NSA_KV_HEADS * DK
CMP_LEN = 32
CMP_STRIDE = 16
SLC_BLOCK = 64
SLC_SHIFT = 6
N_SELECT = 16
WINDOW = 512
LRU_WIDTH = 512
LRU_BLOCKS = 8
LRU_BLOCK_DIM = 64
CONV_WIDTH = 4
LRU_C = 8.0
GLA_WIDTH = 512
GLA_HEADS = 4
GLA_HEAD_DIM = 128
GLA_GATE_RANK = 16
GLA_GATE_TAU = 16.0
GLA_CHUNK = 64
CHUNK_SHIFT = 6
D_FF = 4 * D_MODEL
EPS = 1e-6
NEG_INF = -1e30
MASKED = 2.0 * NEG_INF

COL_Q = 0
COL_KV = 1024
COL_LRU_X = 2560
COL_LRU_Y = 3072
COL_GLA_Q = 3584
COL_GLA_K = 4096
COL_GLA_V = 4608
COL_GLA_R = 5120
COL_GZ = 5632
D_IN_PAD = 5760
N_GATES = 3 * NSA_HEADS

SLOPES = tuple(2.0 ** (-8.0 * i / NSA_HEADS) for i in range(1, NSA_HEADS + 1))

VMEM_LIMIT = 56 * 1024 * 1024

NT_DIMS = (((1,), (1,)), ((), ()))


def _params(sem):
    return pltpu.CompilerParams(dimension_semantics=sem, vmem_limit_bytes=VMEM_LIMIT)


def _dot(a, b, **kw):
    return jnp.dot(a, b, preferred_element_type=F32, **kw)


def _dot_nt(a, b, **kw):
    return lax.dot_general(a, b, NT_DIMS, preferred_element_type=F32, **kw)


def _sigmoid(x):
    return 1.0 / (1.0 + jnp.exp(-x))


def _rms_scale(x):
    return x * lax.rsqrt(jnp.mean(x * x, axis=-1, keepdims=True) + EPS)


def _in_proj_kernel(x_ref, g_ref, w_ref, o_ref, xn_ref):
    @pl.when(pl.program_id(1) == 0)
    def _():
        xn_ref[...] = (_rms_scale(x_ref[...]) * g_ref[...]).astype(BF16)

    o_ref[...] = _dot(xn_ref[...], w_ref[...])


def _in_proj(h2d, g, w):
    m, k = h2d.shape
    n = w.shape[1]
    tm = min(1024, m)
    tn = 640
    return pl.pallas_call(
        _in_proj_kernel,
        grid=(m // tm, n // tn),
        in_specs=[pl.BlockSpec((tm, k), lambda i, j: (i, 0)),
                  pl.BlockSpec((1, k), lambda i, j: (0, 0)),
                  pl.BlockSpec((k, tn), lambda i, j: (0, j))],
        out_specs=pl.BlockSpec((tm, tn), lambda i, j: (i, j)),
        out_shape=jax.ShapeDtypeStruct((m, n), F32),
        scratch_shapes=[pltpu.VMEM((tm, k), BF16)],
        compiler_params=_params(("parallel", "arbitrary")),
        name="in_proj",
    )(h2d, g, w)


def _compress_kernel(x_ref, pea_ref, peb_ref, wa_ref, wb_ref, ok_ref, ov_ref, acca_ref, accb_ref, *, nc):
    k = pl.program_id(1)

    @pl.when(k == 0)
    def _():
        acca_ref[...] = jnp.zeros_like(acca_ref)
        accb_ref[...] = jnp.zeros_like(accb_ref)

    x = x_ref[...]
    acca_ref[...] += _dot((x + pea_ref[...]).astype(BF16), wa_ref[...])
    accb_ref[...] += _dot((x + peb_ref[...]).astype(BF16), wb_ref[...])

    @pl.when(k == pl.num_programs(1) - 1)
    def _():
        res = acca_ref[...] + pltpu.roll(accb_ref[...], nc - 1, 0)
        for g in range(NSA_KV_HEADS):
            ok_ref[g] = res[:, g * DK:(g + 1) * DK]
        ov_ref[...] = res[:, NSA_KV_WIDTH:]


def _compress(xr, pea, peb, wa, wb):
    b, nc, kk = xr.shape
    tk = 2048
    return pl.pallas_call(
        functools.partial(_compress_kernel, nc=nc),
        grid=(b, kk // tk),
        in_specs=[pl.BlockSpec((None, nc, tk), lambda i, k: (i, 0, k)),
                  pl.BlockSpec((1, tk), lambda i, k: (0, k)),
                  pl.BlockSpec((1, tk), lambda i, k: (0, k)),
                  pl.BlockSpec((tk, 2 * NSA_KV_WIDTH), lambda i, k: (k, 0)),
                  pl.BlockSpec((tk, 2 * NSA_KV_WIDTH), lambda i, k: (k, 0))],
        out_specs=[pl.BlockSpec((None, NSA_KV_HEADS, nc, DK), lambda i, k: (i, 0, 0, 0)),
                   pl.BlockSpec((None, nc, NSA_KV_WIDTH), lambda i, k: (i, 0, 0))],
        out_shape=[jax.ShapeDtypeStruct((b, NSA_KV_HEADS, nc, DK), F32),
                   jax.ShapeDtypeStruct((b, nc, NSA_KV_WIDTH), F32)],
        scratch_shapes=[pltpu.VMEM((nc, 2 * NSA_KV_WIDTH), F32),
                        pltpu.VMEM((nc, 2 * NSA_KV_WIDTH), F32)],
        compiler_params=_params(("parallel", "arbitrary")),
        name="nsa_compress",
    )(xr, pea, peb, wa, wb)


def _cmp_attn_kernel(q_ref, kc_ref, vc_ref, ocmp_ref, selt_ref, vct_ref, *, tq, nc, nslc, nsel):
    t0 = pl.program_id(1) * tq

    @pl.when(pl.program_id(1) == 0)
    def _():
        for gp in range(NSA_KV_HEADS // 2):
            vct_ref[gp] = vc_ref[:, gp * 2 * DK:(gp + 1) * 2 * DK].T.astype(BF16)

    n_row = lax.broadcasted_iota(jnp.int32, (nc, tq), 0)
    t_col = lax.broadcasted_iota(jnp.int32, (nc, tq), 1) + t0
    d = (t_col - (n_row * CMP_STRIDE + (CMP_LEN - 1))).astype(F32)
    valid = d >= 0.0
    jj = lax.broadcasted_iota(jnp.int32, (nslc, nc), 0) * SLC_BLOCK
    nn = lax.broadcasted_iota(jnp.int32, (nslc, nc), 1) * CMP_STRIDE
    overlap_t = jnp.where((nn < jj + SLC_BLOCK) & (jj < nn + CMP_LEN), 1.0, 0.0)
    blk = lax.broadcasted_iota(jnp.int32, (nslc, tq), 0)
    t_blk = (lax.broadcasted_iota(jnp.int32, (nslc, tq), 1) + t0) >> SLC_SHIFT
    forced = (blk == 0) | (blk == t_blk) | (blk == t_blk - 1)
    future = blk > t_blk
    sub = lax.broadcasted_iota(jnp.int32, (8, tq), 0)
    ngrp = nslc // 8

    for g in range(NSA_KV_HEADS):
        kc = kc_ref[g].astype(BF16)
        vct = vct_ref[g // 2, (g % 2) * DK:(g % 2 + 1) * DK, :]
        psum = jnp.zeros((nc, tq), F32)
        for h in range(HPG):
            hh = g * HPG + h
            qh = (q_ref[:, hh * DK:(hh + 1) * DK] * DK ** -0.5).astype(BF16)
            s = _dot_nt(kc, qh) - SLOPES[hh] * d
            s = jnp.where(valid, s, NEG_INF)
            e = jnp.exp(s - jnp.max(s, axis=0, keepdims=True))
            p = jnp.where(valid, e / jnp.sum(e, axis=0, keepdims=True), 0.0)
            ocmp_ref[:, hh * DK:(hh + 1) * DK] = _dot(vct, p.astype(BF16)).T
            psum = psum + p
        imp_t = _dot(overlap_t, psum, precision=lax.Precision.HIGHEST)
        score = jnp.where(future, -1.0, jnp.where(forced, 1e4, imp_t))

        sc = [score[8 * r:8 * r + 8] for r in range(ngrp)]
        rank = [jnp.zeros((8, tq), F32) for _ in range(ngrp)]
        for i in range(nslc):
            si = jnp.broadcast_to(score[i:i + 1, :], (8, tq))
            for r in range(ngrp):
                if r < i // 8:
                    before = si > sc[r]
                elif r > i // 8:
                    before = si >= sc[r]
                else:
                    before = (si > sc[r]) | ((sub > i % 8) & (si >= sc[r]))
                rank[r] = rank[r] + jnp.where(before, 1.0, 0.0)
        for r in range(ngrp):
            selt_ref[g, 8 * r:8 * r + 8, :] = jnp.where(rank[r] < float(nsel), 1.0, 0.0)


def _cmp_attn(proj, cmp_k, cmp_v, bsz, seq):
    tq = 128
    nc = cmp_k.shape[2]
    nslc = seq // SLC_BLOCK
    nsel = min(N_SELECT, nslc)
    nq = seq // tq
    return pl.pallas_call(
        functools.partial(_cmp_attn_kernel, tq=tq, nc=nc, nslc=nslc, nsel=nsel),
        grid=(bsz, nq),
        in_specs=[pl.BlockSpec((tq, NSA_WIDTH), lambda b, i: (b * nq + i, 0)),
                  pl.BlockSpec((None, NSA_KV_HEADS, nc, DK), lambda b, i: (b, 0, 0, 0)),
                  pl.BlockSpec((None, nc, NSA_KV_WIDTH), lambda b, i: (b, 0, 0))],
        out_specs=[pl.BlockSpec((tq, NSA_WIDTH), lambda b, i: (b * nq + i, 0)),
                   pl.BlockSpec((NSA_KV_HEADS, None, nslc, tq), lambda b, i: (0, b, 0, i))],
        out_shape=[jax.ShapeDtypeStruct((bsz * seq, NSA_WIDTH), F32),
                   jax.ShapeDtypeStruct((NSA_KV_HEADS, bsz, nslc, seq), F32)],
        scratch_shapes=[pltpu.VMEM((NSA_KV_HEADS // 2, 2 * DK, nc), BF16)],
        compiler_params=_params(("parallel", "arbitrary")),
        name="nsa_cmp_attn",
    )(proj, cmp_k, cmp_v)


def _flash_kernel(*refs, mode, tq, ts, nsub, nk):
    if mode == "sel":
        slopes_ref, q_ref, k_ref, v_ref, selt_ref, o_ref = refs[:6]
    else:
        slopes_ref, q_ref, k_ref, v_ref, o_ref = refs[:5]
    kb_ref, vt_ref, qs_ref, bias_ref, s_ref, m_ref, l_ref, acc_ref = refs[-8:]
    gp = pl.program_id(1)
    qi = pl.program_id(2)
    t0 = qi * tq
    nb = ts // SLC_BLOCK

    @pl.when(qi == 0)
    def _():
        for c in range(nk):
            blk_k = k_ref[c * ts:(c + 1) * ts, :]
            for gi in range(2):
                kb_ref[gi, c] = blk_k[:, gi * DK:(gi + 1) * DK].astype(BF16)
            vt_ref[c] = v_ref[c * ts:(c + 1) * ts, :].T.astype(BF16)

    key_i = lax.broadcasted_iota(jnp.int32, (ts, tq), 0)
    qry_t = lax.broadcasted_iota(jnp.int32, (ts, tq), 1) + t0
    key_f = key_i.astype(F32)
    for gi in range(2):
        for h in range(HPG):
            c0 = (gi * HPG + h) * DK
            qs_ref[gi, h * tq:(h + 1) * tq, :] = (q_ref[:, c0:c0 + DK] * DK ** -0.5).astype(BF16)
            bias_ref[gi * HPG + h] = slopes_ref[(gp * 2 + gi) * HPG + h] * key_f
    m_ref[...] = jnp.full(m_ref.shape, NEG_INF, F32)
    l_ref[...] = jnp.zeros(l_ref.shape, F32)
    acc_ref[...] = jnp.zeros(acc_ref.shape, F32)

    step_shift = (ts * nsub).bit_length() - 1
    last = (t0 + tq - 1) >> step_shift
    first = 0 if mode == "sel" else jnp.maximum(t0 - (WINDOW - 1), 0) >> step_shift

    def body(step, carry):
        for u in range(nsub):
            kj = (first + step) * nsub + u
            for gi in range(2):
                s_ref[u * 2 + gi] = _dot_nt(kb_ref[gi, kj], qs_ref[gi])
        pvs = [[], []]
        for u in range(nsub):
            kj = (first + step) * nsub + u
            j0 = kj * ts
            rel0 = (jnp.zeros((1, tq), jnp.int32) + (j0 - t0)).astype(F32)
            dd = qry_t - (key_i + j0)
            for gi in range(2):
                if mode == "sel":
                    chosen = selt_ref[gi, pl.ds(kj * nb + nb - 1, 1), :]
                    for c in range(nb - 2, -1, -1):
                        chosen = jnp.where(key_i < (c + 1) * SLC_BLOCK,
                                           selt_ref[gi, pl.ds(kj * nb + c, 1), :], chosen)
                    mask = (chosen > 0.5) & (dd >= 0)
                else:
                    mask = (dd >= 0) & (dd < WINDOW)
                ps = []
                alphas = []
                for h in range(HPG):
                    shift = slopes_ref[(gp * 2 + gi) * HPG + h] * rel0
                    cols = slice(h * tq, (h + 1) * tq)
                    x = jnp.where(mask, s_ref[u * 2 + gi, :, cols] + bias_ref[gi * HPG + h], MASKED)
                    m_old = m_ref[gi, :, cols]
                    m_new = jnp.maximum(m_old, jnp.max(x, axis=0, keepdims=True) + shift)
                    p = jnp.exp(x - (m_new - shift))
                    alpha = jnp.exp(m_old - m_new)
                    l_ref[gi, :, cols] = alpha * l_ref[gi, :, cols] + jnp.sum(p, axis=0, keepdims=True)
                    m_ref[gi, :, cols] = m_new
                    ps.append(p.astype(BF16))
                    alphas.append(alpha)
                vt = vt_ref[kj, gi * DK:(gi + 1) * DK, :]
                pv = _dot(vt, jnp.concatenate(ps, axis=1))
                pvs[gi].append((jnp.concatenate(alphas, axis=1), pv))
        for gi in range(2):
            acc = acc_ref[gi]
            for alpha, pv in pvs[gi]:
                acc = acc * alpha + pv
            acc_ref[gi] = acc
        return carry

    lax.fori_loop(0, last + 1 - first, body, 0)
    for gi in range(2):
        out_t = acc_ref[gi] / l_ref[gi]
        for h in range(HPG):
            c0 = (gi * HPG + h) * DK
            o_ref[:, c0:c0 + DK] = out_t[:, h * tq:(h + 1) * tq].T


def _flash(proj, selt, slopes, bsz, seq, mode):
    tq = ts = 128
    nsub = 2 if seq % (2 * ts) == 0 else 1
    nq = seq // tq
    nk = seq // ts
    nslc = seq // SLC_BLOCK
    ncol = 2 * DK
    kcol = (COL_KV + (2 if mode == "sel" else 4) * NSA_KV_WIDTH) // ncol
    vcol = kcol + NSA_KV_WIDTH // ncol
    in_specs = [pl.BlockSpec(memory_space=pltpu.SMEM),
                pl.BlockSpec((tq, 2 * HPG * DK), lambda b, g, i: (b * nq + i, g)),
                pl.BlockSpec((seq, ncol), lambda b, g, i: (b, kcol + g)),
                pl.BlockSpec((seq, ncol), lambda b, g, i: (b, vcol + g))]
    args = [slopes, proj, proj, proj]
    if mode == "sel":
        in_specs.append(pl.BlockSpec((2, None, nslc, tq), lambda b, g, i: (g, b, 0, i)))
        args.append(selt)
    return pl.pallas_call(
        functools.partial(_flash_kernel, mode=mode, tq=tq, ts=ts, nsub=nsub, nk=nk),
        grid=(bsz, 2, nq),
        in_specs=in_specs,
        out_specs=pl.BlockSpec((tq, 2 * HPG * DK), lambda b, g, i: (b * nq + i, g)),
        out_shape=jax.ShapeDtypeStruct((bsz * seq, NSA_WIDTH), F32),
        scratch_shapes=[pltpu.VMEM((2, nk, ts, DK), BF16),
                        pltpu.VMEM((nk, 2 * DK, ts), BF16),
                        pltpu.VMEM((2, HPG * tq, DK), BF16),
                        pltpu.VMEM((2 * HPG, ts, tq), F32),
                        pltpu.VMEM((2 * nsub, ts, HPG * tq), F32),
                        pltpu.VMEM((2, 1, HPG * tq), F32),
                        pltpu.VMEM((2, 1, HPG * tq), F32),
                        pltpu.VMEM((2, DK, HPG * tq), F32)],
        compiler_params=_params(("parallel", "parallel", "arbitrary")),
        name="nsa_" + mode,
    )(*args)


def _lru_kernel(x_ref, y_ref, cw_ref, cb_ref, wa_ref, ba_ref, wi_ref, bi_ref, lam_ref, o_ref,
                tail_ref, h_ref, *, tt):
    @pl.when(pl.program_id(1) == 0)
    def _():
        tail_ref[...] = jnp.zeros_like(tail_ref)
        h_ref[...] = jnp.zeros_like(h_ref)

    x = x_ref[...]
    ext = jnp.concatenate([tail_ref[...], x], axis=0)
    xc = cb_ref[...] + x * cw_ref[CONV_WIDTH - 1:CONV_WIDTH, :]
    for k in range(CONV_WIDTH - 1):
        back = CONV_WIDTH - 1 - k
        xc = xc + ext[8 - back:8 - back + tt, :] * cw_ref[k:k + 1, :]
    tail_ref[...] = x[tt - 8:, :]

    xcb = xc.astype(BF16)
    r = _sigmoid(_dot(xcb, wa_ref[...]) + ba_ref[...])
    gate_i = _sigmoid(_dot(xcb, wi_ref[...]) + bi_ref[...])
    neg_lam = -lam_ref[...]
    softplus = jnp.maximum(neg_lam, 0.0) + jnp.log1p(jnp.exp(-jnp.abs(neg_lam)))
    log_a = -LRU_C * r * softplus
    a = jnp.exp(log_a)
    u = jnp.sqrt(jnp.tanh(-log_a) * (a * a + 1.0)) * (gate_i * xc)

    rows = lax.broadcasted_iota(jnp.int32, (tt, 1), 0)
    step = 1
    while step < tt:
        keep = rows >= step
        a_prev = jnp.where(keep, pltpu.roll(a, step, 0), 1.0)
        u_prev = jnp.where(keep, pltpu.roll(u, step, 0), 0.0)
        u = u + a * u_prev
        a = a * a_prev
        step *= 2
    hs = u + a * h_ref[...]
    h_ref[...] = hs[tt - 1:tt, :]
    o_ref[...] = hs * jax.nn.gelu(y_ref[...], approximate=True)


def _lru(proj, cw, cb, wa, ba, wi, bi, lam, bsz, seq):
    tt = min(512, seq)
    nt = seq // tt
    w = LRU_WIDTH
    xcol = COL_LRU_X // w
    ycol = COL_LRU_Y // w
    vec = pl.BlockSpec((1, w), lambda b, i: (0, 0))
    mat = pl.BlockSpec((w, w), lambda b, i: (0, 0))
    return pl.pallas_call(
        functools.partial(_lru_kernel, tt=tt),
        grid=(bsz, nt),
        in_specs=[pl.BlockSpec((tt, w), lambda b, i: (b * nt + i, xcol)),
                  pl.BlockSpec((tt, w), lambda b, i: (b * nt + i, ycol)),
                  pl.BlockSpec((CONV_WIDTH, w), lambda b, i: (0, 0)),
                  vec, mat, vec, mat, vec, vec],
        out_specs=pl.BlockSpec((tt, w), lambda b, i: (b * nt + i, 0)),
        out_shape=jax.ShapeDtypeStruct((bsz * seq, w), F32),
        scratch_shapes=[pltpu.VMEM((8, w), F32), pltpu.VMEM((1, w), F32)],
        compiler_params=_params(("parallel", "arbitrary")),
        name="rglru",
    )(proj, proj, cw, cb, wa, ba, wi, bi, lam)


def _gla_kernel(q_ref, k_ref, v_ref, r_ref, gz_ref, w2_ref, bg_ref, ng_ref, o_ref, st_ref, *, tt):
    c = GLA_CHUNK
    dh = GLA_HEAD_DIM

    @pl.when(pl.program_id(1) == 0)
    def _():
        st_ref[...] = jnp.zeros_like(st_ref)

    gate = _dot(gz_ref[...].astype(BF16), w2_ref[...]) + bg_ref[...]
    log_alpha = (jnp.minimum(gate, 0.0) - jnp.log1p(jnp.exp(-jnp.abs(gate)))) / GLA_GATE_TAU
    ri = lax.broadcasted_iota(jnp.int32, (tt, tt), 0)
    ci = lax.broadcasted_iota(jnp.int32, (tt, tt), 1)
    tri = jnp.where((ri >> CHUNK_SHIFT == ci >> CHUNK_SHIFT) & (ci <= ri), 1.0, 0.0)
    bcum_all = _dot(tri, log_alpha, precision=lax.Precision.HIGHEST)
    causal = (lax.broadcasted_iota(jnp.int32, (c, c), 1) <= lax.broadcasted_iota(jnp.int32, (c, c), 0))

    for n in range(tt // c):
        rs = slice(n * c, (n + 1) * c)
        bcum = bcum_all[rs]
        b_last = bcum[c - 1:c, :]
        e_pos = jnp.exp(bcum)
        q_t = (q_ref[rs, :] * dh ** -0.5) * e_pos
        kk = k_ref[rs, :]
        k_t = kk * jnp.exp(-bcum)
        k_end = kk * jnp.exp(b_last - bcum)
        decay = jnp.exp(b_last)
        vv = v_ref[rs, :]
        for hh in range(GLA_HEADS):
            cs = slice(hh * dh, (hh + 1) * dh)
            qh = q_t[:, cs].astype(BF16)
            vh = vv[:, cs]
            att = jnp.where(causal, _dot_nt(qh, k_t[:, cs].astype(BF16)), 0.0)
            st = st_ref[hh]
            o = _dot(att.astype(BF16), vh.astype(BF16)) + _dot_nt(qh, st.astype(BF16))
            st_ref[hh] = st * decay[:, cs] + _dot(vh.T.astype(BF16), k_end[:, cs].astype(BF16))
            o = _rms_scale(o) * ng_ref[:, cs]
            rr = r_ref[rs, cs]
            o_ref[rs, cs] = o * (rr * _sigmoid(rr))


def _gla(proj, w2pad, bg, ng, bsz, seq):
    tt = min(256, seq)
    nt = seq // tt
    w = GLA_WIDTH
    cols = [COL_GLA_Q // w, COL_GLA_K // w, COL_GLA_V // w, COL_GLA_R // w]
    seg = [pl.BlockSpec((tt, w), functools.partial(lambda b, i, cc: (b * nt + i, cc), cc=cc)) for cc in cols]
    vec = pl.BlockSpec((1, w), lambda b, i: (0, 0))
    return pl.pallas_call(
        functools.partial(_gla_kernel, tt=tt),
        grid=(bsz, nt),
        in_specs=seg + [pl.BlockSpec((tt, 128), lambda b, i: (b * nt + i, COL_GZ // 128)),
                        pl.BlockSpec((128, w), lambda b, i: (0, 0)), vec, vec],
        out_specs=pl.BlockSpec((tt, w), lambda b, i: (b * nt + i, 0)),
        out_shape=jax.ShapeDtypeStruct((bsz * seq, w), F32),
        scratch_shapes=[pltpu.VMEM((GLA_HEADS, GLA_HEAD_DIM, GLA_HEAD_DIM), F32)],
        compiler_params=_params(("parallel", "arbitrary")),
        name="gla",
    )(proj, proj, proj, proj, proj, w2pad, bg, ng)


def _out_proj_kernel(ocmp_ref, osel_ref, owin_ref, gz_ref, olru_ref, ogla_ref, h_ref, e_ref, w_ref,
                     g_ref, o_ref):
    sg = _sigmoid(gz_ref[...])
    nsa = None
    for c, o_c in enumerate((ocmp_ref, osel_ref, owin_ref)):
        term = _dot(sg, e_ref[c], precision=lax.Precision.HIGHEST) * o_c[...]
        nsa = term if nsa is None else nsa + term
    y = _dot(nsa.astype(BF16), w_ref[0:NSA_WIDTH, :])
    y = y + _dot(olru_ref[...].astype(BF16), w_ref[NSA_WIDTH:NSA_WIDTH + LRU_WIDTH, :])
    y = y + _dot(ogla_ref[...].astype(BF16), w_ref[NSA_WIDTH + LRU_WIDTH:, :])
    o_ref[...] = h_ref[...] + _rms_scale(y) * g_ref[...]


def _out_proj(ocmp, osel, owin, proj, olru, ogla, h2d, expand, w, g):
    m = h2d.shape[0]
    tm = min(256, m)
    row = lambda width, col=0: pl.BlockSpec((tm, width), lambda i: (i, col))
    return pl.pallas_call(
        _out_proj_kernel,
        grid=(m // tm,),
        in_specs=[row(NSA_WIDTH), row(NSA_WIDTH), row(NSA_WIDTH), row(128, COL_GZ // 128),
                  row(LRU_WIDTH), row(GLA_WIDTH), row(D_MODEL),
                  pl.BlockSpec((3, 128, NSA_WIDTH), lambda i: (0, 0, 0)),
                  pl.BlockSpec((D_MODEL, D_MODEL), lambda i: (0, 0)),
                  pl.BlockSpec((1, D_MODEL), lambda i: (0, 0))],
        out_specs=row(D_MODEL),
        out_shape=jax.ShapeDtypeStruct((m, D_MODEL), F32),
        compiler_params=_params(("parallel",)),
        name="out_proj",
    )(ocmp, osel, owin, proj, olru, ogla, h2d, expand, w, g)


def _mlp_kernel(h_ref, gpre_ref, wu_ref, wd_ref, gpost_ref, o_ref, un_ref, acc_ref):
    f = pl.program_id(1)

    @pl.when(f == 0)
    def _():
        un_ref[...] = (_rms_scale(h_ref[...]) * gpre_ref[...]).astype(BF16)
        acc_ref[...] = jnp.zeros_like(acc_ref)

    a = jnp.maximum(_dot(un_ref[...], wu_ref[...]), 0.0)
    acc_ref[...] += _dot((a * a).astype(BF16), wd_ref[...])

    @pl.when(f == pl.num_programs(1) - 1)
    def _():
        o_ref[...] = h_ref[...] + _rms_scale(acc_ref[...]) * gpost_ref[...]


def _mlp(h2d, gpre, wu, wd, gpost):
    m = h2d.shape[0]
    tm = min(512, m)
    tf = 512
    vec = pl.BlockSpec((1, D_MODEL), lambda i, f: (0, 0))
    return pl.pallas_call(
        _mlp_kernel,
        grid=(m // tm, D_FF // tf),
        in_specs=[pl.BlockSpec((tm, D_MODEL), lambda i, f: (i, 0)), vec,
                  pl.BlockSpec((D_MODEL, tf), lambda i, f: (0, f)),
                  pl.BlockSpec((tf, D_MODEL), lambda i, f: (f, 0)), vec],
        out_specs=pl.BlockSpec((tm, D_MODEL), lambda i, f: (i, 0)),
        out_shape=jax.ShapeDtypeStruct((m, D_MODEL), F32),
        scratch_shapes=[pltpu.VMEM((tm, D_MODEL), BF16), pltpu.VMEM((tm, D_MODEL), F32)],
        compiler_params=_params(("parallel", "arbitrary")),
        name="mlp",
    )(h2d, gpre, wu, wd, gpost)


def _ple_kernel(h_ref, p_ref, wg_ref, wp_ref, o_ref):
    h = h_ref[...]
    gate = _sigmoid(_dot(h.astype(BF16), wg_ref[...]))
    o_ref[...] = h + gate * _dot(p_ref[...].astype(BF16), wp_ref[...])


def _ple(h2d, p2d, wg, wp):
    m = h2d.shape[0]
    tm = min(512, m)
    return pl.pallas_call(
        _ple_kernel,
        grid=(m // tm,),
        in_specs=[pl.BlockSpec((tm, D_MODEL), lambda i: (i, 0)),
                  pl.BlockSpec((tm, PLE_DIM), lambda i: (i, 0)),
                  pl.BlockSpec((D_MODEL, D_MODEL), lambda i: (0, 0)),
                  pl.BlockSpec((PLE_DIM, D_MODEL), lambda i: (0, 0))],
        out_specs=pl.BlockSpec((tm, D_MODEL), lambda i: (i, 0)),
        out_shape=jax.ShapeDtypeStruct((m, D_MODEL), F32),
        compiler_params=_params(("parallel",)),
        name="ple",
    )(h2d, p2d, wg, wp)


def _permute_w_in(w):
    g0 = NSA_WIDTH + 6 * NSA_KV_WIDTH
    g1 = g0 + N_GATES
    z0 = g1 + 2 * LRU_WIDTH + 4 * GLA_WIDTH
    pad = jnp.zeros((w.shape[0], D_IN_PAD - w.shape[1]), w.dtype)
    return jnp.concatenate([w[:, :g0], w[:, g1:z0], w[:, g0:g1], w[:, z0:], pad], axis=1).astype(BF16)


def _compress_weights(cmp_w, cmp_pe):
    w4 = cmp_w.reshape(2, CMP_LEN, DK, DK)
    eye_c = jnp.eye(2, dtype=cmp_w.dtype)
    eye_g = jnp.eye(NSA_KV_HEADS, dtype=cmp_w.dtype)
    big = jnp.einsum("clde,cx,gy->lcgdxye", w4, eye_c, eye_g)
    big = big.reshape(CMP_LEN * 2 * NSA_KV_WIDTH, 2 * NSA_KV_WIDTH).astype(BF16)
    half = CMP_STRIDE * 2 * NSA_KV_WIDTH
    pe = jnp.broadcast_to(cmp_pe.transpose(1, 0, 2)[:, :, None, :], (CMP_LEN, 2, NSA_KV_HEADS, DK))
    pe = pe.reshape(1, CMP_LEN * 2 * NSA_KV_WIDTH)
    return big[:half], big[half:], pe[:, :half], pe[:, half:]


def _block_diag(w):
    eye = jnp.eye(LRU_BLOCKS, dtype=w.dtype)
    return jnp.einsum("ncd,nm->ncmd", w, eye).reshape(LRU_WIDTH, LRU_WIDTH).astype(BF16)


def _gate_expand():
    e = np.zeros((3, 128, NSA_WIDTH), np.float32)
    for hh in range(NSA_HEADS):
        for c in range(3):
            e[c, hh * 3 + c, hh * DK:(hh + 1) * DK] = 1.0
    return jnp.asarray(e)


def _layer(h2d, p2d, bsz, seq, norm_mix_pre, w_in, nsa_cmp_w, nsa_cmp_pe, lru_conv_w, lru_conv_b,
           lru_wa, lru_ba, lru_wi, lru_bi, lru_lambda, gla_w_gate2, gla_b_gate, gla_norm,
           w_out, norm_mix_post, norm_mlp_pre, w_up, w_down, norm_mlp_post, w_ple_gate, w_ple):
    row = lambda v: v.reshape(1, -1)
    proj = _in_proj(h2d, row(norm_mix_pre), _permute_w_in(w_in))

    nc = seq // CMP_STRIDE
    cmp_in = proj[:, COL_KV:COL_KV + 2 * NSA_KV_WIDTH].reshape(bsz, nc, CMP_STRIDE * 2 * NSA_KV_WIDTH)
    wa, wb, pea, peb = _compress_weights(nsa_cmp_w, nsa_cmp_pe)
    cmp_k, cmp_v = _compress(cmp_in, pea, peb, wa, wb)
    o_cmp, selt = _cmp_attn(proj, cmp_k, cmp_v, bsz, seq)
    slopes = jnp.asarray(SLOPES, F32)
    o_sel = _flash(proj, selt, slopes, bsz, seq, "sel")
    o_win = _flash(proj, None, slopes, bsz, seq, "win")

    o_lru = _lru(proj, lru_conv_w, row(lru_conv_b), _block_diag(lru_wa), row(lru_ba),
                 _block_diag(lru_wi), row(lru_bi), row(lru_lambda), bsz, seq)

    w2pad = jnp.zeros((128, GLA_WIDTH), F32).at[N_GATES:N_GATES + GLA_GATE_RANK].set(gla_w_gate2)
    o_gla = _gla(proj, w2pad.astype(BF16), row(gla_b_gate), row(gla_norm), bsz, seq)

    h2d = _out_proj(o_cmp, o_sel, o_win, proj, o_lru, o_gla, h2d, _gate_expand(),
                    w_out.astype(BF16), row(norm_mix_post))
    h2d = _mlp(h2d, row(norm_mlp_pre), w_up.astype(BF16), w_down.astype(BF16), row(norm_mlp_post))
    return _ple(h2d, p2d, w_ple_gate.astype(BF16), w_ple.astype(BF16))


def kernel(x, p, norm_mix_pre, w_in, nsa_cmp_w, nsa_cmp_pe, lru_conv_w, lru_conv_b, lru_wa, lru_ba,
           lru_wi, lru_bi, lru_lambda, gla_w_gate2, gla_b_gate, gla_norm, w_out, norm_mix_post,
           norm_mlp_pre, w_up, w_down, norm_mlp_post, w_ple_gate, w_ple):
    bsz, seq, _ = x.shape
    h2d = x.reshape(bsz * seq, D_MODEL)
    weights = (norm_mix_pre, w_in, nsa_cmp_w, nsa_cmp_pe, lru_conv_w, lru_conv_b, lru_wa, lru_ba,
               lru_wi, lru_bi, lru_lambda, gla_w_gate2, gla_b_gate, gla_norm, w_out, norm_mix_post,
               norm_mlp_pre, w_up, w_down, norm_mlp_post, w_ple_gate, w_ple)
    for i in range(p.shape[0]):
        h2d = _layer(h2d, p[i].reshape(bsz * seq, PLE_DIM), bsz, seq, *(w[i] for w in weights))
    return h2d.reshape(bsz, seq, D_MODEL)
```

```python
import functools

import jax
import jax.numpy as jnp
from jax import lax
from jax.experimental import pallas as pl
from jax.experimental.pallas import tpu as pltpu

F32 = jnp.float32
BF16 = jnp.bfloat16

D_MODEL = 2048
PLE_DIM = 256
NSA_HEADS = 16
NSA_KV_HEADS = 4
HPG = NSA_HEADS // NSA_KV_HEADS
NSA_WIDTH = 1024
DK = 64
NSA_KV_WIDTH = NSA_KV_HEADS * DK
CMP_LEN = 32
CMP_STRIDE = 16
SLC_BLOCK = 64
SLC_SHIFT = 6
N_SELECT = 16
WINDOW = 512
LRU_WIDTH = 512
LRU_BLOCKS = 8
LRU_BLOCK_DIM = 64
CONV_WIDTH = 4
LRU_C = 8.0
GLA_WIDTH = 512
GLA_HEADS = 4
GLA_HEAD_DIM = 128
GLA_GATE_RANK = 16
GLA_GATE_TAU = 16.0
GLA_CHUNK = 64
CHUNK_SHIFT = 6
D_FF = 4 * D_MODEL
EPS = 1e-6
NEG_INF = -1e30
MASKED = 2.0 * NEG_INF

COL_Q = 0
COL_KV = 1024
COL_LRU_X = 2560
COL_LRU_Y = 3072
COL_GLA_Q = 3584
COL_GLA_K = 4096
COL_GLA_V = 4608
COL_GLA_R = 5120
COL_GZ = 5632
D_IN_PAD = 5760
N_GATES = 3 * NSA_HEADS
LANES = 128
KEY_TILE = 128

SLOPES = tuple(2.0 ** (-8.0 * i / NSA_HEADS) for i in range(1, NSA_HEADS + 1))

VMEM_LIMIT = 56 * 1024 * 1024

NT_DIMS = (((1,), (1,)), ((), ()))
TN_DIMS = (((0,), (0,)), ((), ()))


def _params(sem):
    return pltpu.CompilerParams(dimension_semantics=sem, vmem_limit_bytes=VMEM_LIMIT)


def _dot(a, b, **kw):
    return jnp.dot(a, b, preferred_element_type=F32, **kw)


def _dot_nt(a, b, **kw):
    return lax.dot_general(a, b, NT_DIMS, preferred_element_type=F32, **kw)


def _dot_tn(a, b, **kw):
    return lax.dot_general(a, b, TN_DIMS, preferred_element_type=F32, **kw)


def _sigmoid(x):
    return 1.0 / (1.0 + jnp.exp(-x))


def _rms_scale(x):
    return x * lax.rsqrt(jnp.mean(x * x, axis=-1, keepdims=True) + EPS)


def _half_masks(rows):
    lane = lax.broadcasted_iota(jnp.int32, (rows, LANES), 1)
    return lane < DK, lane >= DK


def _pad_halves(blk):
    lo, hi = _half_masks(blk.shape[0])
    swapped = pltpu.roll(blk, DK, 1)
    zero = jnp.zeros_like(blk)
    return (jnp.where(lo, blk, zero), jnp.where(hi, swapped, zero),
            jnp.where(lo, swapped, zero), jnp.where(hi, blk, zero))


def _untranspose_pairs(o_t, tq):
    sub = lax.broadcasted_iota(jnp.int32, (4 * DK, LANES), 0)
    lane = lax.broadcasted_iota(jnp.int32, (4 * DK, LANES), 1)
    place = jnp.where(lane == (sub & (DK - 1)) + jnp.where(sub >= 2 * DK, DK, 0), 1.0, 0.0).astype(BF16)
    out = []
    for pair in range(o_t.shape[1] // (2 * tq)):
        parts = []
        for half in range(2):
            v = o_t[:, (2 * pair + half) * tq:(2 * pair + half + 1) * tq]
            v_hi = v.astype(BF16)
            parts += [v_hi, (v - v_hi.astype(F32)).astype(BF16)]
        out.append(_dot_tn(jnp.concatenate(parts, axis=0), place))
    return out


def _in_proj_kernel(x_ref, g_ref, w_ref, o_ref, xn_ref):
    @pl.when(pl.program_id(1) == 0)
    def _():
        xn_ref[...] = (_rms_scale(x_ref[...]) * g_ref[...]).astype(BF16)

    o_ref[...] = _dot(xn_ref[...], w_ref[...])


def _in_proj(h2d, g, w):
    m, k = h2d.shape
    n = w.shape[1]
    tm = min(1024, m)
    tn = 640
    return pl.pallas_call(
        _in_proj_kernel,
        grid=(m // tm, n // tn),
        in_specs=[pl.BlockSpec((tm, k), lambda i, j: (i, 0)),
                  pl.BlockSpec((1, k), lambda i, j: (0, 0)),
                  pl.BlockSpec((k, tn), lambda i, j: (0, j))],
        out_specs=pl.BlockSpec((tm, tn), lambda i, j: (i, j)),
        out_shape=jax.ShapeDtypeStruct((m, n), F32),
        scratch_shapes=[pltpu.VMEM((tm, k), BF16)],
        compiler_params=_params(("parallel", "arbitrary")),
        name="in_proj",
    )(h2d, g, w)


def _compress_kernel(x_ref, pe_ref, w_ref, o_ref, *, nc):
    half = CMP_STRIDE * DK
    res = []
    for j in range(2):
        x = x_ref[j]
        a = _dot((x + pe_ref[:, :half]).astype(BF16), w_ref[:half, :])
        b = _dot((x + pe_ref[:, half:]).astype(BF16), w_ref[half:, :])
        res.append(a + pltpu.roll(b, nc - 1, 0))
    o_ref[...] = jnp.concatenate(res, axis=1)


def _compress(x4, pe, w):
    b, _, nc, kk = x4.shape
    return pl.pallas_call(
        functools.partial(_compress_kernel, nc=nc),
        grid=(b, NSA_KV_HEADS),
        in_specs=[pl.BlockSpec((None, 2, nc, kk), lambda i, j: (i, j, 0, 0)),
                  pl.BlockSpec((None, 1, CMP_LEN * DK), lambda i, j: (j // 2, 0, 0)),
                  pl.BlockSpec((None, CMP_LEN * DK, DK), lambda i, j: (j // 2, 0, 0))],
        out_specs=pl.BlockSpec((None, nc, LANES), lambda i, j: (i, 0, j)),
        out_shape=jax.ShapeDtypeStruct((b, nc, 2 * NSA_KV_WIDTH), F32),
        compiler_params=_params(("parallel", "parallel")),
        name="nsa_compress",
    )(x4, pe, w)


def _cmp_attn_kernel(q_ref, kv_ref, gz_ref, ocmp_ref, selt_ref, cnt_ref, kz_ref, vct_ref, bias_ref, s_ref,
                     *, tq, nc, nslc, nsel, nb):
    t0 = pl.program_id(1) * tq

    @pl.when(pl.program_id(1) == 0)
    def _():
        for gp in range(NSA_KV_HEADS // 2):
            padded = _pad_halves(kv_ref[:, gp * LANES:(gp + 1) * LANES])
            for idx in range(4):
                kz_ref[gp * 4 + idx] = padded[idx].astype(BF16)
            vct_ref[gp] = kv_ref[:, NSA_KV_WIDTH + gp * LANES:NSA_KV_WIDTH + (gp + 1) * LANES].T.astype(BF16)
        end_n = (lax.broadcasted_iota(jnp.int32, (nc, tq), 0) * CMP_STRIDE + (CMP_LEN - 1)).astype(F32)
        for hh in range(NSA_HEADS):
            bias_ref[hh] = SLOPES[hh] * end_n

    qb = (q_ref[...] * DK ** -0.5).astype(BF16)
    for hh in range(NSA_HEADS):
        pair = hh // 2
        s_ref[hh] = _dot_nt(kz_ref[(hh // HPG) * 2 + hh % 2], qb[:, pair * LANES:(pair + 1) * LANES])

    n_row = lax.broadcasted_iota(jnp.int32, (nc, tq), 0)
    t_col = lax.broadcasted_iota(jnp.int32, (nc, tq), 1) + t0
    valid = t_col >= n_row * CMP_STRIDE + (CMP_LEN - 1)
    any_valid = jnp.where(t_col[0:1, :] >= CMP_LEN - 1, 1.0, 0.0)
    jj = lax.broadcasted_iota(jnp.int32, (nslc, nc), 0) * SLC_BLOCK
    nn = lax.broadcasted_iota(jnp.int32, (nslc, nc), 1) * CMP_STRIDE
    overlap_t = jnp.where((nn < jj + SLC_BLOCK) & (jj < nn + CMP_LEN), 1.0, 0.0)
    blk = lax.broadcasted_iota(jnp.int32, (nslc, tq), 0)
    t_blk = (lax.broadcasted_iota(jnp.int32, (nslc, tq), 1) + t0) >> SLC_SHIFT
    forced = (blk == 0) | (blk == t_blk) | (blk == t_blk - 1)
    future = blk > t_blk
    sub = lax.broadcasted_iota(jnp.int32, (8, tq), 0)
    ngrp = nslc // 8
    gate_t = _sigmoid(gz_ref[...].T)
    ones = jnp.ones((8, tq), BF16)

    outs = []
    for g in range(NSA_KV_HEADS):
        vct = vct_ref[g // 2, (g % 2) * DK:(g % 2 + 1) * DK, :]
        psum = jnp.zeros((nc, tq), F32)
        for h in range(HPG):
            hh = g * HPG + h
            s = jnp.where(valid, s_ref[hh] + bias_ref[hh], NEG_INF)
            e = jnp.exp(s - jnp.max(s, axis=0, keepdims=True))
            p = e * (any_valid / jnp.sum(e, axis=0, keepdims=True))
            outs.append(_dot(vct, p.astype(BF16)) * gate_t[3 * hh:3 * hh + 1, :])
            psum = psum + p
        imp_t = _dot(overlap_t, psum, precision=lax.Precision.HIGHEST)
        score = jnp.where(future, -1.0, jnp.where(forced, 1e4, imp_t))

        sc = [score[8 * r:8 * r + 8] for r in range(ngrp)]
        rank = [jnp.zeros((8, tq), F32) for _ in range(ngrp)]
        for i in range(nslc):
            si = jnp.broadcast_to(score[i:i + 1, :], (8, tq))
            for r in range(ngrp):
                if r < i // 8:
                    before = si > sc[r]
                elif r > i // 8:
                    before = si >= sc[r]
                else:
                    before = (si > sc[r]) | ((sub > i % 8) & (si >= sc[r]))
                rank[r] = rank[r] + jnp.where(before, 1.0, 0.0)
        for r in range(ngrp):
            selt_ref[g, 8 * r:8 * r + 8, :] = jnp.where(rank[r] < float(nsel), 1.0, 0.0)
        if g % 2:
            picks = None
            for gi in (g - 1, g):
                for c in range(nb):
                    part = selt_ref[gi, pl.ds(c, nslc // nb, stride=nb), :]
                    picks = part if picks is None else picks + part
            cnt = _dot_nt(ones, picks.astype(BF16))
            cnt_ref[g // 2:g // 2 + 1, :] = cnt[0:1, :].astype(jnp.int32)

    for pair, tile in enumerate(_untranspose_pairs(jnp.concatenate(outs, axis=1), tq)):
        ocmp_ref[:, pair * LANES:(pair + 1) * LANES] = tile


def _cmp_attn(proj, cmp_kv, bsz, seq):
    tq = 128
    nc = cmp_kv.shape[1]
    nslc = seq // SLC_BLOCK
    nsel = min(N_SELECT, nslc)
    nq = seq // tq
    return pl.pallas_call(
        functools.partial(_cmp_attn_kernel, tq=tq, nc=nc, nslc=nslc, nsel=nsel, nb=KEY_TILE // SLC_BLOCK),
        grid=(bsz, nq),
        in_specs=[pl.BlockSpec((tq, NSA_WIDTH), lambda b, i: (b * nq + i, 0)),
                  pl.BlockSpec((None, nc, 2 * NSA_KV_WIDTH), lambda b, i: (b, 0, 0)),
                  pl.BlockSpec((tq, LANES), lambda b, i: (b * nq + i, COL_GZ // LANES))],
        out_specs=[pl.BlockSpec((tq, NSA_WIDTH), lambda b, i: (b * nq + i, 0)),
                   pl.BlockSpec((NSA_KV_HEADS, None, nslc, tq), lambda b, i: (0, b, 0, i)),
                   pl.BlockSpec((None, None, NSA_KV_HEADS // 2, seq // KEY_TILE), lambda b, i: (b, i, 0, 0))],
        out_shape=[jax.ShapeDtypeStruct((bsz * seq, NSA_WIDTH), F32),
                   jax.ShapeDtypeStruct((NSA_KV_HEADS, bsz, nslc, seq), F32),
                   jax.ShapeDtypeStruct((bsz, nq, NSA_KV_HEADS // 2, seq // KEY_TILE), jnp.int32)],
        scratch_shapes=[pltpu.VMEM((2 * NSA_KV_HEADS, nc, LANES), BF16),
                        pltpu.VMEM((NSA_KV_HEADS // 2, LANES, nc), BF16),
                        pltpu.VMEM((NSA_HEADS, nc, tq), F32),
                        pltpu.VMEM((NSA_HEADS, nc, tq), F32)],
        compiler_params=_params(("parallel", "arbitrary")),
        name="nsa_cmp_attn",
    )(proj, cmp_kv, proj)


def _flash_kernel(*refs, mode, tq, ts, nsub, nk):
    if mode == "sel":
        slopes_ref, cnt_ref, q_ref, k_ref, v_ref, gz_ref, selt_ref, o_ref = refs[:8]
    else:
        slopes_ref, q_ref, k_ref, v_ref, gz_ref, o_ref = refs[:6]
    kz_ref, vt_ref, qb_ref, bias_ref, gate_ref, s_ref, m_ref, l_ref, acc_ref, idx_ref = refs[-10:]
    branch = 1 if mode == "sel" else 2
    gp = pl.program_id(1)
    qi = pl.program_id(2)
    t0 = qi * tq
    nb = ts // SLC_BLOCK
    ts_shift = ts.bit_length() - 1

    @pl.when(qi == 0)
    def _():
        for c in range(nk):
            padded = _pad_halves(k_ref[c * ts:(c + 1) * ts, :])
            for idx in range(4):
                kz_ref[idx, c] = padded[idx].astype(BF16)
            vt_ref[c] = v_ref[c * ts:(c + 1) * ts, :].T.astype(BF16)

    last_sub = (t0 + tq - 1) >> ts_shift
    if mode == "sel":
        def build(j, n):
            idx_ref[n] = j
            return n + jnp.where((cnt_ref[gp, j] > 0) & (j <= last_sub), 1, 0)

        n_live = lax.fori_loop(0, nk, build, 0, unroll=4)
    else:
        first_sub = jnp.maximum(t0 - (WINDOW - 1), 0) >> ts_shift
        n_live = last_sub + 1 - first_sub
        for u in range(nsub):
            idx_ref[u] = jnp.minimum(first_sub + u, last_sub)

    key_i = lax.broadcasted_iota(jnp.int32, (ts, tq), 0)
    qry_t = lax.broadcasted_iota(jnp.int32, (ts, tq), 1) + t0
    key_f = key_i.astype(F32)
    qb_ref[...] = (q_ref[...] * DK ** -0.5).astype(BF16)
    gate_ref[...] = _sigmoid(gz_ref[...].T)
    for gh in range(2 * HPG):
        bias_ref[gh] = slopes_ref[gp * 2 * HPG + gh] * key_f
    m_ref[...] = jnp.full(m_ref.shape, NEG_INF, F32)
    l_ref[...] = jnp.zeros(l_ref.shape, F32)
    acc_ref[...] = jnp.zeros(acc_ref.shape, F32)

    def sub_tile(step, u):
        pos = step * nsub + u
        return idx_ref[jnp.minimum(pos, n_live - 1)], pos < n_live

    def scores(step, dst_ref):
        for u in range(nsub):
            kj, _ = sub_tile(step, u)
            for gi in range(2):
                for h in range(HPG):
                    pair = gi * (HPG // 2) + h // 2
                    dst_ref[u * 2 + gi, :, h * tq:(h + 1) * tq] = _dot_nt(
                        kz_ref[gi * 2 + h % 2, kj], qb_ref[:, pair * LANES:(pair + 1) * LANES])

    def softmax_update(step, src_ref):
        pvs = [[], []]
        for u in range(nsub):
            kj, live = sub_tile(step, u)
            j0 = kj * ts
            rel0 = (jnp.zeros((1, tq), jnp.int32) + (j0 - t0)).astype(F32)
            dd = qry_t - (key_i + j0)
            lowest = jnp.where(live, 0, 1 << 30)
            for gi in range(2):
                if mode == "sel":
                    chosen = selt_ref[gi, pl.ds(kj * nb + nb - 1, 1), :]
                    for c in range(nb - 2, -1, -1):
                        chosen = jnp.where(key_i < (c + 1) * SLC_BLOCK,
                                           selt_ref[gi, pl.ds(kj * nb + c, 1), :], chosen)
                    mask = (chosen > 0.5) & (dd >= lowest)
                else:
                    mask = (dd >= lowest) & (dd < WINDOW)
                ps = []
                alphas = []
                for h in range(HPG):
                    shift = slopes_ref[(gp * 2 + gi) * HPG + h] * rel0
                    cols = slice(h * tq, (h + 1) * tq)
                    x = jnp.where(mask, src_ref[u * 2 + gi, :, cols] + bias_ref[gi * HPG + h], MASKED)
                    m_old = m_ref[gi, :, cols]
                    m_new = jnp.maximum(m_old, jnp.max(x, axis=0, keepdims=True) + shift)
                    p = jnp.exp(x - (m_new - shift))
                    alpha = jnp.exp(m_old - m_new)
                    l_ref[gi, :, cols] = alpha * l_ref[gi, :, cols] + jnp.sum(p, axis=0, keepdims=True)
                    m_ref[gi, :, cols] = m_new
                    ps.append(p.astype(BF16))
                    alphas.append(alpha)
                vt = vt_ref[kj, gi * DK:(gi + 1) * DK, :]
                pv = _dot(vt, jnp.concatenate(ps, axis=1))
                pvs[gi].append((jnp.concatenate(alphas, axis=1), pv))
        for gi in range(2):
            acc = acc_ref[gi]
            for alpha, pv in pvs[gi]:
                acc = acc * alpha + pv
            acc_ref[gi] = acc

    def body(step, carry):
        scores(step, s_ref)
        softmax_update(step, s_ref)
        return carry

    if mode == "sel":
        assert nsub & (nsub - 1) == 0
        lax.fori_loop(0, (n_live + nsub - 1) >> (nsub.bit_length() - 1), body, 0)
    else:
        body(0, 0)

    outs = []
    for gi in range(2):
        o_t = acc_ref[gi] / l_ref[gi]
        for h in range(HPG):
            row = (gp * 2 * HPG + gi * HPG + h) * 3 + branch
            outs.append(o_t[:, h * tq:(h + 1) * tq] * gate_ref[pl.ds(row, 1), :])
    for pair, tile in enumerate(_untranspose_pairs(jnp.concatenate(outs, axis=1), tq)):
        o_ref[:, pair * LANES:(pair + 1) * LANES] = tile


def _flash(proj, selt, cnt, slopes, bsz, seq, mode):
    tq = ts = KEY_TILE
    nsub = 2 if mode == "sel" else WINDOW // ts + 1
    nq = seq // tq
    nk = seq // ts
    nslc = seq // SLC_BLOCK
    kcol = (COL_KV + (2 if mode == "sel" else 4) * NSA_KV_WIDTH) // LANES
    vcol = kcol + NSA_KV_WIDTH // LANES
    smem = pltpu.SMEM
    in_specs = [pl.BlockSpec(memory_space=smem)]
    args = [slopes]
    if mode == "sel":
        in_specs.append(pl.BlockSpec((None, None, 2, nk), lambda b, g, i: (b, i, 0, 0), memory_space=smem))
        args.append(cnt)
    in_specs += [pl.BlockSpec((tq, 2 * HPG * DK), lambda b, g, i: (b * nq + i, g)),
                 pl.BlockSpec((seq, LANES), lambda b, g, i: (b, kcol + g)),
                 pl.BlockSpec((seq, LANES), lambda b, g, i: (b, vcol + g)),
                 pl.BlockSpec((tq, LANES), lambda b, g, i: (b * nq + i, COL_GZ // LANES))]
    args += [proj, proj, proj, proj]
    if mode == "sel":
        in_specs.append(pl.BlockSpec((2, None, nslc, tq), lambda b, g, i: (g, b, 0, i)))
        args.append(selt)
    return pl.pallas_call(
        functools.partial(_flash_kernel, mode=mode, tq=tq, ts=ts, nsub=nsub, nk=nk),
        grid=(bsz, 2, nq),
        in_specs=in_specs,
        out_specs=pl.BlockSpec((tq, 2 * HPG * DK), lambda b, g, i: (b * nq + i, g)),
        out_shape=jax.ShapeDtypeStruct((bsz * seq, NSA_WIDTH), F32),
        scratch_shapes=[pltpu.VMEM((4, nk, ts, LANES), BF16),
                        pltpu.VMEM((nk, LANES, ts), BF16),
                        pltpu.VMEM((tq, 2 * HPG * DK), BF16),
                        pltpu.VMEM((2 * HPG, ts, tq), F32),
                        pltpu.VMEM((LANES, tq), F32),
                        pltpu.VMEM((2 * nsub, ts, HPG * tq), F32),
                        pltpu.VMEM((2, 1, HPG * tq), F32),
                        pltpu.VMEM((2, 1, HPG * tq), F32),
                        pltpu.VMEM((2, DK, HPG * tq), F32),
                        pltpu.SMEM((nk + nsub,), jnp.int32)],
        compiler_params=_params(("parallel", "parallel", "arbitrary")),
        name="nsa_" + mode,
    )(*args)


def _lru_kernel(x_ref, y_ref, cw_ref, cb_ref, wa_ref, ba_ref, wi_ref, bi_ref, lam_ref, o_ref,
                tail_ref, h_ref, *, tt):
    @pl.when(pl.program_id(1) == 0)
    def _():
        tail_ref[...] = jnp.zeros_like(tail_ref)
        h_ref[...] = jnp.zeros_like(h_ref)

    x = x_ref[...]
    ext = jnp.concatenate([tail_ref[...], x], axis=0)
    xc = cb_ref[...] + x * cw_ref[CONV_WIDTH - 1:CONV_WIDTH, :]
    for k in range(CONV_WIDTH - 1):
        back = CONV_WIDTH - 1 - k
        xc = xc + ext[8 - back:8 - back + tt, :] * cw_ref[k:k + 1, :]
    tail_ref[...] = x[tt - 8:, :]

    xcb = xc.astype(BF16)
    r = _sigmoid(_dot(xcb, wa_ref[...]) + ba_ref[...])
    gate_i = _sigmoid(_dot(xcb, wi_ref[...]) + bi_ref[...])
    neg_lam = -lam_ref[...]
    softplus = jnp.maximum(neg_lam, 0.0) + jnp.log1p(jnp.exp(-jnp.abs(neg_lam)))
    log_a = -LRU_C * r * softplus
    a = jnp.exp(log_a)
    u = jnp.sqrt(jnp.tanh(-log_a) * (a * a + 1.0)) * (gate_i * xc)

    rows = lax.broadcasted_iota(jnp.int32, (tt, 1), 0)
    step = 1
    while step < tt:
        keep = rows >= step
        a_prev = jnp.where(keep, pltpu.roll(a, step, 0), 1.0)
        u_prev = jnp.where(keep, pltpu.roll(u, step, 0), 0.0)
        u = u + a * u_prev
        a = a * a_prev
        step *= 2
    hs = u + a * h_ref[...]
    h_ref[...] = hs[tt - 1:tt, :]
    o_ref[...] = hs * jax.nn.gelu(y_ref[...], approximate=True)


def _lru(proj, cw, cb, wa, ba, wi, bi, lam, bsz, seq):
    tt = min(512, seq)
    nt = seq // tt
    w = LRU_WIDTH
    xcol = COL_LRU_X // w
    ycol = COL_LRU_Y // w
    vec = pl.BlockSpec((1, w), lambda b, i: (0, 0))
    mat = pl.BlockSpec((w, w), lambda b, i: (0, 0))
    return pl.pallas_call(
        functools.partial(_lru_kernel, tt=tt),
        grid=(bsz, nt),
        in_specs=[pl.BlockSpec((tt, w), lambda b, i: (b * nt + i, xcol)),
                  pl.BlockSpec((tt, w), lambda b, i: (b * nt + i, ycol)),
                  pl.BlockSpec((CONV_WIDTH, w), lambda b, i: (0, 0)),
                  vec, mat, vec, mat, vec, vec],
        out_specs=pl.BlockSpec((tt, w), lambda b, i: (b * nt + i, 0)),
        out_shape=jax.ShapeDtypeStruct((bsz * seq, w), F32),
        scratch_shapes=[pltpu.VMEM((8, w), F32), pltpu.VMEM((1, w), F32)],
        compiler_params=_params(("parallel", "arbitrary")),
        name="rglru",
    )(proj, proj, cw, cb, wa, ba, wi, bi, lam)


def _gla_kernel(q_ref, k_ref, v_ref, r_ref, gz_ref, w2_ref, bg_ref, ng_ref, o_ref, st_ref, *, tt):
    c = GLA_CHUNK
    dh = GLA_HEAD_DIM

    @pl.when(pl.program_id(1) == 0)
    def _():
        st_ref[...] = jnp.zeros_like(st_ref)

    gate = _dot(gz_ref[...].astype(BF16), w2_ref[...]) + bg_ref[...]
    log_alpha = (jnp.minimum(gate, 0.0) - jnp.log1p(jnp.exp(-jnp.abs(gate)))) / GLA_GATE_TAU
    ri = lax.broadcasted_iota(jnp.int32, (tt, tt), 0)
    ci = lax.broadcasted_iota(jnp.int32, (tt, tt), 1)
    tri = jnp.where((ri >> CHUNK_SHIFT == ci >> CHUNK_SHIFT) & (ci <= ri), 1.0, 0.0)
    bcum_all = _dot(tri, log_alpha, precision=lax.Precision.HIGHEST)
    causal = (lax.broadcasted_iota(jnp.int32, (c, c), 1) <= lax.broadcasted_iota(jnp.int32, (c, c), 0))

    for n in range(tt // c):
        rs = slice(n * c, (n + 1) * c)
        bcum = bcum_all[rs]
        b_last = bcum[c - 1:c, :]
        e_pos = jnp.exp(bcum)
        q_t = (q_ref[rs, :] * dh ** -0.5) * e_pos
        kk = k_ref[rs, :]
        k_t = kk * jnp.exp(-bcum)
        k_end = kk * jnp.exp(b_last - bcum)
        decay = jnp.exp(b_last)
        vv = v_ref[rs, :]
        for hh in range(GLA_HEADS):
            cs = slice(hh * dh, (hh + 1) * dh)
            qh = q_t[:, cs].astype(BF16)
            vh = vv[:, cs]
            att = jnp.where(causal, _dot_nt(qh, k_t[:, cs].astype(BF16)), 0.0)
            st = st_ref[hh]
            o = _dot(att.astype(BF16), vh.astype(BF16)) + _dot_nt(qh, st.astype(BF16))
            st_ref[hh] = st * decay[:, cs] + _dot(vh.T.astype(BF16), k_end[:, cs].astype(BF16))
            o = _rms_scale(o) * ng_ref[:, cs]
            rr = r_ref[rs, cs]
            o_ref[rs, cs] = o * (rr * _sigmoid(rr))


def _gla(proj, w2pad, bg, ng, bsz, seq):
    tt = min(256, seq)
    nt = seq // tt
    w = GLA_WIDTH
    cols = [COL_GLA_Q // w, COL_GLA_K // w, COL_GLA_V // w, COL_GLA_R // w]
    seg = [pl.BlockSpec((tt, w), functools.partial(lambda b, i, cc: (b * nt + i, cc), cc=cc)) for cc in cols]
    vec = pl.BlockSpec((1, w), lambda b, i: (0, 0))
    return pl.pallas_call(
        functools.partial(_gla_kernel, tt=tt),
        grid=(bsz, nt),
        in_specs=seg + [pl.BlockSpec((tt, LANES), lambda b, i: (b * nt + i, COL_GZ // LANES)),
                        pl.BlockSpec((LANES, w), lambda b, i: (0, 0)), vec, vec],
        out_specs=pl.BlockSpec((tt, w), lambda b, i: (b * nt + i, 0)),
        out_shape=jax.ShapeDtypeStruct((bsz * seq, w), F32),
        scratch_shapes=[pltpu.VMEM((GLA_HEADS, GLA_HEAD_DIM, GLA_HEAD_DIM), F32)],
        compiler_params=_params(("parallel", "arbitrary")),
        name="gla",
    )(proj, proj, proj, proj, proj, w2pad, bg, ng)


def _out_proj_kernel(ocmp_ref, osel_ref, owin_ref, olru_ref, ogla_ref, h_ref, w_ref, g_ref, o_ref):
    nsa = ocmp_ref[...] + osel_ref[...] + owin_ref[...]
    y = _dot(nsa.astype(BF16), w_ref[0:NSA_WIDTH, :])
    y = y + _dot(olru_ref[...].astype(BF16), w_ref[NSA_WIDTH:NSA_WIDTH + LRU_WIDTH, :])
    y = y + _dot(ogla_ref[...].astype(BF16), w_ref[NSA_WIDTH + LRU_WIDTH:, :])
    o_ref[...] = h_ref[...] + _rms_scale(y) * g_ref[...]


def _out_proj(ocmp, osel, owin, olru, ogla, h2d, w, g):
    m = h2d.shape[0]
    tm = min(256, m)
    row = lambda width: pl.BlockSpec((tm, width), lambda i: (i, 0))
    return pl.pallas_call(
        _out_proj_kernel,
        grid=(m // tm,),
        in_specs=[row(NSA_WIDTH), row(NSA_WIDTH), row(NSA_WIDTH), row(LRU_WIDTH), row(GLA_WIDTH), row(D_MODEL),
                  pl.BlockSpec((D_MODEL, D_MODEL), lambda i: (0, 0)),
                  pl.BlockSpec((1, D_MODEL), lambda i: (0, 0))],
        out_specs=row(D_MODEL),
        out_shape=jax.ShapeDtypeStruct((m, D_MODEL), F32),
        compiler_params=_params(("parallel",)),
        name="out_proj",
    )(ocmp, osel, owin, olru, ogla, h2d, w, g)


def _mlp_kernel(h_ref, gpre_ref, wu_ref, wd_ref, gpost_ref, o_ref, un_ref, acc_ref):
    f = pl.program_id(1)

    @pl.when(f == 0)
    def _():
        un_ref[...] = (_rms_scale(h_ref[...]) * gpre_ref[...]).astype(BF16)
        acc_ref[...] = jnp.zeros_like(acc_ref)

    a = jnp.maximum(_dot(un_ref[...], wu_ref[...]), 0.0)
    acc_ref[...] += _dot((a * a).astype(BF16), wd_ref[...])

    @pl.when(f == pl.num_programs(1) - 1)
    def _():
        o_ref[...] = h_ref[...] + _rms_scale(acc_ref[...]) * gpost_ref[...]


def _mlp(h2d, gpre, wu, wd, gpost):
    m = h2d.shape[0]
    tm = min(512, m)
    tf = 512
    vec = pl.BlockSpec((1, D_MODEL), lambda i, f: (0, 0))
    return pl.pallas_call(
        _mlp_kernel,
        grid=(m // tm, D_FF // tf),
        in_specs=[pl.BlockSpec((tm, D_MODEL), lambda i, f: (i, 0)), vec,
                  pl.BlockSpec((D_MODEL, tf), lambda i, f: (0, f)),
                  pl.BlockSpec((tf, D_MODEL), lambda i, f: (f, 0)), vec],
        out_specs=pl.BlockSpec((tm, D_MODEL), lambda i, f: (i, 0)),
        out_shape=jax.ShapeDtypeStruct((m, D_MODEL), F32),
        scratch_shapes=[pltpu.VMEM((tm, D_MODEL), BF16), pltpu.VMEM((tm, D_MODEL), F32)],
        compiler_params=_params(("parallel", "arbitrary")),
        name="mlp",
    )(h2d, gpre, wu, wd, gpost)


def _ple_kernel(h_ref, p_ref, wg_ref, wp_ref, o_ref):
    h = h_ref[...]
    gate = _sigmoid(_dot(h.astype(BF16), wg_ref[...]))
    o_ref[...] = h + gate * _dot(p_ref[...].astype(BF16), wp_ref[...])


def _ple(h2d, p2d, wg, wp):
    m = h2d.shape[0]
    tm = min(512, m)
    return pl.pallas_call(
        _ple_kernel,
        grid=(m // tm,),
        in_specs=[pl.BlockSpec((tm, D_MODEL), lambda i: (i, 0)),
                  pl.BlockSpec((tm, PLE_DIM), lambda i: (i, 0)),
                  pl.BlockSpec((D_MODEL, D_MODEL), lambda i: (0, 0)),
                  pl.BlockSpec((PLE_DIM, D_MODEL), lambda i: (0, 0))],
        out_specs=pl.BlockSpec((tm, D_MODEL), lambda i: (i, 0)),
        out_shape=jax.ShapeDtypeStruct((m, D_MODEL), F32),
        compiler_params=_params(("parallel",)),
        name="ple",
    )(h2d, p2d, wg, wp)


def _permute_w_in(w):
    g0 = NSA_WIDTH + 6 * NSA_KV_WIDTH
    g1 = g0 + N_GATES
    z0 = g1 + 2 * LRU_WIDTH + 4 * GLA_WIDTH
    pad = jnp.zeros((w.shape[0], D_IN_PAD - w.shape[1]), BF16)
    parts = [w[:, :g0], w[:, g1:z0], w[:, g0:g1], w[:, z0:]]
    return jnp.concatenate([part.astype(BF16) for part in parts] + [pad], axis=1)


def _block_diag(w):
    eye = jnp.eye(LRU_BLOCKS, dtype=w.dtype)
    return jnp.einsum("ncd,nm->ncmd", w, eye).reshape(LRU_WIDTH, LRU_WIDTH).astype(BF16)


def _layer(h2d, p2d, bsz, seq, norm_mix_pre, w_in, nsa_cmp_w, nsa_cmp_pe, lru_conv_w, lru_conv_b,
           lru_wa, lru_ba, lru_wi, lru_bi, lru_lambda, gla_w_gate2, gla_b_gate, gla_norm,
           w_out, norm_mix_post, norm_mlp_pre, w_up, w_down, norm_mlp_post, w_ple_gate, w_ple):
    row = lambda v: v.reshape(1, -1)
    proj = _in_proj(h2d, row(norm_mix_pre), _permute_w_in(w_in))

    nc = seq // CMP_STRIDE
    cmp_in = proj[:, COL_KV:COL_KV + 2 * NSA_KV_WIDTH].reshape(bsz, nc, CMP_STRIDE, 2 * NSA_KV_HEADS, DK)
    cmp_in = cmp_in.transpose(0, 3, 1, 2, 4).reshape(bsz, 2 * NSA_KV_HEADS, nc, CMP_STRIDE * DK)
    cmp_kv = _compress(cmp_in, nsa_cmp_pe.reshape(2, 1, CMP_LEN * DK), nsa_cmp_w.astype(BF16))
    o_cmp, selt, cnt = _cmp_attn(proj, cmp_kv, bsz, seq)
    slopes = jnp.asarray(SLOPES, F32)
    o_sel = _flash(proj, selt, cnt, slopes, bsz, seq, "sel")
    o_win = _flash(proj, None, None, slopes, bsz, seq, "win")

    o_lru = _lru(proj, lru_conv_w, row(lru_conv_b), _block_diag(lru_wa), row(lru_ba),
                 _block_diag(lru_wi), row(lru_bi), row(lru_lambda), bsz, seq)

    w2pad = jnp.zeros((LANES, GLA_WIDTH), F32).at[N_GATES:N_GATES + GLA_GATE_RANK].set(gla_w_gate2)
    o_gla = _gla(proj, w2pad.astype(BF16), row(gla_b_gate), row(gla_norm), bsz, seq)

    h2d = _out_proj(o_cmp, o_sel, o_win, o_lru, o_gla, h2d, w_out.astype(BF16), row(norm_mix_post))
    h2d = _mlp(h2d, row(norm_mlp_pre), w_up.astype(BF16), w_down.astype(BF16), row(norm_mlp_post))
    return _ple(h2d, p2d, w_ple_gate.astype(BF16), w_ple.astype(BF16))


def kernel(x, p, norm_mix_pre, w_in, nsa_cmp_w, nsa_cmp_pe, lru_conv_w, lru_conv_b, lru_wa, lru_ba,
           lru_wi, lru_bi, lru_lambda, gla_w_gate2, gla_b_gate, gla_norm, w_out, norm_mix_post,
           norm_mlp_pre, w_up, w_down, norm_mlp_post, w_ple_gate, w_ple):
    bsz, seq, _ = x.shape
    h2d = x.reshape(bsz * seq, D_MODEL)
    weights = (norm_mix_pre, w_in, nsa_cmp_w, nsa_cmp_pe, lru_conv_w, lru_conv_b, lru_wa, lru_ba,
               lru_wi, lru_bi, lru_lambda, gla_w_gate2, gla_b_gate, gla_norm, w_out, norm_mix_post,
               norm_mlp_pre, w_up, w_down, norm_mlp_post, w_ple_gate, w_ple)
    for i in range(p.shape[0]):
        h2d = _layer(h2d, p[i].reshape(bsz * seq, PLE_DIM), bsz, seq, *(w[i] for w in weights))
    return h2d.reshape(bsz, seq, D_MODEL)
```

```python
import functools

import jax
import jax.numpy as jnp
from jax import lax
from jax.experimental import pallas as pl
from jax.experimental.pallas import tpu as pltpu

F32 = jnp.float32
BF16 = jnp.bfloat16

D_MODEL = 2048
PLE_DIM = 256
NSA_HEADS = 16
NSA_KV_HEADS = 4
HPG = NSA_HEADS // NSA_KV_HEADS
NSA_WIDTH = 1024
DK = 64
NSA_KV_WIDTH = NSA_KV_HEADS * DK
CMP_LEN = 32
CMP_STRIDE = 16
SLC_BLOCK = 64
SLC_SHIFT = 6
N_SELECT = 16
WINDOW = 512
LRU_WIDTH = 512
LRU_BLOCKS = 8
LRU_BLOCK_DIM = 64
CONV_WIDTH = 4
LRU_C = 8.0
GLA_WIDTH = 512
GLA_HEADS = 4
GLA_HEAD_DIM = 128
GLA_GATE_RANK = 16
GLA_GATE_TAU = 16.0
GLA_CHUNK = 64
CHUNK_SHIFT = 6
D_FF = 4 * D_MODEL
EPS = 1e-6
NEG_INF = -1e30
MASKED = 2.0 * NEG_INF

COL_Q = 0
COL_KV = 1024
COL_LRU_X = 2560
COL_LRU_Y = 3072
COL_GLA_Q = 3584
COL_GLA_K = 4096
COL_GLA_V = 4608
COL_GLA_R = 5120
COL_GZ = 5632
D_IN_PAD = 5760
N_GATES = 3 * NSA_HEADS
LANES = 128
KEY_TILE = 128

SLOPES = tuple(2.0 ** (-8.0 * i / NSA_HEADS) for i in range(1, NSA_HEADS + 1))

VMEM_LIMIT = 56 * 1024 * 1024

NT_DIMS = (((1,), (1,)), ((), ()))
TN_DIMS = (((0,), (0,)), ((), ()))


def _params(sem):
    return pltpu.CompilerParams(dimension_semantics=sem, vmem_limit_bytes=VMEM_LIMIT)


def _dot(a, b, **kw):
    return jnp.dot(a, b, preferred_element_type=F32, **kw)


def _dot_nt(a, b, **kw):
    return lax.dot_general(a, b, NT_DIMS, preferred_element_type=F32, **kw)


def _dot_tn(a, b, **kw):
    return lax.dot_general(a, b, TN_DIMS, preferred_element_type=F32, **kw)


def _sigmoid(x):
    return 1.0 / (1.0 + jnp.exp(-x))


def _rms_scale(x):
    return x * lax.rsqrt(jnp.mean(x * x, axis=-1, keepdims=True) + EPS)


def _half_masks(rows):
    lane = lax.broadcasted_iota(jnp.int32, (rows, LANES), 1)
    return lane < DK, lane >= DK


def _pad_halves(blk):
    lo, hi = _half_masks(blk.shape[0])
    swapped = pltpu.roll(blk, DK, 1)
    zero = jnp.zeros_like(blk)
    return (jnp.where(lo, blk, zero), jnp.where(hi, swapped, zero),
            jnp.where(lo, swapped, zero), jnp.where(hi, blk, zero))


def _untranspose_pairs(o_t, tq):
    sub = lax.broadcasted_iota(jnp.int32, (4 * DK, LANES), 0)
    lane = lax.broadcasted_iota(jnp.int32, (4 * DK, LANES), 1)
    place = jnp.where(lane == (sub & (DK - 1)) + jnp.where(sub >= 2 * DK, DK, 0), 1.0, 0.0).astype(BF16)
    out = []
    for pair in range(o_t.shape[1] // (2 * tq)):
        parts = []
        for half in range(2):
            v = o_t[:, (2 * pair + half) * tq:(2 * pair + half + 1) * tq]
            v_hi = v.astype(BF16)
            parts += [v_hi, (v - v_hi.astype(F32)).astype(BF16)]
        out.append(_dot_tn(jnp.concatenate(parts, axis=0), place))
    return out


def _in_proj_kernel(x_ref, g_ref, w_ref, o_ref, xn_ref):
    @pl.when(pl.program_id(1) == 0)
    def _():
        xn_ref[...] = (_rms_scale(x_ref[...]) * g_ref[...]).astype(BF16)

    o_ref[...] = _dot(xn_ref[...], w_ref[...])


def _in_proj(h2d, g, w):
    m, k = h2d.shape
    n = w.shape[1]
    tm = min(1024, m)
    tn = 640
    return pl.pallas_call(
        _in_proj_kernel,
        grid=(m // tm, n // tn),
        in_specs=[pl.BlockSpec((tm, k), lambda i, j: (i, 0)),
                  pl.BlockSpec((1, k), lambda i, j: (0, 0)),
                  pl.BlockSpec((k, tn), lambda i, j: (0, j))],
        out_specs=pl.BlockSpec((tm, tn), lambda i, j: (i, j)),
        out_shape=jax.ShapeDtypeStruct((m, n), F32),
        scratch_shapes=[pltpu.VMEM((tm, k), BF16)],
        compiler_params=_params(("parallel", "arbitrary")),
        name="in_proj",
    )(h2d, g, w)


def _compress_kernel(x_ref, pe_ref, w_ref, o_ref, *, nc):
    half = CMP_STRIDE * DK
    res = []
    for j in range(2):
        x = x_ref[j]
        a = _dot((x + pe_ref[:, :half]).astype(BF16), w_ref[:half, :])
        b = _dot((x + pe_ref[:, half:]).astype(BF16), w_ref[half:, :])
        res.append(a + pltpu.roll(b, nc - 1, 0))
    o_ref[...] = jnp.concatenate(res, axis=1)


def _compress(x4, pe, w):
    b, _, nc, kk = x4.shape
    return pl.pallas_call(
        functools.partial(_compress_kernel, nc=nc),
        grid=(b, NSA_KV_HEADS),
        in_specs=[pl.BlockSpec((None, 2, nc, kk), lambda i, j: (i, j, 0, 0)),
                  pl.BlockSpec((None, 1, CMP_LEN * DK), lambda i, j: (j // 2, 0, 0)),
                  pl.BlockSpec((None, CMP_LEN * DK, DK), lambda i, j: (j // 2, 0, 0))],
        out_specs=pl.BlockSpec((None, nc, LANES), lambda i, j: (i, 0, j)),
        out_shape=jax.ShapeDtypeStruct((b, nc, 2 * NSA_KV_WIDTH), F32),
        compiler_params=_params(("parallel", "parallel")),
        name="nsa_compress",
    )(x4, pe, w)


def _cmp_attn_kernel(q_ref, kv_ref, gz_ref, ocmp_ref, selt_ref, cnt_ref, kz_ref, vct_ref, bias_ref, s_ref,
                     *, tq, nc, nslc, nsel, nb):
    t0 = pl.program_id(1) * tq

    @pl.when(pl.program_id(1) == 0)
    def _():
        for gp in range(NSA_KV_HEADS // 2):
            padded = _pad_halves(kv_ref[:, gp * LANES:(gp + 1) * LANES])
            for idx in range(4):
                kz_ref[gp * 4 + idx] = padded[idx].astype(BF16)
            vct_ref[gp] = kv_ref[:, NSA_KV_WIDTH + gp * LANES:NSA_KV_WIDTH + (gp + 1) * LANES].T.astype(BF16)
        end_n = (lax.broadcasted_iota(jnp.int32, (nc, tq), 0) * CMP_STRIDE + (CMP_LEN - 1)).astype(F32)
        for hh in range(NSA_HEADS):
            bias_ref[hh] = SLOPES[hh] * end_n

    qb = (q_ref[...] * DK ** -0.5).astype(BF16)
    for hh in range(NSA_HEADS):
        pair = hh // 2
        s_ref[hh] = _dot_nt(kz_ref[(hh // HPG) * 2 + hh % 2], qb[:, pair * LANES:(pair + 1) * LANES])

    n_row = lax.broadcasted_iota(jnp.int32, (nc, tq), 0)
    t_col = lax.broadcasted_iota(jnp.int32, (nc, tq), 1) + t0
    valid = t_col >= n_row * CMP_STRIDE + (CMP_LEN - 1)
    any_valid = jnp.where(t_col[0:1, :] >= CMP_LEN - 1, 1.0, 0.0)
    jj = lax.broadcasted_iota(jnp.int32, (nslc, nc), 0) * SLC_BLOCK
    nn = lax.broadcasted_iota(jnp.int32, (nslc, nc), 1) * CMP_STRIDE
    overlap_t = jnp.where((nn < jj + SLC_BLOCK) & (jj < nn + CMP_LEN), 1.0, 0.0)
    blk = lax.broadcasted_iota(jnp.int32, (nslc, tq), 0)
    t_blk = (lax.broadcasted_iota(jnp.int32, (nslc, tq), 1) + t0) >> SLC_SHIFT
    forced = (blk == 0) | (blk == t_blk) | (blk == t_blk - 1)
    future = blk > t_blk
    sub = lax.broadcasted_iota(jnp.int32, (8, tq), 0)
    ngrp = nslc // 8
    gate_t = _sigmoid(gz_ref[...].T)
    ones = jnp.ones((8, tq), BF16)

    outs = []
    for g in range(NSA_KV_HEADS):
        vct = vct_ref[g // 2, (g % 2) * DK:(g % 2 + 1) * DK, :]
        psum = jnp.zeros((nc, tq), F32)
        for h in range(HPG):
            hh = g * HPG + h
            s = jnp.where(valid, s_ref[hh] + bias_ref[hh], NEG_INF)
            e = jnp.exp(s - jnp.max(s, axis=0, keepdims=True))
            p = e * (any_valid / jnp.sum(e, axis=0, keepdims=True))
            outs.append(_dot(vct, p.astype(BF16)) * gate_t[3 * hh:3 * hh + 1, :])
            psum = psum + p
        imp_t = _dot(overlap_t, psum, precision=lax.Precision.HIGHEST)
        score = jnp.where(future, -1.0, jnp.where(forced, 1e4, imp_t))

        sc = [score[8 * r:8 * r + 8] for r in range(ngrp)]
        rank = [jnp.zeros((8, tq), F32) for _ in range(ngrp)]
        for i in range(nslc):
            si = jnp.broadcast_to(score[i:i + 1, :], (8, tq))
            for r in range(ngrp):
                if r < i // 8:
                    before = si > sc[r]
                elif r > i // 8:
                    before = si >= sc[r]
                else:
                    before = (si > sc[r]) | ((sub > i % 8) & (si >= sc[r]))
                rank[r] = rank[r] + jnp.where(before, 1.0, 0.0)
        for r in range(ngrp):
            selt_ref[g, 8 * r:8 * r + 8, :] = jnp.where(rank[r] < float(nsel), 1.0, 0.0)
        if g % 2:
            picks = None
            for gi in (g - 1, g):
                for c in range(nb):
                    part = selt_ref[gi, pl.ds(c, nslc // nb, stride=nb), :]
                    picks = part if picks is None else picks + part
            cnt = _dot_nt(ones, picks.astype(BF16))
            cnt_ref[g // 2:g // 2 + 1, :] = cnt[0:1, :].astype(jnp.int32)

    for pair, tile in enumerate(_untranspose_pairs(jnp.concatenate(outs, axis=1), tq)):
        ocmp_ref[:, pair * LANES:(pair + 1) * LANES] = tile


def _cmp_attn(proj, cmp_kv, bsz, seq):
    tq = 128
    nc = cmp_kv.shape[1]
    nslc = seq // SLC_BLOCK
    nsel = min(N_SELECT, nslc)
    nq = seq // tq
    return pl.pallas_call(
        functools.partial(_cmp_attn_kernel, tq=tq, nc=nc, nslc=nslc, nsel=nsel, nb=KEY_TILE // SLC_BLOCK),
        grid=(bsz, nq),
        in_specs=[pl.BlockSpec((tq, NSA_WIDTH), lambda b, i: (b * nq + i, 0)),
                  pl.BlockSpec((None, nc, 2 * NSA_KV_WIDTH), lambda b, i: (b, 0, 0)),
                  pl.BlockSpec((tq, LANES), lambda b, i: (b * nq + i, COL_GZ // LANES))],
        out_specs=[pl.BlockSpec((tq, NSA_WIDTH), lambda b, i: (b * nq + i, 0)),
                   pl.BlockSpec((NSA_KV_HEADS, None, nslc, tq), lambda b, i: (0, b, 0, i)),
                   pl.BlockSpec((None, None, NSA_KV_HEADS // 2, seq // KEY_TILE), lambda b, i: (b, i, 0, 0))],
        out_shape=[jax.ShapeDtypeStruct((bsz * seq, NSA_WIDTH), F32),
                   jax.ShapeDtypeStruct((NSA_KV_HEADS, bsz, nslc, seq), F32),
                   jax.ShapeDtypeStruct((bsz, nq, NSA_KV_HEADS // 2, seq // KEY_TILE), jnp.int32)],
        scratch_shapes=[pltpu.VMEM((2 * NSA_KV_HEADS, nc, LANES), BF16),
                        pltpu.VMEM((NSA_KV_HEADS // 2, LANES, nc), BF16),
                        pltpu.VMEM((NSA_HEADS, nc, tq), F32),
                        pltpu.VMEM((NSA_HEADS, nc, tq), F32)],
        compiler_params=_params(("parallel", "arbitrary")),
        name="nsa_cmp_attn",
    )(proj, cmp_kv, proj)


def _flash_kernel(*refs, mode, tq, ts, nsub, nk):
    if mode == "sel":
        slopes_ref, cnt_ref, q_ref, k_ref, v_ref, gz_ref, selt_ref, o_ref = refs[:8]
    else:
        slopes_ref, q_ref, k_ref, v_ref, gz_ref, o_ref = refs[:6]
    kz_ref, vt_ref, qb_ref, bias_ref, gate_ref, s_ref, m_ref, l_ref, acc_ref, idx_ref = refs[-10:]
    branch = 1 if mode == "sel" else 2
    gp = pl.program_id(1)
    qi = pl.program_id(2)
    t0 = qi * tq
    nb = ts // SLC_BLOCK
    ts_shift = ts.bit_length() - 1

    @pl.when(qi == 0)
    def _():
        for c in range(nk):
            padded = _pad_halves(k_ref[c * ts:(c + 1) * ts, :])
            for idx in range(4):
                kz_ref[idx, c] = padded[idx].astype(BF16)
            vt_ref[c] = v_ref[c * ts:(c + 1) * ts, :].T.astype(BF16)

    last_sub = (t0 + tq - 1) >> ts_shift
    if mode == "sel":
        def build(j, n):
            idx_ref[n] = j
            return n + jnp.where((cnt_ref[gp, j] > 0) & (j <= last_sub), 1, 0)

        n_live = lax.fori_loop(0, nk, build, 0, unroll=4)
    else:
        first_sub = jnp.maximum(t0 - (WINDOW - 1), 0) >> ts_shift
        n_live = last_sub + 1 - first_sub
        for u in range(nsub):
            idx_ref[u] = jnp.minimum(first_sub + u, last_sub)

    key_i = lax.broadcasted_iota(jnp.int32, (ts, tq), 0)
    qry_t = lax.broadcasted_iota(jnp.int32, (ts, tq), 1) + t0
    key_f = key_i.astype(F32)
    qb_ref[...] = (q_ref[...] * DK ** -0.5).astype(BF16)
    gate_ref[...] = _sigmoid(gz_ref[...].T)
    for gh in range(2 * HPG):
        bias_ref[gh] = slopes_ref[gp * 2 * HPG + gh] * key_f
    m_ref[...] = jnp.full(m_ref.shape, NEG_INF, F32)
    l_ref[...] = jnp.zeros(l_ref.shape, F32)
    acc_ref[...] = jnp.zeros(acc_ref.shape, F32)

    def sub_tile(step, u):
        pos = step * nsub + u
        return idx_ref[jnp.minimum(pos, n_live - 1)], pos < n_live

    def scores(step, dst_ref):
        for u in range(nsub):
            kj, _ = sub_tile(step, u)
            for gi in range(2):
                for h in range(HPG):
                    pair = gi * (HPG // 2) + h // 2
                    dst_ref[u * 2 + gi, :, h * tq:(h + 1) * tq] = _dot_nt(
                        kz_ref[gi * 2 + h % 2, kj], qb_ref[:, pair * LANES:(pair + 1) * LANES])

    def softmax_update(step, src_ref):
        pvs = [[], []]
        for u in range(nsub):
            kj, live = sub_tile(step, u)
            j0 = kj * ts
            rel0 = (jnp.zeros((1, tq), jnp.int32) + (j0 - t0)).astype(F32)
            dd = qry_t - (key_i + j0)
            lowest = jnp.where(live, 0, 1 << 30)
            for gi in range(2):
                if mode == "sel":
                    chosen = selt_ref[gi, pl.ds(kj * nb + nb - 1, 1), :]
                    for c in range(nb - 2, -1, -1):
                        chosen = jnp.where(key_i < (c + 1) * SLC_BLOCK,
                                           selt_ref[gi, pl.ds(kj * nb + c, 1), :], chosen)
                    mask = (chosen > 0.5) & (dd >= lowest)
                else:
                    mask = (dd >= lowest) & (dd < WINDOW)
                ps = []
                alphas = []
                for h in range(HPG):
                    shift = slopes_ref[(gp * 2 + gi) * HPG + h] * rel0
                    cols = slice(h * tq, (h + 1) * tq)
                    x = jnp.where(mask, src_ref[u * 2 + gi, :, cols] + bias_ref[gi * HPG + h], MASKED)
                    m_old = m_ref[gi, :, cols]
                    m_new = jnp.maximum(m_old, jnp.max(x, axis=0, keepdims=True) + shift)
                    p = jnp.exp(x - (m_new - shift))
                    alpha = jnp.exp(m_old - m_new)
                    l_ref[gi, :, cols] = alpha * l_ref[gi, :, cols] + jnp.sum(p, axis=0, keepdims=True)
                    m_ref[gi, :, cols] = m_new
                    ps.append(p.astype(BF16))
                    alphas.append(alpha)
                vt = vt_ref[kj, gi * DK:(gi + 1) * DK, :]
                pv = _dot(vt, jnp.concatenate(ps, axis=1))
                pvs[gi].append((jnp.concatenate(alphas, axis=1), pv))
        for gi in range(2):
            acc = acc_ref[gi]
            for alpha, pv in pvs[gi]:
                acc = acc * alpha + pv
            acc_ref[gi] = acc

    def body(step, carry):
        scores(step, s_ref)
        softmax_update(step, s_ref)
        return carry

    if mode == "sel":
        assert nsub & (nsub - 1) == 0
        lax.fori_loop(0, (n_live + nsub - 1) >> (nsub.bit_length() - 1), body, 0)
    else:
        body(0, 0)

    outs = []
    for gi in range(2):
        o_t = acc_ref[gi] / l_ref[gi]
        for h in range(HPG):
            row = (gp * 2 * HPG + gi * HPG + h) * 3 + branch
            outs.append(o_t[:, h * tq:(h + 1) * tq] * gate_ref[pl.ds(row, 1), :])
    for pair, tile in enumerate(_untranspose_pairs(jnp.concatenate(outs, axis=1), tq)):
        o_ref[:, pair * LANES:(pair + 1) * LANES] = tile


def _flash(proj, selt, cnt, slopes, bsz, seq, mode):
    tq = ts = KEY_TILE
    nsub = 4 if mode == "sel" else WINDOW // ts + 1
    nq = seq // tq
    nk = seq // ts
    nslc = seq // SLC_BLOCK
    kcol = (COL_KV + (2 if mode == "sel" else 4) * NSA_KV_WIDTH) // LANES
    vcol = kcol + NSA_KV_WIDTH // LANES
    smem = pltpu.SMEM
    in_specs = [pl.BlockSpec(memory_space=smem)]
    args = [slopes]
    if mode == "sel":
        in_specs.append(pl.BlockSpec((None, None, 2, nk), lambda b, g, i: (b, i, 0, 0), memory_space=smem))
        args.append(cnt)
    in_specs += [pl.BlockSpec((tq, 2 * HPG * DK), lambda b, g, i: (b * nq + i, g)),
                 pl.BlockSpec((seq, LANES), lambda b, g, i: (b, kcol + g)),
                 pl.BlockSpec((seq, LANES), lambda b, g, i: (b, vcol + g)),
                 pl.BlockSpec((tq, LANES), lambda b, g, i: (b * nq + i, COL_GZ // LANES))]
    args += [proj, proj, proj, proj]
    if mode == "sel":
        in_specs.append(pl.BlockSpec((2, None, nslc, tq), lambda b, g, i: (g, b, 0, i)))
        args.append(selt)
    return pl.pallas_call(
        functools.partial(_flash_kernel, mode=mode, tq=tq, ts=ts, nsub=nsub, nk=nk),
        grid=(bsz, 2, nq),
        in_specs=in_specs,
        out_specs=pl.BlockSpec((tq, 2 * HPG * DK), lambda b, g, i: (b * nq + i, g)),
        out_shape=jax.ShapeDtypeStruct((bsz * seq, NSA_WIDTH), F32),
        scratch_shapes=[pltpu.VMEM((4, nk, ts, LANES), BF16),
                        pltpu.VMEM((nk, LANES, ts), BF16),
                        pltpu.VMEM((tq, 2 * HPG * DK), BF16),
                        pltpu.VMEM((2 * HPG, ts, tq), F32),
                        pltpu.VMEM((LANES, tq), F32),
                        pltpu.VMEM((2 * nsub, ts, HPG * tq), F32),
                        pltpu.VMEM((2, 1, HPG * tq), F32),
                        pltpu.VMEM((2, 1, HPG * tq), F32),
                        pltpu.VMEM((2, DK, HPG * tq), F32),
                        pltpu.SMEM((nk + nsub,), jnp.int32)],
        compiler_params=_params(("parallel", "parallel", "arbitrary")),
        name="nsa_" + mode,
    )(*args)


def _lru_kernel(x_ref, y_ref, cw_ref, cb_ref, wa_ref, ba_ref, wi_ref, bi_ref, lam_ref, o_ref,
                tail_ref, h_ref, *, tt):
    @pl.when(pl.program_id(1) == 0)
    def _():
        tail_ref[...] = jnp.zeros_like(tail_ref)
        h_ref[...] = jnp.zeros_like(h_ref)

    x = x_ref[...]
    ext = jnp.concatenate([tail_ref[...], x], axis=0)
    xc = cb_ref[...] + x * cw_ref[CONV_WIDTH - 1:CONV_WIDTH, :]
    for k in range(CONV_WIDTH - 1):
        back = CONV_WIDTH - 1 - k
        xc = xc + ext[8 - back:8 - back + tt, :] * cw_ref[k:k + 1, :]
    tail_ref[...] = x[tt - 8:, :]

    xcb = xc.astype(BF16)
    r = _sigmoid(_dot(xcb, wa_ref[...]) + ba_ref[...])
    gate_i = _sigmoid(_dot(xcb, wi_ref[...]) + bi_ref[...])
    neg_lam = -lam_ref[...]
    softplus = jnp.maximum(neg_lam, 0.0) + jnp.log1p(jnp.exp(-jnp.abs(neg_lam)))
    log_a = -LRU_C * r * softplus
    a = jnp.exp(log_a)
    u = jnp.sqrt(jnp.tanh(-log_a) * (a * a + 1.0)) * (gate_i * xc)

    rows = lax.broadcasted_iota(jnp.int32, (tt, 1), 0)
    step = 1
    while step < tt:
        keep = rows >= step
        a_prev = jnp.where(keep, pltpu.roll(a, step, 0), 1.0)
        u_prev = jnp.where(keep, pltpu.roll(u, step, 0), 0.0)
        u = u + a * u_prev
        a = a * a_prev
        step *= 2
    hs = u + a * h_ref[...]
    h_ref[...] = hs[tt - 1:tt, :]
    o_ref[...] = hs * jax.nn.gelu(y_ref[...], approximate=True)


def _lru(proj, cw, cb, wa, ba, wi, bi, lam, bsz, seq):
    tt = min(512, seq)
    nt = seq // tt
    w = LRU_WIDTH
    xcol = COL_LRU_X // w
    ycol = COL_LRU_Y // w
    vec = pl.BlockSpec((1, w), lambda b, i: (0, 0))
    mat = pl.BlockSpec((w, w), lambda b, i: (0, 0))
    return pl.pallas_call(
        functools.partial(_lru_kernel, tt=tt),
        grid=(bsz, nt),
        in_specs=[pl.BlockSpec((tt, w), lambda b, i: (b * nt + i, xcol)),
                  pl.BlockSpec((tt, w), lambda b, i: (b * nt + i, ycol)),
                  pl.BlockSpec((CONV_WIDTH, w), lambda b, i: (0, 0)),
                  vec, mat, vec, mat, vec, vec],
        out_specs=pl.BlockSpec((tt, w), lambda b, i: (b * nt + i, 0)),
        out_shape=jax.ShapeDtypeStruct((bsz * seq, w), F32),
        scratch_shapes=[pltpu.VMEM((8, w), F32), pltpu.VMEM((1, w), F32)],
        compiler_params=_params(("parallel", "arbitrary")),
        name="rglru",
    )(proj, proj, cw, cb, wa, ba, wi, bi, lam)


def _gla_kernel(q_ref, k_ref, v_ref, r_ref, gz_ref, w2_ref, bg_ref, ng_ref, o_ref, st_ref, *, tt):
    c = GLA_CHUNK
    dh = GLA_HEAD_DIM

    @pl.when(pl.program_id(1) == 0)
    def _():
        st_ref[...] = jnp.zeros_like(st_ref)

    gate = _dot(gz_ref[...].astype(BF16), w2_ref[...]) + bg_ref[...]
    log_alpha = (jnp.minimum(gate, 0.0) - jnp.log1p(jnp.exp(-jnp.abs(gate)))) / GLA_GATE_TAU
    ri = lax.broadcasted_iota(jnp.int32, (tt, tt), 0)
    ci = lax.broadcasted_iota(jnp.int32, (tt, tt), 1)
    tri = jnp.where((ri >> CHUNK_SHIFT == ci >> CHUNK_SHIFT) & (ci <= ri), 1.0, 0.0)
    bcum_all = _dot(tri, log_alpha, precision=lax.Precision.HIGHEST)
    causal = (lax.broadcasted_iota(jnp.int32, (c, c), 1) <= lax.broadcasted_iota(jnp.int32, (c, c), 0))

    for n in range(tt // c):
        rs = slice(n * c, (n + 1) * c)
        bcum = bcum_all[rs]
        b_last = bcum[c - 1:c, :]
        e_pos = jnp.exp(bcum)
        q_t = (q_ref[rs, :] * dh ** -0.5) * e_pos
        kk = k_ref[rs, :]
        k_t = kk * jnp.exp(-bcum)
        k_end = kk * jnp.exp(b_last - bcum)
        decay = jnp.exp(b_last)
        vv = v_ref[rs, :]
        for hh in range(GLA_HEADS):
            cs = slice(hh * dh, (hh + 1) * dh)
            qh = q_t[:, cs].astype(BF16)
            vh = vv[:, cs]
            att = jnp.where(causal, _dot_nt(qh, k_t[:, cs].astype(BF16)), 0.0)
            st = st_ref[hh]
            o = _dot(att.astype(BF16), vh.astype(BF16)) + _dot_nt(qh, st.astype(BF16))
            st_ref[hh] = st * decay[:, cs] + _dot(vh.T.astype(BF16), k_end[:, cs].astype(BF16))
            o = _rms_scale(o) * ng_ref[:, cs]
            rr = r_ref[rs, cs]
            o_ref[rs, cs] = o * (rr * _sigmoid(rr))


def _gla(proj, w2pad, bg, ng, bsz, seq):
    tt = min(256, seq)
    nt = seq // tt
    w = GLA_WIDTH
    cols = [COL_GLA_Q // w, COL_GLA_K // w, COL_GLA_V // w, COL_GLA_R // w]
    seg = [pl.BlockSpec((tt, w), functools.partial(lambda b, i, cc: (b * nt + i, cc), cc=cc)) for cc in cols]
    vec = pl.BlockSpec((1, w), lambda b, i: (0, 0))
    return pl.pallas_call(
        functools.partial(_gla_kernel, tt=tt),
        grid=(bsz, nt),
        in_specs=seg + [pl.BlockSpec((tt, LANES), lambda b, i: (b * nt + i, COL_GZ // LANES)),
                        pl.BlockSpec((LANES, w), lambda b, i: (0, 0)), vec, vec],
        out_specs=pl.BlockSpec((tt, w), lambda b, i: (b * nt + i, 0)),
        out_shape=jax.ShapeDtypeStruct((bsz * seq, w), F32),
        scratch_shapes=[pltpu.VMEM((GLA_HEADS, GLA_HEAD_DIM, GLA_HEAD_DIM), F32)],
        compiler_params=_params(("parallel", "arbitrary")),
        name="gla",
    )(proj, proj, proj, proj, proj, w2pad, bg, ng)


def _out_proj_kernel(ocmp_ref, osel_ref, owin_ref, olru_ref, ogla_ref, h_ref, w_ref, g_ref, o_ref):
    nsa = ocmp_ref[...] + osel_ref[...] + owin_ref[...]
    y = _dot(nsa.astype(BF16), w_ref[0:NSA_WIDTH, :])
    y = y + _dot(olru_ref[...].astype(BF16), w_ref[NSA_WIDTH:NSA_WIDTH + LRU_WIDTH, :])
    y = y + _dot(ogla_ref[...].astype(BF16), w_ref[NSA_WIDTH + LRU_WIDTH:, :])
    o_ref[...] = h_ref[...] + _rms_scale(y) * g_ref[...]


def _out_proj(ocmp, osel, owin, olru, ogla, h2d, w, g):
    m = h2d.shape[0]
    tm = min(256, m)
    row = lambda width: pl.BlockSpec((tm, width), lambda i: (i, 0))
    return pl.pallas_call(
        _out_proj_kernel,
        grid=(m // tm,),
        in_specs=[row(NSA_WIDTH), row(NSA_WIDTH), row(NSA_WIDTH), row(LRU_WIDTH), row(GLA_WIDTH), row(D_MODEL),
                  pl.BlockSpec((D_MODEL, D_MODEL), lambda i: (0, 0)),
                  pl.BlockSpec((1, D_MODEL), lambda i: (0, 0))],
        out_specs=row(D_MODEL),
        out_shape=jax.ShapeDtypeStruct((m, D_MODEL), F32),
        compiler_params=_params(("parallel",)),
        name="out_proj",
    )(ocmp, osel, owin, olru, ogla, h2d, w, g)


def _mlp_kernel(h_ref, gpre_ref, wu_ref, wd_ref, gpost_ref, o_ref, un_ref, acc_ref):
    f = pl.program_id(1)

    @pl.when(f == 0)
    def _():
        un_ref[...] = (_rms_scale(h_ref[...]) * gpre_ref[...]).astype(BF16)
        acc_ref[...] = jnp.zeros_like(acc_ref)

    a = jnp.maximum(_dot(un_ref[...], wu_ref[...].astype(BF16)), 0.0)
    acc_ref[...] += _dot((a * a).astype(BF16), wd_ref[...].astype(BF16))

    @pl.when(f == pl.num_programs(1) - 1)
    def _():
        o_ref[...] = h_ref[...] + _rms_scale(acc_ref[...]) * gpost_ref[...]


def _mlp(h2d, gpre, w_up, w_down, gpost, layer):
    m = h2d.shape[0]
    tm = min(1024, m)
    tf = 512
    vec = pl.BlockSpec((1, D_MODEL), lambda i, f: (0, 0))
    return pl.pallas_call(
        _mlp_kernel,
        grid=(m // tm, D_FF // tf),
        in_specs=[pl.BlockSpec((tm, D_MODEL), lambda i, f: (i, 0), pipeline_mode=pl.Buffered(1)), vec,
                  pl.BlockSpec((None, D_MODEL, tf), lambda i, f: (layer, 0, f)),
                  pl.BlockSpec((None, tf, D_MODEL), lambda i, f: (layer, f, 0)), vec],
        out_specs=pl.BlockSpec((tm, D_MODEL), lambda i, f: (i, 0), pipeline_mode=pl.Buffered(1)),
        out_shape=jax.ShapeDtypeStruct((m, D_MODEL), F32),
        scratch_shapes=[pltpu.VMEM((tm, D_MODEL), BF16), pltpu.VMEM((tm, D_MODEL), F32)],
        compiler_params=_params(("parallel", "arbitrary")),
        name="mlp",
    )(h2d, gpre, w_up, w_down, gpost)


def _ple_kernel(h_ref, p_ref, wg_ref, wp_ref, o_ref):
    h = h_ref[...]
    gate = _sigmoid(_dot(h.astype(BF16), wg_ref[...]))
    o_ref[...] = h + gate * _dot(p_ref[...].astype(BF16), wp_ref[...])


def _ple(h2d, p2d, wg, wp):
    m = h2d.shape[0]
    tm = min(512, m)
    return pl.pallas_call(
        _ple_kernel,
        grid=(m // tm,),
        in_specs=[pl.BlockSpec((tm, D_MODEL), lambda i: (i, 0)),
                  pl.BlockSpec((tm, PLE_DIM), lambda i: (i, 0)),
                  pl.BlockSpec((D_MODEL, D_MODEL), lambda i: (0, 0)),
                  pl.BlockSpec((PLE_DIM, D_MODEL), lambda i: (0, 0))],
        out_specs=pl.BlockSpec((tm, D_MODEL), lambda i: (i, 0)),
        out_shape=jax.ShapeDtypeStruct((m, D_MODEL), F32),
        compiler_params=_params(("parallel",)),
        name="ple",
    )(h2d, p2d, wg, wp)


def _permute_w_in(w):
    g0 = NSA_WIDTH + 6 * NSA_KV_WIDTH
    g1 = g0 + N_GATES
    z0 = g1 + 2 * LRU_WIDTH + 4 * GLA_WIDTH
    pad = jnp.zeros((w.shape[0], D_IN_PAD - w.shape[1]), BF16)
    parts = [w[:, :g0], w[:, g1:z0], w[:, g0:g1], w[:, z0:]]
    return jnp.concatenate([part.astype(BF16) for part in parts] + [pad], axis=1)


def _block_diag(w):
    eye = jnp.eye(LRU_BLOCKS, dtype=w.dtype)
    return jnp.einsum("ncd,nm->ncmd", w, eye).reshape(LRU_WIDTH, LRU_WIDTH).astype(BF16)


def _layer(h2d, p2d, bsz, seq, layer, w_up_all, w_down_all, norm_mix_pre, w_in, nsa_cmp_w, nsa_cmp_pe,
           lru_conv_w, lru_conv_b, lru_wa, lru_ba, lru_wi, lru_bi, lru_lambda, gla_w_gate2, gla_b_gate,
           gla_norm, w_out, norm_mix_post, norm_mlp_pre, norm_mlp_post, w_ple_gate, w_ple):
    row = lambda v: v.reshape(1, -1)
    proj = _in_proj(h2d, row(norm_mix_pre), _permute_w_in(w_in))

    nc = seq // CMP_STRIDE
    cmp_in = proj[:, COL_KV:COL_KV + 2 * NSA_KV_WIDTH].reshape(bsz, nc, CMP_STRIDE, 2 * NSA_KV_HEADS, DK)
    cmp_in = cmp_in.transpose(0, 3, 1, 2, 4).reshape(bsz, 2 * NSA_KV_HEADS, nc, CMP_STRIDE * DK)
    cmp_kv = _compress(cmp_in, nsa_cmp_pe.reshape(2, 1, CMP_LEN * DK), nsa_cmp_w.astype(BF16))
    o_cmp, selt, cnt = _cmp_attn(proj, cmp_kv, bsz, seq)
    slopes = jnp.asarray(SLOPES, F32)
    o_sel = _flash(proj, selt, cnt, slopes, bsz, seq, "sel")
    o_win = _flash(proj, None, None, slopes, bsz, seq, "win")

    o_lru = _lru(proj, lru_conv_w, row(lru_conv_b), _block_diag(lru_wa), row(lru_ba),
                 _block_diag(lru_wi), row(lru_bi), row(lru_lambda), bsz, seq)

    w2pad = jnp.zeros((LANES, GLA_WIDTH), F32).at[N_GATES:N_GATES + GLA_GATE_RANK].set(gla_w_gate2)
    o_gla = _gla(proj, w2pad.astype(BF16), row(gla_b_gate), row(gla_norm), bsz, seq)

    h2d = _out_proj(o_cmp, o_sel, o_win, o_lru, o_gla, h2d, w_out.astype(BF16), row(norm_mix_post))
    h2d = _mlp(h2d, row(norm_mlp_pre), w_up_all, w_down_all, row(norm_mlp_post), layer)
    return _ple(h2d, p2d, w_ple_gate.astype(BF16), w_ple.astype(BF16))


def kernel(x, p, norm_mix_pre, w_in, nsa_cmp_w, nsa_cmp_pe, lru_conv_w, lru_conv_b, lru_wa, lru_ba,
           lru_wi, lru_bi, lru_lambda, gla_w_gate2, gla_b_gate, gla_norm, w_out, norm_mix_post,
           norm_mlp_pre, w_up, w_down, norm_mlp_post, w_ple_gate, w_ple):
    bsz, seq, _ = x.shape
    h2d = x.reshape(bsz * seq, D_MODEL)
    weights = (norm_mix_pre, w_in, nsa_cmp_w, nsa_cmp_pe, lru_conv_w, lru_conv_b, lru_wa, lru_ba,
               lru_wi, lru_bi, lru_lambda, gla_w_gate2, gla_b_gate, gla_norm, w_out, norm_mix_post,
               norm_mlp_pre, norm_mlp_post, w_ple_gate, w_ple)
    for i in range(p.shape[0]):
        h2d = _layer(h2d, p[i].reshape(bsz * seq, PLE_DIM), bsz, seq, i, w_up, w_down,
                     *(w[i] for w in weights))
    return h2d.reshape(bsz, seq, D_MODEL)
```

```python
import functools

import jax
import jax.numpy as jnp
from jax import lax
from jax.experimental import pallas as pl
from jax.experimental.pallas import tpu as pltpu

F32 = jnp.float32
BF16 = jnp.bfloat16

D_MODEL = 2048
PLE_DIM = 256
NSA_HEADS = 16
NSA_KV_HEADS = 4
HPG = NSA_HEADS // NSA_KV_HEADS
NSA_WIDTH = 1024
DK = 64
NSA_KV_WIDTH = NSA_KV_HEADS * DK
CMP_LEN = 32
CMP_STRIDE = 16
SLC_BLOCK = 64
SLC_SHIFT = 6
N_SELECT = 16
WINDOW = 512
LRU_WIDTH = 512
LRU_BLOCKS = 8
LRU_BLOCK_DIM = 64
CONV_WIDTH = 4
LRU_C = 8.0
GLA_WIDTH = 512
GLA_HEADS = 4
GLA_HEAD_DIM = 128
GLA_GATE_RANK = 16
GLA_GATE_TAU = 16.0
GLA_CHUNK = 64
CHUNK_SHIFT = 6
D_FF = 4 * D_MODEL
EPS = 1e-6
NEG_INF = -1e30
MASKED = 2.0 * NEG_INF

COL_Q = 0
COL_KV = 1024
COL_LRU_X = 2560
COL_LRU_Y = 3072
COL_GLA_Q = 3584
COL_GLA_K = 4096
COL_GLA_V = 4608
COL_GLA_R = 5120
COL_GZ = 5632
D_IN_PAD = 6144
N_GATES = 3 * NSA_HEADS
LANES = 128
KEY_TILE = 128
VT_ROWS = DK + 16
LOG2E = 1.4426950408889634

SLOPES = tuple(2.0 ** (-8.0 * i / NSA_HEADS) for i in range(1, NSA_HEADS + 1))

VMEM_LIMIT = 56 * 1024 * 1024

NT_DIMS = (((1,), (1,)), ((), ()))
TN_DIMS = (((0,), (0,)), ((), ()))


def _params(sem):
    return pltpu.CompilerParams(dimension_semantics=sem, vmem_limit_bytes=VMEM_LIMIT)


def _dot(a, b, **kw):
    return jnp.dot(a, b, preferred_element_type=F32, **kw)


def _dot_nt(a, b, **kw):
    return lax.dot_general(a, b, NT_DIMS, preferred_element_type=F32, **kw)


def _dot_tn(a, b, **kw):
    return lax.dot_general(a, b, TN_DIMS, preferred_element_type=F32, **kw)


def _sigmoid(x):
    return 1.0 / (1.0 + jnp.exp(-x))


def _rms_scale(x):
    return x * lax.rsqrt(jnp.mean(x * x, axis=-1, keepdims=True) + EPS)


def _half_masks(rows):
    lane = lax.broadcasted_iota(jnp.int32, (rows, LANES), 1)
    return lane < DK, lane >= DK


def _pad_halves(blk):
    lo, hi = _half_masks(blk.shape[0])
    swapped = pltpu.roll(blk, DK, 1)
    zero = jnp.zeros_like(blk)
    return (jnp.where(lo, blk, zero), jnp.where(hi, swapped, zero),
            jnp.where(lo, swapped, zero), jnp.where(hi, blk, zero))


def _untranspose_pairs(o_t, tq):
    sub = lax.broadcasted_iota(jnp.int32, (4 * DK, LANES), 0)
    lane = lax.broadcasted_iota(jnp.int32, (4 * DK, LANES), 1)
    place = jnp.where(lane == (sub & (DK - 1)) + jnp.where(sub >= 2 * DK, DK, 0), 1.0, 0.0).astype(BF16)
    out = []
    for pair in range(o_t.shape[1] // (2 * tq)):
        parts = []
        for half in range(2):
            v = o_t[:, (2 * pair + half) * tq:(2 * pair + half + 1) * tq]
            v_hi = v.astype(BF16)
            parts += [v_hi, (v - v_hi.astype(F32)).astype(BF16)]
        out.append(_dot_tn(jnp.concatenate(parts, axis=0), place))
    return out


def _in_proj_kernel(x_ref, g_ref, w_ref, o_ref, xn_ref):
    @pl.when(pl.program_id(1) == 0)
    def _():
        xn_ref[...] = (_rms_scale(x_ref[...]) * g_ref[...]).astype(BF16)

    o_ref[...] = _dot(xn_ref[...], w_ref[...])


def _in_proj(h2d, g, w):
    m, k = h2d.shape
    n = w.shape[1]
    tm = min(1024, m)
    tn = 1024
    return pl.pallas_call(
        _in_proj_kernel,
        grid=(m // tm, n // tn),
        in_specs=[pl.BlockSpec((tm, k), lambda i, j: (i, 0)),
                  pl.BlockSpec((1, k), lambda i, j: (0, 0)),
                  pl.BlockSpec((k, tn), lambda i, j: (0, j))],
        out_specs=pl.BlockSpec((tm, tn), lambda i, j: (i, j)),
        out_shape=jax.ShapeDtypeStruct((m, n), F32),
        scratch_shapes=[pltpu.VMEM((tm, k), BF16)],
        compiler_params=_params(("parallel", "arbitrary")),
        name="in_proj",
    )(h2d, g, w)


def _compress_kernel(x_ref, pe_ref, w_ref, o_ref, *, nc):
    half = CMP_STRIDE * DK
    res = []
    for j in range(2):
        x = x_ref[j]
        a = _dot((x + pe_ref[:, :half]).astype(BF16), w_ref[:half, :])
        b = _dot((x + pe_ref[:, half:]).astype(BF16), w_ref[half:, :])
        res.append(a + pltpu.roll(b, nc - 1, 0))
    o_ref[...] = jnp.concatenate(res, axis=1)


def _compress(x4, pe, w):
    b, _, nc, kk = x4.shape
    return pl.pallas_call(
        functools.partial(_compress_kernel, nc=nc),
        grid=(b, NSA_KV_HEADS),
        in_specs=[pl.BlockSpec((None, 2, nc, kk), lambda i, j: (i, j, 0, 0)),
                  pl.BlockSpec((None, 1, CMP_LEN * DK), lambda i, j: (j // 2, 0, 0)),
                  pl.BlockSpec((None, CMP_LEN * DK, DK), lambda i, j: (j // 2, 0, 0))],
        out_specs=pl.BlockSpec((None, nc, LANES), lambda i, j: (i, 0, j)),
        out_shape=jax.ShapeDtypeStruct((b, nc, 2 * NSA_KV_WIDTH), F32),
        compiler_params=_params(("parallel", "parallel")),
        name="nsa_compress",
    )(x4, pe, w)


def _cmp_attn_kernel(q_ref, kv_ref, gz_ref, ocmp_ref, selt_ref, cnt_ref, kz_ref, vct_ref, bias_ref, s_ref,
                     *, tq, nc, nslc, nsel, nb):
    t0 = pl.program_id(1) * tq

    @pl.when(pl.program_id(1) == 0)
    def _():
        for gp in range(NSA_KV_HEADS // 2):
            padded = _pad_halves(kv_ref[:, gp * LANES:(gp + 1) * LANES])
            for idx in range(4):
                kz_ref[gp * 4 + idx] = padded[idx].astype(BF16)
            vct_ref[gp] = kv_ref[:, NSA_KV_WIDTH + gp * LANES:NSA_KV_WIDTH + (gp + 1) * LANES].T.astype(BF16)
        end_n = (lax.broadcasted_iota(jnp.int32, (nc, tq), 0) * CMP_STRIDE + (CMP_LEN - 1)).astype(F32)
        for hh in range(NSA_HEADS):
            bias_ref[hh] = (SLOPES[hh] * LOG2E) * end_n

    qb = (q_ref[...] * (DK ** -0.5 * LOG2E)).astype(BF16)
    for hh in range(NSA_HEADS):
        pair = hh // 2
        s_ref[hh] = _dot_nt(kz_ref[(hh // HPG) * 2 + hh % 2], qb[:, pair * LANES:(pair + 1) * LANES])

    n_row = lax.broadcasted_iota(jnp.int32, (nc, tq), 0)
    t_col = lax.broadcasted_iota(jnp.int32, (nc, tq), 1) + t0
    valid = t_col >= n_row * CMP_STRIDE + (CMP_LEN - 1)
    any_valid = jnp.where(t_col[0:1, :] >= CMP_LEN - 1, 1.0, 0.0)
    jj = lax.broadcasted_iota(jnp.int32, (nslc, nc), 0) * SLC_BLOCK
    nn = lax.broadcasted_iota(jnp.int32, (nslc, nc), 1) * CMP_STRIDE
    overlap_t = jnp.where((nn < jj + SLC_BLOCK) & (jj < nn + CMP_LEN), 1.0, 0.0)
    blk = lax.broadcasted_iota(jnp.int32, (nslc, tq), 0)
    t_blk = (lax.broadcasted_iota(jnp.int32, (nslc, tq), 1) + t0) >> SLC_SHIFT
    forced = (blk == 0) | (blk == t_blk) | (blk == t_blk - 1)
    future = blk > t_blk
    sub = lax.broadcasted_iota(jnp.int32, (8, tq), 0)
    ngrp = nslc // 8
    gate_t = _sigmoid(gz_ref[...].T)
    ones = jnp.ones((8, tq), BF16)

    outs = []
    for g in range(NSA_KV_HEADS):
        vct = vct_ref[g // 2, (g % 2) * DK:(g % 2 + 1) * DK, :]
        psum = jnp.zeros((nc, tq), F32)
        for h in range(HPG):
            hh = g * HPG + h
            s = jnp.where(valid, s_ref[hh] + bias_ref[hh], NEG_INF)
            e = jnp.exp2(s - jnp.max(s, axis=0, keepdims=True))
            p = e * (any_valid / jnp.sum(e, axis=0, keepdims=True))
            outs.append(_dot(vct, p.astype(BF16)) * gate_t[3 * hh:3 * hh + 1, :])
            psum = psum + p
        imp_t = _dot(overlap_t, psum, precision=lax.Precision.HIGHEST)
        score = jnp.where(future, -1.0, jnp.where(forced, 1e4, imp_t))

        sc = [score[8 * r:8 * r + 8] for r in range(ngrp)]
        rank = [jnp.zeros((8, tq), F32) for _ in range(ngrp)]
        for i in range(nslc):
            si = jnp.broadcast_to(score[i:i + 1, :], (8, tq))
            for r in range(ngrp):
                if r < i // 8:
                    before = si > sc[r]
                elif r > i // 8:
                    before = si >= sc[r]
                else:
                    before = (si > sc[r]) | ((sub > i % 8) & (si >= sc[r]))
                rank[r] = rank[r] + jnp.where(before, 1.0, 0.0)
        for r in range(ngrp):
            selt_ref[g, 8 * r:8 * r + 8, :] = jnp.where(rank[r] < float(nsel), 1.0, 0.0)
        if g % 2:
            picks = None
            for gi in (g - 1, g):
                for c in range(nb):
                    part = selt_ref[gi, pl.ds(c, nslc // nb, stride=nb), :]
                    picks = part if picks is None else picks + part
            cnt = _dot_nt(ones, picks.astype(BF16))
            cnt_ref[g // 2:g // 2 + 1, :] = cnt[0:1, :].astype(jnp.int32)

    for pair, tile in enumerate(_untranspose_pairs(jnp.concatenate(outs, axis=1), tq)):
        ocmp_ref[:, pair * LANES:(pair + 1) * LANES] = tile


def _cmp_attn(proj, cmp_kv, bsz, seq):
    tq = 128
    nc = cmp_kv.shape[1]
    nslc = seq // SLC_BLOCK
    nsel = min(N_SELECT, nslc)
    nq = seq // tq
    return pl.pallas_call(
        functools.partial(_cmp_attn_kernel, tq=tq, nc=nc, nslc=nslc, nsel=nsel, nb=KEY_TILE // SLC_BLOCK),
        grid=(bsz, nq),
        in_specs=[pl.BlockSpec((tq, NSA_WIDTH), lambda b, i: (b * nq + i, 0)),
                  pl.BlockSpec((None, nc, 2 * NSA_KV_WIDTH), lambda b, i: (b, 0, 0)),
                  pl.BlockSpec((tq, LANES), lambda b, i: (b * nq + i, COL_GZ // LANES))],
        out_specs=[pl.BlockSpec((tq, NSA_WIDTH), lambda b, i: (b * nq + i, 0)),
                   pl.BlockSpec((NSA_KV_HEADS, None, nslc, tq), lambda b, i: (0, b, 0, i)),
                   pl.BlockSpec((None, None, NSA_KV_HEADS // 2, seq // KEY_TILE), lambda b, i: (b, i, 0, 0))],
        out_shape=[jax.ShapeDtypeStruct((bsz * seq, NSA_WIDTH), F32),
                   jax.ShapeDtypeStruct((NSA_KV_HEADS, bsz, nslc, seq), F32),
                   jax.ShapeDtypeStruct((bsz, nq, NSA_KV_HEADS // 2, seq // KEY_TILE), jnp.int32)],
        scratch_shapes=[pltpu.VMEM((2 * NSA_KV_HEADS, nc, LANES), BF16),
                        pltpu.VMEM((NSA_KV_HEADS // 2, LANES, nc), BF16),
                        pltpu.VMEM((NSA_HEADS, nc, tq), F32),
                        pltpu.VMEM((NSA_HEADS, nc, tq), F32)],
        compiler_params=_params(("parallel", "arbitrary")),
        name="nsa_cmp_attn",
    )(proj, cmp_kv, proj)


def _flash_kernel(*refs, mode, tq, ts, nsub, nk):
    if mode == "sel":
        slopes_ref, cnt_ref, q_ref, k_ref, v_ref, gz_ref, selt_ref, o_ref = refs[:8]
    else:
        slopes_ref, q_ref, k_ref, v_ref, gz_ref, o_ref = refs[:6]
    kz_ref, vt_ref, qb_ref, bias_ref, gate_ref, s_ref, m_ref, acc_ref, idx_ref = refs[-9:]
    branch = 1 if mode == "sel" else 2
    gp = pl.program_id(1)
    qi = pl.program_id(2)
    t0 = qi * tq
    nb = ts // SLC_BLOCK
    ts_shift = ts.bit_length() - 1

    @pl.when(qi == 0)
    def _():
        for c in range(nk):
            padded = _pad_halves(k_ref[c * ts:(c + 1) * ts, :])
            for idx in range(4):
                kz_ref[idx, c] = padded[idx].astype(BF16)
            v_t = v_ref[c * ts:(c + 1) * ts, :].T
            ones_row = jnp.where(lax.broadcasted_iota(jnp.int32, (VT_ROWS - DK, ts), 0) == 0, 1.0, 0.0)
            for gi in range(2):
                vt_ref[gi, c, 0:DK, :] = v_t[gi * DK:(gi + 1) * DK, :].astype(BF16)
                vt_ref[gi, c, DK:VT_ROWS, :] = ones_row.astype(BF16)

    last_sub = (t0 + tq - 1) >> ts_shift
    if mode == "sel":
        def build(j, n):
            idx_ref[n] = j
            return n + jnp.where(((cnt_ref[gp, j] > 0) | (j == last_sub)) & (j <= last_sub), 1, 0)

        n_live = lax.fori_loop(0, nk, build, 0, unroll=4)
    else:
        first_sub = jnp.maximum(t0 - (WINDOW - 1), 0) >> ts_shift
        n_live = last_sub + 1 - first_sub
        for u in range(nsub):
            idx_ref[u] = jnp.minimum(first_sub + u, last_sub)

    key_i = lax.broadcasted_iota(jnp.int32, (ts, tq), 0)
    qry_t = lax.broadcasted_iota(jnp.int32, (ts, tq), 1) + t0
    key_f = key_i.astype(F32)
    qb_ref[...] = (q_ref[...] * (DK ** -0.5 * LOG2E)).astype(BF16)
    gate_ref[...] = _sigmoid(gz_ref[...].T)
    for gh in range(2 * HPG):
        bias_ref[gh] = (slopes_ref[gp * 2 * HPG + gh] * LOG2E) * key_f
    m_ref[...] = jnp.full(m_ref.shape, NEG_INF, F32)
    acc_ref[...] = jnp.zeros(acc_ref.shape, F32)

    def sub_tile(step, u):
        pos = step * nsub + u
        return idx_ref[jnp.minimum(pos, n_live - 1)], pos < n_live

    def scores(step, dst_ref):
        for u in range(nsub):
            kj, _ = sub_tile(step, u)
            for gi in range(2):
                for h in range(HPG):
                    pair = gi * (HPG // 2) + h // 2
                    dst_ref[u * 2 + gi, :, h * tq:(h + 1) * tq] = _dot_nt(
                        kz_ref[gi * 2 + h % 2, kj], qb_ref[:, pair * LANES:(pair + 1) * LANES])

    def softmax_update(step, src_ref):
        pvs = [[], []]
        for u in range(nsub):
            kj, live = sub_tile(step, u)
            j0 = kj * ts
            rel0 = (jnp.zeros((1, tq), jnp.int32) + (j0 - t0)).astype(F32)
            dd = qry_t - (key_i + j0)
            lowest = jnp.where(live, 0, 1 << 30)
            for gi in range(2):
                if mode == "sel":
                    chosen = selt_ref[gi, pl.ds(kj * nb + nb - 1, 1), :]
                    for c in range(nb - 2, -1, -1):
                        chosen = jnp.where(key_i < (c + 1) * SLC_BLOCK,
                                           selt_ref[gi, pl.ds(kj * nb + c, 1), :], chosen)
                    mask = (chosen > 0.5) & (dd >= lowest)
                else:
                    mask = (dd >= lowest) & (dd < WINDOW)
                ps = []
                alphas = []
                for h in range(HPG):
                    shift = (slopes_ref[(gp * 2 + gi) * HPG + h] * LOG2E) * rel0
                    cols = slice(h * tq, (h + 1) * tq)
                    x = jnp.where(mask, src_ref[u * 2 + gi, :, cols] + bias_ref[gi * HPG + h], MASKED)
                    m_old = m_ref[gi, :, cols]
                    m_new = jnp.maximum(m_old, jnp.max(x, axis=0, keepdims=True) + shift)
                    p = jnp.exp2(x - (m_new - shift))
                    m_ref[gi, :, cols] = m_new
                    ps.append(p.astype(BF16))
                    alphas.append(jnp.exp2(m_old - m_new))
                pv = _dot(vt_ref[gi, kj], jnp.concatenate(ps, axis=1))
                pvs[gi].append((jnp.concatenate(alphas, axis=1), pv))
        for gi in range(2):
            acc = acc_ref[gi]
            for alpha, pv in pvs[gi]:
                acc = acc * alpha + pv
            acc_ref[gi] = acc

    def body(step, carry):
        scores(step, s_ref)
        softmax_update(step, s_ref)
        return carry

    if mode == "sel":
        assert nsub & (nsub - 1) == 0
        lax.fori_loop(0, (n_live + nsub - 1) >> (nsub.bit_length() - 1), body, 0)
    else:
        body(0, 0)

    outs = []
    for gi in range(2):
        o_t = acc_ref[gi, 0:DK, :] / acc_ref[gi, DK:DK + 1, :]
        for h in range(HPG):
            row = (gp * 2 * HPG + gi * HPG + h) * 3 + branch
            outs.append(o_t[:, h * tq:(h + 1) * tq] * gate_ref[pl.ds(row, 1), :])
    for pair, tile in enumerate(_untranspose_pairs(jnp.concatenate(outs, axis=1), tq)):
        o_ref[:, pair * LANES:(pair + 1) * LANES] = tile


def _flash(proj, selt, cnt, slopes, bsz, seq, mode):
    tq = ts = KEY_TILE
    nsub = 4 if mode == "sel" else WINDOW // ts + 1
    nq = seq // tq
    nk = seq // ts
    nslc = seq // SLC_BLOCK
    kcol = (COL_KV + (2 if mode == "sel" else 4) * NSA_KV_WIDTH) // LANES
    vcol = kcol + NSA_KV_WIDTH // LANES
    smem = pltpu.SMEM
    in_specs = [pl.BlockSpec(memory_space=smem)]
    args = [slopes]
    if mode == "sel":
        in_specs.append(pl.BlockSpec((None, None, 2, nk), lambda b, g, i: (b, i, 0, 0), memory_space=smem))
        args.append(cnt)
    in_specs += [pl.BlockSpec((tq, 2 * HPG * DK), lambda b, g, i: (b * nq + i, g)),
                 pl.BlockSpec((seq, LANES), lambda b, g, i: (b, kcol + g)),
                 pl.BlockSpec((seq, LANES), lambda b, g, i: (b, vcol + g)),
                 pl.BlockSpec((tq, LANES), lambda b, g, i: (b * nq + i, COL_GZ // LANES))]
    args += [proj, proj, proj, proj]
    if mode == "sel":
        in_specs.append(pl.BlockSpec((2, None, nslc, tq), lambda b, g, i: (g, b, 0, i)))
        args.append(selt)
    return pl.pallas_call(
        functools.partial(_flash_kernel, mode=mode, tq=tq, ts=ts, nsub=nsub, nk=nk),
        grid=(bsz, 2, nq),
        in_specs=in_specs,
        out_specs=pl.BlockSpec((tq, 2 * HPG * DK), lambda b, g, i: (b * nq + i, g)),
        out_shape=jax.ShapeDtypeStruct((bsz * seq, NSA_WIDTH), F32),
        scratch_shapes=[pltpu.VMEM((4, nk, ts, LANES), BF16),
                        pltpu.VMEM((2, nk, VT_ROWS, ts), BF16),
                        pltpu.VMEM((tq, 2 * HPG * DK), BF16),
                        pltpu.VMEM((2 * HPG, ts, tq), F32),
                        pltpu.VMEM((LANES, tq), F32),
                        pltpu.VMEM((2 * nsub, ts, HPG * tq), F32),
                        pltpu.VMEM((2, 1, HPG * tq), F32),
                        pltpu.VMEM((2, VT_ROWS, HPG * tq), F32),
                        pltpu.SMEM((nk + nsub,), jnp.int32)],
        compiler_params=_params(("parallel", "parallel", "arbitrary")),
        name="nsa_" + mode,
    )(*args)


def _lru_kernel(x_ref, y_ref, cw_ref, cb_ref, wa_ref, ba_ref, wi_ref, bi_ref, lam_ref, o_ref,
                tail_ref, h_ref, *, tt):
    @pl.when(pl.program_id(1) == 0)
    def _():
        tail_ref[...] = jnp.zeros_like(tail_ref)
        h_ref[...] = jnp.zeros_like(h_ref)

    x = x_ref[...]
    ext = jnp.concatenate([tail_ref[...], x], axis=0)
    xc = cb_ref[...] + x * cw_ref[CONV_WIDTH - 1:CONV_WIDTH, :]
    for k in range(CONV_WIDTH - 1):
        back = CONV_WIDTH - 1 - k
        xc = xc + ext[8 - back:8 - back + tt, :] * cw_ref[k:k + 1, :]
    tail_ref[...] = x[tt - 8:, :]

    xcb = xc.astype(BF16)
    r = _sigmoid(_dot(xcb, wa_ref[...]) + ba_ref[...])
    gate_i = _sigmoid(_dot(xcb, wi_ref[...]) + bi_ref[...])
    neg_lam = -lam_ref[...]
    softplus = jnp.maximum(neg_lam, 0.0) + jnp.log1p(jnp.exp(-jnp.abs(neg_lam)))
    log_a = -LRU_C * r * softplus
    a = jnp.exp(log_a)
    u = jnp.sqrt(jnp.tanh(-log_a) * (a * a + 1.0)) * (gate_i * xc)

    rows = lax.broadcasted_iota(jnp.int32, (tt, 1), 0)
    step = 1
    while step < tt:
        keep = rows >= step
        a_prev = jnp.where(keep, pltpu.roll(a, step, 0), 1.0)
        u_prev = jnp.where(keep, pltpu.roll(u, step, 0), 0.0)
        u = u + a * u_prev
        a = a * a_prev
        step *= 2
    hs = u + a * h_ref[...]
    h_ref[...] = hs[tt - 1:tt, :]
    o_ref[...] = hs * jax.nn.gelu(y_ref[...], approximate=True)


def _lru(proj, cw, cb, wa, ba, wi, bi, lam, bsz, seq):
    tt = min(512, seq)
    nt = seq // tt
    w = LRU_WIDTH
    xcol = COL_LRU_X // w
    ycol = COL_LRU_Y // w
    vec = pl.BlockSpec((1, w), lambda b, i: (0, 0))
    mat = pl.BlockSpec((w, w), lambda b, i: (0, 0))
    return pl.pallas_call(
        functools.partial(_lru_kernel, tt=tt),
        grid=(bsz, nt),
        in_specs=[pl.BlockSpec((tt, w), lambda b, i: (b * nt + i, xcol)),
                  pl.BlockSpec((tt, w), lambda b, i: (b * nt + i, ycol)),
                  pl.BlockSpec((CONV_WIDTH, w), lambda b, i: (0, 0)),
                  vec, mat, vec, mat, vec, vec],
        out_specs=pl.BlockSpec((tt, w), lambda b, i: (b * nt + i, 0)),
        out_shape=jax.ShapeDtypeStruct((bsz * seq, w), F32),
        scratch_shapes=[pltpu.VMEM((8, w), F32), pltpu.VMEM((1, w), F32)],
        compiler_params=_params(("parallel", "arbitrary")),
        name="rglru",
    )(proj, proj, cw, cb, wa, ba, wi, bi, lam)


def _gla_kernel(q_ref, k_ref, v_ref, r_ref, gz_ref, w2_ref, bg_ref, ng_ref, o_ref, st_ref, *, tt):
    c = GLA_CHUNK
    dh = GLA_HEAD_DIM

    @pl.when(pl.program_id(1) == 0)
    def _():
        st_ref[...] = jnp.zeros_like(st_ref)

    gate = _dot(gz_ref[...].astype(BF16), w2_ref[...]) + bg_ref[...]
    log_alpha = (jnp.minimum(gate, 0.0) - jnp.log1p(jnp.exp(-jnp.abs(gate)))) / GLA_GATE_TAU
    ri = lax.broadcasted_iota(jnp.int32, (tt, tt), 0)
    ci = lax.broadcasted_iota(jnp.int32, (tt, tt), 1)
    tri = jnp.where((ri >> CHUNK_SHIFT == ci >> CHUNK_SHIFT) & (ci <= ri), 1.0, 0.0)
    bcum_all = _dot(tri, log_alpha, precision=lax.Precision.HIGHEST)
    causal = (lax.broadcasted_iota(jnp.int32, (c, c), 1) <= lax.broadcasted_iota(jnp.int32, (c, c), 0))

    for n in range(tt // c):
        rs = slice(n * c, (n + 1) * c)
        bcum = bcum_all[rs]
        b_last = bcum[c - 1:c, :]
        e_pos = jnp.exp(bcum)
        q_t = (q_ref[rs, :] * dh ** -0.5) * e_pos
        kk = k_ref[rs, :]
        k_t = kk * jnp.exp(-bcum)
        k_end = kk * jnp.exp(b_last - bcum)
        decay = jnp.exp(b_last)
        vv = v_ref[rs, :]
        for hh in range(GLA_HEADS):
            cs = slice(hh * dh, (hh + 1) * dh)
            qh = q_t[:, cs].astype(BF16)
            vh = vv[:, cs]
            att = jnp.where(causal, _dot_nt(qh, k_t[:, cs].astype(BF16)), 0.0)
            st = st_ref[hh]
            o = _dot(att.astype(BF16), vh.astype(BF16)) + _dot_nt(qh, st.astype(BF16))
            st_ref[hh] = st * decay[:, cs] + _dot(vh.T.astype(BF16), k_end[:, cs].astype(BF16))
            o = _rms_scale(o) * ng_ref[:, cs]
            rr = r_ref[rs, cs]
            o_ref[rs, cs] = o * (rr * _sigmoid(rr))


def _gla(proj, w2pad, bg, ng, bsz, seq):
    tt = min(256, seq)
    nt = seq // tt
    w = GLA_WIDTH
    cols = [COL_GLA_Q // w, COL_GLA_K // w, COL_GLA_V // w, COL_GLA_R // w]
    seg = [pl.BlockSpec((tt, w), functools.partial(lambda b, i, cc: (b * nt + i, cc), cc=cc)) for cc in cols]
    vec = pl.BlockSpec((1, w), lambda b, i: (0, 0))
    return pl.pallas_call(
        functools.partial(_gla_kernel, tt=tt),
        grid=(bsz, nt),
        in_specs=seg + [pl.BlockSpec((tt, LANES), lambda b, i: (b * nt + i, COL_GZ // LANES)),
                        pl.BlockSpec((LANES, w), lambda b, i: (0, 0)), vec, vec],
        out_specs=pl.BlockSpec((tt, w), lambda b, i: (b * nt + i, 0)),
        out_shape=jax.ShapeDtypeStruct((bsz * seq, w), F32),
        scratch_shapes=[pltpu.VMEM((GLA_HEADS, GLA_HEAD_DIM, GLA_HEAD_DIM), F32)],
        compiler_params=_params(("parallel", "arbitrary")),
        name="gla",
    )(proj, proj, proj, proj, proj, w2pad, bg, ng)


def _out_proj_kernel(ocmp_ref, osel_ref, owin_ref, olru_ref, ogla_ref, h_ref, w_ref, g_ref, o_ref):
    nsa = ocmp_ref[...] + osel_ref[...] + owin_ref[...]
    y = _dot(nsa.astype(BF16), w_ref[0:NSA_WIDTH, :])
    y = y + _dot(olru_ref[...].astype(BF16), w_ref[NSA_WIDTH:NSA_WIDTH + LRU_WIDTH, :])
    y = y + _dot(ogla_ref[...].astype(BF16), w_ref[NSA_WIDTH + LRU_WIDTH:, :])
    o_ref[...] = h_ref[...] + _rms_scale(y) * g_ref[...]


def _out_proj(ocmp, osel, owin, olru, ogla, h2d, w, g):
    m = h2d.shape[0]
    tm = min(256, m)
    row = lambda width: pl.BlockSpec((tm, width), lambda i: (i, 0))
    return pl.pallas_call(
        _out_proj_kernel,
        grid=(m // tm,),
        in_specs=[row(NSA_WIDTH), row(NSA_WIDTH), row(NSA_WIDTH), row(LRU_WIDTH), row(GLA_WIDTH), row(D_MODEL),
                  pl.BlockSpec((D_MODEL, D_MODEL), lambda i: (0, 0)),
                  pl.BlockSpec((1, D_MODEL), lambda i: (0, 0))],
        out_specs=row(D_MODEL),
        out_shape=jax.ShapeDtypeStruct((m, D_MODEL), F32),
        compiler_params=_params(("parallel",)),
        name="out_proj",
    )(ocmp, osel, owin, olru, ogla, h2d, w, g)


def _mlp_kernel(h_ref, gpre_ref, wu_ref, wd_ref, gpost_ref, o_ref, un_ref):
    f = pl.program_id(1)

    @pl.when(f == 0)
    def _():
        un_ref[...] = (_rms_scale(h_ref[...]) * gpre_ref[...]).astype(BF16)
        o_ref[...] = jnp.zeros_like(o_ref)

    a = jnp.maximum(_dot(un_ref[...], wu_ref[...].astype(BF16)), 0.0)
    o_ref[...] += _dot((a * a).astype(BF16), wd_ref[...].astype(BF16))

    @pl.when(f == pl.num_programs(1) - 1)
    def _():
        o_ref[...] = h_ref[...] + _rms_scale(o_ref[...]) * gpost_ref[...]


def _mlp(h2d, gpre, w_up, w_down, gpost, layer):
    m = h2d.shape[0]
    tm = min(1024, m)
    tf = 512
    vec = pl.BlockSpec((1, D_MODEL), lambda i, f: (0, 0))
    return pl.pallas_call(
        _mlp_kernel,
        grid=(m // tm, D_FF // tf),
        in_specs=[pl.BlockSpec((tm, D_MODEL), lambda i, f: (i, 0)), vec,
                  pl.BlockSpec((None, D_MODEL, tf), lambda i, f: (layer, 0, f)),
                  pl.BlockSpec((None, tf, D_MODEL), lambda i, f: (layer, f, 0)), vec],
        out_specs=pl.BlockSpec((tm, D_MODEL), lambda i, f: (i, 0), pipeline_mode=pl.Buffered(1)),
        out_shape=jax.ShapeDtypeStruct((m, D_MODEL), F32),
        scratch_shapes=[pltpu.VMEM((tm, D_MODEL), BF16)],
        compiler_params=_params(("parallel", "arbitrary")),
        name="mlp",
    )(h2d, gpre, w_up, w_down, gpost)


def _ple_kernel(h_ref, p_ref, wg_ref, wp_ref, o_ref):
    h = h_ref[...]
    gate = _sigmoid(_dot(h.astype(BF16), wg_ref[...]))
    o_ref[...] = h + gate * _dot(p_ref[...].astype(BF16), wp_ref[...])


def _ple(h2d, p2d, wg, wp):
    m = h2d.shape[0]
    tm = min(512, m)
    return pl.pallas_call(
        _ple_kernel,
        grid=(m // tm,),
        in_specs=[pl.BlockSpec((tm, D_MODEL), lambda i: (i, 0)),
                  pl.BlockSpec((tm, PLE_DIM), lambda i: (i, 0)),
                  pl.BlockSpec((D_MODEL, D_MODEL), lambda i: (0, 0)),
                  pl.BlockSpec((PLE_DIM, D_MODEL), lambda i: (0, 0))],
        out_specs=pl.BlockSpec((tm, D_MODEL), lambda i: (i, 0)),
        out_shape=jax.ShapeDtypeStruct((m, D_MODEL), F32),
        compiler_params=_params(("parallel",)),
        name="ple",
    )(h2d, p2d, wg, wp)


def _permute_w_in(w):
    g0 = NSA_WIDTH + 6 * NSA_KV_WIDTH
    g1 = g0 + N_GATES
    z0 = g1 + 2 * LRU_WIDTH + 4 * GLA_WIDTH
    pad = jnp.zeros((w.shape[0], D_IN_PAD - w.shape[1]), BF16)
    parts = [w[:, :g0], w[:, g1:z0], w[:, g0:g1], w[:, z0:]]
    return jnp.concatenate([part.astype(BF16) for part in parts] + [pad], axis=1)


def _block_diag(w):
    eye = jnp.eye(LRU_BLOCKS, dtype=w.dtype)
    return jnp.einsum("ncd,nm->ncmd", w, eye).reshape(LRU_WIDTH, LRU_WIDTH).astype(BF16)


def _layer(h2d, p2d, bsz, seq, layer, w_up_all, w_down_all, norm_mix_pre, w_in, nsa_cmp_w, nsa_cmp_pe,
           lru_conv_w, lru_conv_b, lru_wa, lru_ba, lru_wi, lru_bi, lru_lambda, gla_w_gate2, gla_b_gate,
           gla_norm, w_out, norm_mix_post, norm_mlp_pre, norm_mlp_post, w_ple_gate, w_ple):
    row = lambda v: v.reshape(1, -1)
    proj = _in_proj(h2d, row(norm_mix_pre), _permute_w_in(w_in))

    nc = seq // CMP_STRIDE
    cmp_in = proj[:, COL_KV:COL_KV + 2 * NSA_KV_WIDTH].reshape(bsz, nc, CMP_STRIDE, 2 * NSA_KV_HEADS, DK)
    cmp_in = cmp_in.transpose(0, 3, 1, 2, 4).reshape(bsz, 2 * NSA_KV_HEADS, nc, CMP_STRIDE * DK)
    cmp_kv = _compress(cmp_in, nsa_cmp_pe.reshape(2, 1, CMP_LEN * DK), nsa_cmp_w.astype(BF16))
    o_cmp, selt, cnt = _cmp_attn(proj, cmp_kv, bsz, seq)
    slopes = jnp.asarray(SLOPES, F32)
    o_sel = _flash(proj, selt, cnt, slopes, bsz, seq, "sel")
    o_win = _flash(proj, None, None, slopes, bsz, seq, "win")

    o_lru = _lru(proj, lru_conv_w, row(lru_conv_b), _block_diag(lru_wa), row(lru_ba),
                 _block_diag(lru_wi), row(lru_bi), row(lru_lambda), bsz, seq)

    w2pad = jnp.zeros((LANES, GLA_WIDTH), F32).at[N_GATES:N_GATES + GLA_GATE_RANK].set(gla_w_gate2)
    o_gla = _gla(proj, w2pad.astype(BF16), row(gla_b_gate), row(gla_norm), bsz, seq)

    h2d = _out_proj(o_cmp, o_sel, o_win, o_lru, o_gla, h2d, w_out.astype(BF16), row(norm_mix_post))
    h2d = _mlp(h2d, row(norm_mlp_pre), w_up_all, w_down_all, row(norm_mlp_post), layer)
    return _ple(h2d, p2d, w_ple_gate.astype(BF16), w_ple.astype(BF16))


def kernel(x, p, norm_mix_pre, w_in, nsa_cmp_w, nsa_cmp_pe, lru_conv_w, lru_conv_b, lru_wa, lru_ba,
           lru_wi, lru_bi, lru_lambda, gla_w_gate2, gla_b_gate, gla_norm, w_out, norm_mix_post,
           norm_mlp_pre, w_up, w_down, norm_mlp_post, w_ple_gate, w_ple):
    bsz, seq, _ = x.shape
    h2d = x.reshape(bsz * seq, D_MODEL)
    weights = (norm_mix_pre, w_in, nsa_cmp_w, nsa_cmp_pe, lru_conv_w, lru_conv_b, lru_wa, lru_ba,
               lru_wi, lru_bi, lru_lambda, gla_w_gate2, gla_b_gate, gla_norm, w_out, norm_mix_post,
               norm_mlp_pre, norm_mlp_post, w_ple_gate, w_ple)
    for i in range(p.shape[0]):
        h2d = _layer(h2d, p[i].reshape(bsz * seq, PLE_DIM), bsz, seq, i, w_up, w_down,
                     *(w[i] for w in weights))
    return h2d.reshape(bsz, seq, D_MODEL)
```

```python
import functools

import jax
import jax.numpy as jnp
from jax import lax
from jax.experimental import pallas as pl
from jax.experimental.pallas import tpu as pltpu

F32 = jnp.float32
BF16 = jnp.bfloat16

D_MODEL = 2048
PLE_DIM = 256
NSA_HEADS = 16
NSA_KV_HEADS = 4
HPG = NSA_HEADS // NSA_KV_HEADS
NSA_WIDTH = 1024
DK = 64
NSA_KV_WIDTH = NSA_KV_HEADS * DK
CMP_LEN = 32
CMP_STRIDE = 16
SLC_BLOCK = 64
SLC_SHIFT = 6
N_SELECT = 16
WINDOW = 512
LRU_WIDTH = 512
LRU_BLOCKS = 8
LRU_BLOCK_DIM = 64
CONV_WIDTH = 4
LRU_C = 8.0
GLA_WIDTH = 512
GLA_HEADS = 4
GLA_HEAD_DIM = 128
GLA_GATE_RANK = 16
GLA_GATE_TAU = 16.0
GLA_CHUNK = 64
CHUNK_SHIFT = 6
D_FF = 4 * D_MODEL
EPS = 1e-6
NEG_INF = -1e30
MASKED = 2.0 * NEG_INF

COL_Q = 0
COL_KV = 1024
COL_LRU_X = 2560
COL_LRU_Y = 3072
COL_GLA_Q = 3584
COL_GLA_K = 4096
COL_GLA_V = 4608
COL_GLA_R = 5120
COL_GZ = 5632
D_IN_PAD = 6144
N_GATES = 3 * NSA_HEADS
LANES = 128
KEY_TILE = 128
VT_ROWS = DK + 16
LOG2E = 1.4426950408889634

SLOPES = tuple(2.0 ** (-8.0 * i / NSA_HEADS) for i in range(1, NSA_HEADS + 1))

VMEM_LIMIT = 56 * 1024 * 1024

NT_DIMS = (((1,), (1,)), ((), ()))
TN_DIMS = (((0,), (0,)), ((), ()))


def _params(sem):
    return pltpu.CompilerParams(dimension_semantics=sem, vmem_limit_bytes=VMEM_LIMIT)


def _dot(a, b, **kw):
    return jnp.dot(a, b, preferred_element_type=F32, **kw)


def _dot_nt(a, b, **kw):
    return lax.dot_general(a, b, NT_DIMS, preferred_element_type=F32, **kw)


def _dot_tn(a, b, **kw):
    return lax.dot_general(a, b, TN_DIMS, preferred_element_type=F32, **kw)


def _sigmoid(x):
    return 1.0 / (1.0 + jnp.exp(-x))


def _rms_scale(x):
    return x * lax.rsqrt(jnp.mean(x * x, axis=-1, keepdims=True) + EPS)


def _half_masks(rows):
    lane = lax.broadcasted_iota(jnp.int32, (rows, LANES), 1)
    return lane < DK, lane >= DK


def _pad_halves(blk):
    lo, hi = _half_masks(blk.shape[0])
    swapped = pltpu.roll(blk, DK, 1)
    zero = jnp.zeros_like(blk)
    return (jnp.where(lo, blk, zero), jnp.where(hi, swapped, zero),
            jnp.where(lo, swapped, zero), jnp.where(hi, blk, zero))


def _untranspose_pairs(o_t, tq):
    sub = lax.broadcasted_iota(jnp.int32, (4 * DK, LANES), 0)
    lane = lax.broadcasted_iota(jnp.int32, (4 * DK, LANES), 1)
    place = jnp.where(lane == (sub & (DK - 1)) + jnp.where(sub >= 2 * DK, DK, 0), 1.0, 0.0).astype(BF16)
    out = []
    for pair in range(o_t.shape[1] // (2 * tq)):
        parts = []
        for half in range(2):
            v = o_t[:, (2 * pair + half) * tq:(2 * pair + half + 1) * tq]
            v_hi = v.astype(BF16)
            parts += [v_hi, (v - v_hi.astype(F32)).astype(BF16)]
        out.append(_dot_tn(jnp.concatenate(parts, axis=0), place))
    return out


def _in_proj_kernel(x_ref, g_ref, w_ref, o_ref, xn_ref):
    @pl.when(pl.program_id(1) == 0)
    def _():
        xn_ref[...] = (_rms_scale(x_ref[...]) * g_ref[...]).astype(BF16)

    o_ref[...] = _dot(xn_ref[...], w_ref[...])


def _in_proj(h2d, g, w_all, layer):
    m, k = h2d.shape
    n = w_all.shape[2]
    tm = min(1024, m)
    tn = 1024
    return pl.pallas_call(
        _in_proj_kernel,
        grid=(m // tm, n // tn),
        in_specs=[pl.BlockSpec((tm, k), lambda i, j: (i, 0)),
                  pl.BlockSpec((1, k), lambda i, j: (0, 0)),
                  pl.BlockSpec((None, k, tn), lambda i, j: (layer, 0, j))],
        out_specs=pl.BlockSpec((tm, tn), lambda i, j: (i, j)),
        out_shape=jax.ShapeDtypeStruct((m, n), F32),
        scratch_shapes=[pltpu.VMEM((tm, k), BF16)],
        compiler_params=_params(("parallel", "arbitrary")),
        name="in_proj",
    )(h2d, g, w_all)


def _compress_kernel(x_ref, pe_ref, w_ref, o_ref, *, nc):
    first = jnp.zeros((nc, LANES), F32)
    second = jnp.zeros((nc, LANES), F32)
    for l in range(CMP_STRIDE):
        x = x_ref[pl.ds(l, nc, stride=CMP_STRIDE), :]
        first = first + _dot((x + pe_ref[l]).astype(BF16), w_ref[l])
        second = second + _dot((x + pe_ref[CMP_STRIDE + l]).astype(BF16), w_ref[CMP_STRIDE + l])
    o_ref[...] = first + pltpu.roll(second, nc - 1, 0)


def _compress(proj, pe, w, bsz, seq):
    nc = seq // CMP_STRIDE
    return pl.pallas_call(
        functools.partial(_compress_kernel, nc=nc),
        grid=(bsz, NSA_KV_HEADS),
        in_specs=[pl.BlockSpec((seq, LANES), lambda i, j: (i, COL_KV // LANES + j)),
                  pl.BlockSpec((None, CMP_LEN, 1, LANES), lambda i, j: (j // 2, 0, 0, 0)),
                  pl.BlockSpec((None, CMP_LEN, LANES, LANES), lambda i, j: (j // 2, 0, 0, 0))],
        out_specs=pl.BlockSpec((None, nc, LANES), lambda i, j: (i, 0, j)),
        out_shape=jax.ShapeDtypeStruct((bsz, nc, 2 * NSA_KV_WIDTH), F32),
        compiler_params=_params(("parallel", "parallel")),
        name="nsa_compress",
    )(proj, pe, w)


def _cmp_attn_kernel(q_ref, kv_ref, gz_ref, ocmp_ref, selt_ref, cnt_ref, kz_ref, vct_ref, bias_ref, s_ref,
                     *, tq, nc, nslc, nsel, nb):
    t0 = pl.program_id(1) * tq

    @pl.when(pl.program_id(1) == 0)
    def _():
        for gp in range(NSA_KV_HEADS // 2):
            padded = _pad_halves(kv_ref[:, gp * LANES:(gp + 1) * LANES])
            for idx in range(4):
                kz_ref[gp * 4 + idx] = padded[idx].astype(BF16)
            vct_ref[gp] = kv_ref[:, NSA_KV_WIDTH + gp * LANES:NSA_KV_WIDTH + (gp + 1) * LANES].T.astype(BF16)
        end_n = (lax.broadcasted_iota(jnp.int32, (nc, tq), 0) * CMP_STRIDE + (CMP_LEN - 1)).astype(F32)
        for hh in range(NSA_HEADS):
            bias_ref[hh] = (SLOPES[hh] * LOG2E) * end_n

    qb = (q_ref[...] * (DK ** -0.5 * LOG2E)).astype(BF16)
    for hh in range(NSA_HEADS):
        pair = hh // 2
        s_ref[hh] = _dot_nt(kz_ref[(hh // HPG) * 2 + hh % 2], qb[:, pair * LANES:(pair + 1) * LANES])

    n_row = lax.broadcasted_iota(jnp.int32, (nc, tq), 0)
    t_col = lax.broadcasted_iota(jnp.int32, (nc, tq), 1) + t0
    valid = t_col >= n_row * CMP_STRIDE + (CMP_LEN - 1)
    any_valid = jnp.where(t_col[0:1, :] >= CMP_LEN - 1, 1.0, 0.0)
    jj = lax.broadcasted_iota(jnp.int32, (nslc, nc), 0) * SLC_BLOCK
    nn = lax.broadcasted_iota(jnp.int32, (nslc, nc), 1) * CMP_STRIDE
    overlap_t = jnp.where((nn < jj + SLC_BLOCK) & (jj < nn + CMP_LEN), 1.0, 0.0)
    blk = lax.broadcasted_iota(jnp.int32, (nslc, tq), 0)
    t_blk = (lax.broadcasted_iota(jnp.int32, (nslc, tq), 1) + t0) >> SLC_SHIFT
    forced = (blk == 0) | (blk == t_blk) | (blk == t_blk - 1)
    future = blk > t_blk
    sub = lax.broadcasted_iota(jnp.int32, (8, tq), 0)
    ngrp = nslc // 8
    gate_t = _sigmoid(gz_ref[...].T)
    ones = jnp.ones((8, tq), BF16)

    outs = []
    for g in range(NSA_KV_HEADS):
        vct = vct_ref[g // 2, (g % 2) * DK:(g % 2 + 1) * DK, :]
        psum = jnp.zeros((nc, tq), F32)
        for h in range(HPG):
            hh = g * HPG + h
            s = jnp.where(valid, s_ref[hh] + bias_ref[hh], NEG_INF)
            e = jnp.exp2(s - jnp.max(s, axis=0, keepdims=True))
            p = e * (any_valid / jnp.sum(e, axis=0, keepdims=True))
            outs.append(_dot(vct, p.astype(BF16)) * gate_t[3 * hh:3 * hh + 1, :])
            psum = psum + p
        imp_t = _dot(overlap_t, psum, precision=lax.Precision.HIGHEST)
        score = jnp.where(future, -1.0, jnp.where(forced, 1e4, imp_t))

        sc = [score[8 * r:8 * r + 8] for r in range(ngrp)]
        rank = [jnp.zeros((8, tq), F32) for _ in range(ngrp)]
        for i in range(nslc):
            si = jnp.broadcast_to(score[i:i + 1, :], (8, tq))
            for r in range(ngrp):
                if r < i // 8:
                    before = si > sc[r]
                elif r > i // 8:
                    before = si >= sc[r]
                else:
                    before = (si > sc[r]) | ((sub > i % 8) & (si >= sc[r]))
                rank[r] = rank[r] + jnp.where(before, 1.0, 0.0)
        for r in range(ngrp):
            selt_ref[g, 8 * r:8 * r + 8, :] = jnp.where(rank[r] < float(nsel), 1.0, 0.0)
        if g % 2:
            picks = None
            for gi in (g - 1, g):
                for c in range(nb):
                    part = selt_ref[gi, pl.ds(c, nslc // nb, stride=nb), :]
                    picks = part if picks is None else picks + part
            cnt = _dot_nt(ones, picks.astype(BF16))
            cnt_ref[g // 2:g // 2 + 1, :] = cnt[0:1, :].astype(jnp.int32)

    for pair, tile in enumerate(_untranspose_pairs(jnp.concatenate(outs, axis=1), tq)):
        ocmp_ref[:, pair * LANES:(pair + 1) * LANES] = tile


def _cmp_attn(proj, cmp_kv, bsz, seq):
    tq = 128
    nc = cmp_kv.shape[1]
    nslc = seq // SLC_BLOCK
    nsel = min(N_SELECT, nslc)
    nq = seq // tq
    return pl.pallas_call(
        functools.partial(_cmp_attn_kernel, tq=tq, nc=nc, nslc=nslc, nsel=nsel, nb=KEY_TILE // SLC_BLOCK),
        grid=(bsz, nq),
        in_specs=[pl.BlockSpec((tq, NSA_WIDTH), lambda b, i: (b * nq + i, 0)),
                  pl.BlockSpec((None, nc, 2 * NSA_KV_WIDTH), lambda b, i: (b, 0, 0)),
                  pl.BlockSpec((tq, LANES), lambda b, i: (b * nq + i, COL_GZ // LANES))],
        out_specs=[pl.BlockSpec((tq, NSA_WIDTH), lambda b, i: (b * nq + i, 0)),
                   pl.BlockSpec((NSA_KV_HEADS, None, nslc, tq), lambda b, i: (0, b, 0, i)),
                   pl.BlockSpec((None, None, NSA_KV_HEADS // 2, seq // KEY_TILE), lambda b, i: (b, i, 0, 0))],
        out_shape=[jax.ShapeDtypeStruct((bsz * seq, NSA_WIDTH), F32),
                   jax.ShapeDtypeStruct((NSA_KV_HEADS, bsz, nslc, seq), F32),
                   jax.ShapeDtypeStruct((bsz, nq, NSA_KV_HEADS // 2, seq // KEY_TILE), jnp.int32)],
        scratch_shapes=[pltpu.VMEM((2 * NSA_KV_HEADS, nc, LANES), BF16),
                        pltpu.VMEM((NSA_KV_HEADS // 2, LANES, nc), BF16),
                        pltpu.VMEM((NSA_HEADS, nc, tq), F32),
                        pltpu.VMEM((NSA_HEADS, nc, tq), F32)],
        compiler_params=_params(("parallel", "arbitrary")),
        name="nsa_cmp_attn",
    )(proj, cmp_kv, proj)


def _flash_kernel(*refs, mode, tq, ts, nsub, nk):
    if mode == "sel":
        slopes_ref, cnt_ref, q_ref, k_ref, v_ref, gz_ref, selt_ref, o_ref = refs[:8]
    else:
        slopes_ref, q_ref, k_ref, v_ref, gz_ref, o_ref = refs[:6]
    kz_ref, vt_ref, qb_ref, bias_ref, gate_ref, s_ref, m_ref, acc_ref, idx_ref = refs[-9:]
    branch = 1 if mode == "sel" else 2
    gp = pl.program_id(1)
    qi = pl.program_id(2)
    t0 = qi * tq
    nb = ts // SLC_BLOCK
    ts_shift = ts.bit_length() - 1

    @pl.when(qi == 0)
    def _():
        for c in range(nk):
            padded = _pad_halves(k_ref[c * ts:(c + 1) * ts, :])
            for idx in range(4):
                kz_ref[idx, c] = padded[idx].astype(BF16)
            v_t = v_ref[c * ts:(c + 1) * ts, :].T
            ones_row = jnp.where(lax.broadcasted_iota(jnp.int32, (VT_ROWS - DK, ts), 0) == 0, 1.0, 0.0)
            for gi in range(2):
                vt_ref[gi, c, 0:DK, :] = v_t[gi * DK:(gi + 1) * DK, :].astype(BF16)
                vt_ref[gi, c, DK:VT_ROWS, :] = ones_row.astype(BF16)

    last_sub = (t0 + tq - 1) >> ts_shift
    if mode == "sel":
        def build(j, n):
            idx_ref[n] = j
            return n + jnp.where(((cnt_ref[gp, j] > 0) | (j == last_sub)) & (j <= last_sub), 1, 0)

        n_live = lax.fori_loop(0, nk, build, 0, unroll=4)
    else:
        first_sub = jnp.maximum(t0 - (WINDOW - 1), 0) >> ts_shift
        n_live = last_sub + 1 - first_sub
        for u in range(nsub):
            idx_ref[u] = jnp.minimum(first_sub + u, last_sub)

    key_i = lax.broadcasted_iota(jnp.int32, (ts, tq), 0)
    qry_t = lax.broadcasted_iota(jnp.int32, (ts, tq), 1) + t0
    key_f = key_i.astype(F32)
    qb_ref[...] = (q_ref[...] * (DK ** -0.5 * LOG2E)).astype(BF16)
    gate_ref[...] = _sigmoid(gz_ref[...].T)
    for gh in range(2 * HPG):
        bias_ref[gh] = (slopes_ref[gp * 2 * HPG + gh] * LOG2E) * key_f
    m_ref[...] = jnp.full(m_ref.shape, NEG_INF, F32)
    acc_ref[...] = jnp.zeros(acc_ref.shape, F32)

    def sub_tile(step, u):
        pos = step * nsub + u
        return idx_ref[jnp.minimum(pos, n_live - 1)], pos < n_live

    def scores(step, dst_ref):
        for u in range(nsub):
            kj, _ = sub_tile(step, u)
            for gi in range(2):
                for h in range(HPG):
                    pair = gi * (HPG // 2) + h // 2
                    dst_ref[u * 2 + gi, :, h * tq:(h + 1) * tq] = _dot_nt(
                        kz_ref[gi * 2 + h % 2, kj], qb_ref[:, pair * LANES:(pair + 1) * LANES])

    def softmax_update(step, src_ref):
        pvs = [[], []]
        for u in range(nsub):
            kj, live = sub_tile(step, u)
            j0 = kj * ts
            rel0 = (jnp.zeros((1, tq), jnp.int32) + (j0 - t0)).astype(F32)
            dd = qry_t - (key_i + j0)
            lowest = jnp.where(live, 0, 1 << 30)
            for gi in range(2):
                if mode == "sel":
                    chosen = selt_ref[gi, pl.ds(kj * nb + nb - 1, 1), :]
                    for c in range(nb - 2, -1, -1):
                        chosen = jnp.where(key_i < (c + 1) * SLC_BLOCK,
                                           selt_ref[gi, pl.ds(kj * nb + c, 1), :], chosen)
                    mask = (chosen > 0.5) & (dd >= lowest)
                else:
                    mask = (dd >= lowest) & (dd < WINDOW)
                ps = []
                alphas = []
                for h in range(HPG):
                    shift = (slopes_ref[(gp * 2 + gi) * HPG + h] * LOG2E) * rel0
                    cols = slice(h * tq, (h + 1) * tq)
                    x = jnp.where(mask, src_ref[u * 2 + gi, :, cols] + bias_ref[gi * HPG + h], MASKED)
                    m_old = m_ref[gi, :, cols]
                    m_new = jnp.maximum(m_old, jnp.max(x, axis=0, keepdims=True) + shift)
                    p = jnp.exp2(x - (m_new - shift))
                    m_ref[gi, :, cols] = m_new
                    ps.append(p.astype(BF16))
                    alphas.append(jnp.exp2(m_old - m_new))
                pv = _dot(vt_ref[gi, kj], jnp.concatenate(ps, axis=1))
                pvs[gi].append((jnp.concatenate(alphas, axis=1), pv))
        for gi in range(2):
            acc = acc_ref[gi]
            for alpha, pv in pvs[gi]:
                acc = acc * alpha + pv
            acc_ref[gi] = acc

    def body(step, carry):
        scores(step, s_ref)
        softmax_update(step, s_ref)
        return carry

    if mode == "sel":
        assert nsub & (nsub - 1) == 0
        lax.fori_loop(0, (n_live + nsub - 1) >> (nsub.bit_length() - 1), body, 0)
    else:
        body(0, 0)

    outs = []
    for gi in range(2):
        o_t = acc_ref[gi, 0:DK, :] / acc_ref[gi, DK:DK + 1, :]
        for h in range(HPG):
            row = (gp * 2 * HPG + gi * HPG + h) * 3 + branch
            outs.append(o_t[:, h * tq:(h + 1) * tq] * gate_ref[pl.ds(row, 1), :])
    for pair, tile in enumerate(_untranspose_pairs(jnp.concatenate(outs, axis=1), tq)):
        o_ref[:, pair * LANES:(pair + 1) * LANES] = tile


def _flash(proj, selt, cnt, slopes, bsz, seq, mode):
    tq = ts = KEY_TILE
    nsub = 4 if mode == "sel" else WINDOW // ts + 1
    nq = seq // tq
    nk = seq // ts
    nslc = seq // SLC_BLOCK
    kcol = (COL_KV + (2 if mode == "sel" else 4) * NSA_KV_WIDTH) // LANES
    vcol = kcol + NSA_KV_WIDTH // LANES
    smem = pltpu.SMEM
    in_specs = [pl.BlockSpec(memory_space=smem)]
    args = [slopes]
    if mode == "sel":
        in_specs.append(pl.BlockSpec((None, None, 2, nk), lambda b, g, i: (b, i, 0, 0), memory_space=smem))
        args.append(cnt)
    in_specs += [pl.BlockSpec((tq, 2 * HPG * DK), lambda b, g, i: (b * nq + i, g)),
                 pl.BlockSpec((seq, LANES), lambda b, g, i: (b, kcol + g)),
                 pl.BlockSpec((seq, LANES), lambda b, g, i: (b, vcol + g)),
                 pl.BlockSpec((tq, LANES), lambda b, g, i: (b * nq + i, COL_GZ // LANES))]
    args += [proj, proj, proj, proj]
    if mode == "sel":
        in_specs.append(pl.BlockSpec((2, None, nslc, tq), lambda b, g, i: (g, b, 0, i)))
        args.append(selt)
    return pl.pallas_call(
        functools.partial(_flash_kernel, mode=mode, tq=tq, ts=ts, nsub=nsub, nk=nk),
        grid=(bsz, 2, nq),
        in_specs=in_specs,
        out_specs=pl.BlockSpec((tq, 2 * HPG * DK), lambda b, g, i: (b * nq + i, g)),
        out_shape=jax.ShapeDtypeStruct((bsz * seq, NSA_WIDTH), F32),
        scratch_shapes=[pltpu.VMEM((4, nk, ts, LANES), BF16),
                        pltpu.VMEM((2, nk, VT_ROWS, ts), BF16),
                        pltpu.VMEM((tq, 2 * HPG * DK), BF16),
                        pltpu.VMEM((2 * HPG, ts, tq), F32),
                        pltpu.VMEM((LANES, tq), F32),
                        pltpu.VMEM((2 * nsub, ts, HPG * tq), F32),
                        pltpu.VMEM((2, 1, HPG * tq), F32),
                        pltpu.VMEM((2, VT_ROWS, HPG * tq), F32),
                        pltpu.SMEM((nk + nsub,), jnp.int32)],
        compiler_params=_params(("parallel", "parallel", "arbitrary")),
        name="nsa_" + mode,
    )(*args)


def _lru_kernel(x_ref, y_ref, cw_ref, cb_ref, wa_ref, ba_ref, wi_ref, bi_ref, lam_ref, o_ref,
                tail_ref, h_ref, *, tt):
    @pl.when(pl.program_id(1) == 0)
    def _():
        tail_ref[...] = jnp.zeros_like(tail_ref)
        h_ref[...] = jnp.zeros_like(h_ref)

    x = x_ref[...]
    ext = jnp.concatenate([tail_ref[...], x], axis=0)
    xc = cb_ref[...] + x * cw_ref[CONV_WIDTH - 1:CONV_WIDTH, :]
    for k in range(CONV_WIDTH - 1):
        back = CONV_WIDTH - 1 - k
        xc = xc + ext[8 - back:8 - back + tt, :] * cw_ref[k:k + 1, :]
    tail_ref[...] = x[tt - 8:, :]

    xcb = xc.astype(BF16)
    r = _sigmoid(_dot(xcb, wa_ref[...]) + ba_ref[...])
    gate_i = _sigmoid(_dot(xcb, wi_ref[...]) + bi_ref[...])
    neg_lam = -lam_ref[...]
    softplus = jnp.maximum(neg_lam, 0.0) + jnp.log1p(jnp.exp(-jnp.abs(neg_lam)))
    log_a = -LRU_C * r * softplus
    a = jnp.exp(log_a)
    u = jnp.sqrt(jnp.tanh(-log_a) * (a * a + 1.0)) * (gate_i * xc)

    rows = lax.broadcasted_iota(jnp.int32, (tt, 1), 0)
    step = 1
    while step < tt:
        keep = rows >= step
        a_prev = jnp.where(keep, pltpu.roll(a, step, 0), 1.0)
        u_prev = jnp.where(keep, pltpu.roll(u, step, 0), 0.0)
        u = u + a * u_prev
        a = a * a_prev
        step *= 2
    hs = u + a * h_ref[...]
    h_ref[...] = hs[tt - 1:tt, :]
    o_ref[...] = hs * jax.nn.gelu(y_ref[...], approximate=True)


def _lru(proj, cw, cb, wa, ba, wi, bi, lam, bsz, seq):
    tt = min(512, seq)
    nt = seq // tt
    w = LRU_WIDTH
    xcol = COL_LRU_X // w
    ycol = COL_LRU_Y // w
    vec = pl.BlockSpec((1, w), lambda b, i: (0, 0))
    mat = pl.BlockSpec((w, w), lambda b, i: (0, 0))
    return pl.pallas_call(
        functools.partial(_lru_kernel, tt=tt),
        grid=(bsz, nt),
        in_specs=[pl.BlockSpec((tt, w), lambda b, i: (b * nt + i, xcol)),
                  pl.BlockSpec((tt, w), lambda b, i: (b * nt + i, ycol)),
                  pl.BlockSpec((CONV_WIDTH, w), lambda b, i: (0, 0)),
                  vec, mat, vec, mat, vec, vec],
        out_specs=pl.BlockSpec((tt, w), lambda b, i: (b * nt + i, 0)),
        out_shape=jax.ShapeDtypeStruct((bsz * seq, w), F32),
        scratch_shapes=[pltpu.VMEM((8, w), F32), pltpu.VMEM((1, w), F32)],
        compiler_params=_params(("parallel", "arbitrary")),
        name="rglru",
    )(proj, proj, cw, cb, wa, ba, wi, bi, lam)


def _gla_kernel(q_ref, k_ref, v_ref, r_ref, gz_ref, w2_ref, bg_ref, ng_ref, o_ref, st_ref, *, tt):
    c = GLA_CHUNK
    dh = GLA_HEAD_DIM

    @pl.when(pl.program_id(1) == 0)
    def _():
        st_ref[...] = jnp.zeros_like(st_ref)

    gate = _dot(gz_ref[...].astype(BF16), w2_ref[...]) + bg_ref[...]
    log_alpha = (jnp.minimum(gate, 0.0) - jnp.log1p(jnp.exp(-jnp.abs(gate)))) / GLA_GATE_TAU
    ri = lax.broadcasted_iota(jnp.int32, (tt, tt), 0)
    ci = lax.broadcasted_iota(jnp.int32, (tt, tt), 1)
    tri = jnp.where((ri >> CHUNK_SHIFT == ci >> CHUNK_SHIFT) & (ci <= ri), 1.0, 0.0)
    bcum_all = _dot(tri, log_alpha, precision=lax.Precision.HIGHEST)
    causal = (lax.broadcasted_iota(jnp.int32, (c, c), 1) <= lax.broadcasted_iota(jnp.int32, (c, c), 0))

    for n in range(tt // c):
        rs = slice(n * c, (n + 1) * c)
        bcum = bcum_all[rs]
        b_last = bcum[c - 1:c, :]
        e_pos = jnp.exp(bcum)
        q_t = (q_ref[rs, :] * dh ** -0.5) * e_pos
        kk = k_ref[rs, :]
        k_t = kk * jnp.exp(-bcum)
        k_end = kk * jnp.exp(b_last - bcum)
        decay = jnp.exp(b_last)
        vv = v_ref[rs, :]
        for hh in range(GLA_HEADS):
            cs = slice(hh * dh, (hh + 1) * dh)
            qh = q_t[:, cs].astype(BF16)
            vh = vv[:, cs]
            att = jnp.where(causal, _dot_nt(qh, k_t[:, cs].astype(BF16)), 0.0)
            st = st_ref[hh]
            o = _dot(att.astype(BF16), vh.astype(BF16)) + _dot_nt(qh, st.astype(BF16))
            st_ref[hh] = st * decay[:, cs] + _dot(vh.T.astype(BF16), k_end[:, cs].astype(BF16))
            o = _rms_scale(o) * ng_ref[:, cs]
            rr = r_ref[rs, cs]
            o_ref[rs, cs] = o * (rr * _sigmoid(rr))


def _gla(proj, w2pad, bg, ng, bsz, seq):
    tt = min(256, seq)
    nt = seq // tt
    w = GLA_WIDTH
    cols = [COL_GLA_Q // w, COL_GLA_K // w, COL_GLA_V // w, COL_GLA_R // w]
    seg = [pl.BlockSpec((tt, w), functools.partial(lambda b, i, cc: (b * nt + i, cc), cc=cc)) for cc in cols]
    vec = pl.BlockSpec((1, w), lambda b, i: (0, 0))
    return pl.pallas_call(
        functools.partial(_gla_kernel, tt=tt),
        grid=(bsz, nt),
        in_specs=seg + [pl.BlockSpec((tt, LANES), lambda b, i: (b * nt + i, COL_GZ // LANES)),
                        pl.BlockSpec((LANES, w), lambda b, i: (0, 0)), vec, vec],
        out_specs=pl.BlockSpec((tt, w), lambda b, i: (b * nt + i, 0)),
        out_shape=jax.ShapeDtypeStruct((bsz * seq, w), F32),
        scratch_shapes=[pltpu.VMEM((GLA_HEADS, GLA_HEAD_DIM, GLA_HEAD_DIM), F32)],
        compiler_params=_params(("parallel", "arbitrary")),
        name="gla",
    )(proj, proj, proj, proj, proj, w2pad, bg, ng)


def _out_proj_kernel(ocmp_ref, osel_ref, owin_ref, olru_ref, ogla_ref, h_ref, w_ref, g_ref, o_ref):
    nsa = ocmp_ref[...] + osel_ref[...] + owin_ref[...]
    y = _dot(nsa.astype(BF16), w_ref[0:NSA_WIDTH, :])
    y = y + _dot(olru_ref[...].astype(BF16), w_ref[NSA_WIDTH:NSA_WIDTH + LRU_WIDTH, :])
    y = y + _dot(ogla_ref[...].astype(BF16), w_ref[NSA_WIDTH + LRU_WIDTH:, :])
    o_ref[...] = h_ref[...] + _rms_scale(y) * g_ref[...]


def _out_proj(ocmp, osel, owin, olru, ogla, h2d, w, g):
    m = h2d.shape[0]
    tm = min(256, m)
    row = lambda width: pl.BlockSpec((tm, width), lambda i: (i, 0))
    return pl.pallas_call(
        _out_proj_kernel,
        grid=(m // tm,),
        in_specs=[row(NSA_WIDTH), row(NSA_WIDTH), row(NSA_WIDTH), row(LRU_WIDTH), row(GLA_WIDTH), row(D_MODEL),
                  pl.BlockSpec((D_MODEL, D_MODEL), lambda i: (0, 0)),
                  pl.BlockSpec((1, D_MODEL), lambda i: (0, 0))],
        out_specs=row(D_MODEL),
        out_shape=jax.ShapeDtypeStruct((m, D_MODEL), F32),
        compiler_params=_params(("parallel",)),
        name="out_proj",
    )(ocmp, osel, owin, olru, ogla, h2d, w, g)


def _mlp_kernel(h_ref, gpre_ref, wu_ref, wd_ref, gpost_ref, o_ref, un_ref):
    f = pl.program_id(1)

    @pl.when(f == 0)
    def _():
        un_ref[...] = (_rms_scale(h_ref[...]) * gpre_ref[...]).astype(BF16)
        o_ref[...] = jnp.zeros_like(o_ref)

    a = jnp.maximum(_dot(un_ref[...], wu_ref[...].astype(BF16)), 0.0)
    o_ref[...] += _dot((a * a).astype(BF16), wd_ref[...].astype(BF16))

    @pl.when(f == pl.num_programs(1) - 1)
    def _():
        o_ref[...] = h_ref[...] + _rms_scale(o_ref[...]) * gpost_ref[...]


def _mlp(h2d, gpre, w_up, w_down, gpost, layer):
    m = h2d.shape[0]
    tm = min(1024, m)
    tf = 512
    vec = pl.BlockSpec((1, D_MODEL), lambda i, f: (0, 0))
    return pl.pallas_call(
        _mlp_kernel,
        grid=(m // tm, D_FF // tf),
        in_specs=[pl.BlockSpec((tm, D_MODEL), lambda i, f: (i, 0)), vec,
                  pl.BlockSpec((None, D_MODEL, tf), lambda i, f: (layer, 0, f)),
                  pl.BlockSpec((None, tf, D_MODEL), lambda i, f: (layer, f, 0)), vec],
        out_specs=pl.BlockSpec((tm, D_MODEL), lambda i, f: (i, 0), pipeline_mode=pl.Buffered(1)),
        out_shape=jax.ShapeDtypeStruct((m, D_MODEL), F32),
        scratch_shapes=[pltpu.VMEM((tm, D_MODEL), BF16)],
        compiler_params=_params(("parallel", "arbitrary")),
        name="mlp",
    )(h2d, gpre, w_up, w_down, gpost)


def _ple_kernel(h_ref, p_ref, wg_ref, wp_ref, o_ref):
    h = h_ref[...]
    gate = _sigmoid(_dot(h.astype(BF16), wg_ref[...]))
    o_ref[...] = h + gate * _dot(p_ref[...].astype(BF16), wp_ref[...])


def _ple(h2d, p2d, wg, wp):
    m = h2d.shape[0]
    tm = min(512, m)
    return pl.pallas_call(
        _ple_kernel,
        grid=(m // tm,),
        in_specs=[pl.BlockSpec((tm, D_MODEL), lambda i: (i, 0)),
                  pl.BlockSpec((tm, PLE_DIM), lambda i: (i, 0)),
                  pl.BlockSpec((D_MODEL, D_MODEL), lambda i: (0, 0)),
                  pl.BlockSpec((PLE_DIM, D_MODEL), lambda i: (0, 0))],
        out_specs=pl.BlockSpec((tm, D_MODEL), lambda i: (i, 0)),
        out_shape=jax.ShapeDtypeStruct((m, D_MODEL), F32),
        compiler_params=_params(("parallel",)),
        name="ple",
    )(h2d, p2d, wg, wp)


HEAD_BLOCKS = (NSA_WIDTH + 6 * NSA_KV_WIDTH) // LANES
TAIL_BLOCKS = (2 * LRU_WIDTH + 4 * GLA_WIDTH) // LANES
LAST_IN_BLOCK = (HEAD_BLOCKS * LANES + N_GATES + TAIL_BLOCKS * LANES + GLA_GATE_RANK) // LANES


def _w_in_prep_kernel(a_ref, b_ref, g_ref, o_ref):
    j = pl.program_id(1)
    lane = lax.broadcasted_iota(jnp.int32, o_ref.shape, 1)

    @pl.when(j < HEAD_BLOCKS)
    def _():
        o_ref[...] = a_ref[...].astype(BF16)

    @pl.when((j >= HEAD_BLOCKS) & (j < HEAD_BLOCKS + TAIL_BLOCKS))
    def _():
        both = jnp.concatenate([a_ref[...], b_ref[...]], axis=1)
        o_ref[...] = pltpu.roll(both, 2 * LANES - N_GATES, 1)[:, :LANES].astype(BF16)

    @pl.when(j == HEAD_BLOCKS + TAIL_BLOCKS)
    def _():
        merged = jnp.where(lane < N_GATES, g_ref[...],
                           jnp.where(lane < N_GATES + GLA_GATE_RANK, a_ref[...], 0.0))
        o_ref[...] = merged.astype(BF16)

    @pl.when(j > HEAD_BLOCKS + TAIL_BLOCKS)
    def _():
        o_ref[...] = jnp.zeros(o_ref.shape, BF16)


def _w_in_prep(w_in):
    depth, k, _ = w_in.shape
    blk = lambda f: pl.BlockSpec((None, k, LANES), f)
    return pl.pallas_call(
        _w_in_prep_kernel,
        grid=(depth, D_IN_PAD // LANES),
        in_specs=[blk(lambda d, j: (d, 0, jnp.minimum(j, LAST_IN_BLOCK))),
                  blk(lambda d, j: (d, 0, jnp.minimum(j + 1, LAST_IN_BLOCK))),
                  blk(lambda d, j: (d, 0, HEAD_BLOCKS))],
        out_specs=blk(lambda d, j: (d, 0, j)),
        out_shape=jax.ShapeDtypeStruct((depth, k, D_IN_PAD), BF16),
        compiler_params=_params(("parallel", "arbitrary")),
        name="w_in_prep",
    )(w_in, w_in, w_in)


def _compress_weights(cmp_w, cmp_pe):
    w4 = cmp_w.reshape(2, CMP_LEN, DK, DK)
    eye = jnp.eye(2, dtype=cmp_w.dtype)
    wbd = jnp.einsum("clde,xy->clxdye", w4, eye).reshape(2, CMP_LEN, LANES, LANES).astype(BF16)
    pe = jnp.tile(cmp_pe, (1, 1, 2)).reshape(2, CMP_LEN, 1, LANES)
    return pe, wbd


def _block_diag(w):
    eye = jnp.eye(LRU_BLOCKS, dtype=w.dtype)
    return jnp.einsum("ncd,nm->ncmd", w, eye).reshape(LRU_WIDTH, LRU_WIDTH).astype(BF16)


def _layer(h2d, p2d, bsz, seq, layer, w_in_all, w_up_all, w_down_all, norm_mix_pre, nsa_cmp_w, nsa_cmp_pe,
           lru_conv_w, lru_conv_b, lru_wa, lru_ba, lru_wi, lru_bi, lru_lambda, gla_w_gate2, gla_b_gate,
           gla_norm, w_out, norm_mix_post, norm_mlp_pre, norm_mlp_post, w_ple_gate, w_ple):
    row = lambda v: v.reshape(1, -1)
    proj = _in_proj(h2d, row(norm_mix_pre), w_in_all, layer)

    cmp_kv = _compress(proj, *_compress_weights(nsa_cmp_w, nsa_cmp_pe), bsz, seq)
    o_cmp, selt, cnt = _cmp_attn(proj, cmp_kv, bsz, seq)
    slopes = jnp.asarray(SLOPES, F32)
    o_sel = _flash(proj, selt, cnt, slopes, bsz, seq, "sel")
    o_win = _flash(proj, None, None, slopes, bsz, seq, "win")

    o_lru = _lru(proj, lru_conv_w, row(lru_conv_b), _block_diag(lru_wa), row(lru_ba),
                 _block_diag(lru_wi), row(lru_bi), row(lru_lambda), bsz, seq)

    w2pad = jnp.zeros((LANES, GLA_WIDTH), F32).at[N_GATES:N_GATES + GLA_GATE_RANK].set(gla_w_gate2)
    o_gla = _gla(proj, w2pad.astype(BF16), row(gla_b_gate), row(gla_norm), bsz, seq)

    h2d = _out_proj(o_cmp, o_sel, o_win, o_lru, o_gla, h2d, w_out.astype(BF16), row(norm_mix_post))
    h2d = _mlp(h2d, row(norm_mlp_pre), w_up_all, w_down_all, row(norm_mlp_post), layer)
    return _ple(h2d, p2d, w_ple_gate.astype(BF16), w_ple.astype(BF16))


def kernel(x, p, norm_mix_pre, w_in, nsa_cmp_w, nsa_cmp_pe, lru_conv_w, lru_conv_b, lru_wa, lru_ba,
           lru_wi, lru_bi, lru_lambda, gla_w_gate2, gla_b_gate, gla_norm, w_out, norm_mix_post,
           norm_mlp_pre, w_up, w_down, norm_mlp_post, w_ple_gate, w_ple):
    bsz, seq, _ = x.shape
    h2d = x.reshape(bsz * seq, D_MODEL)
    weights = (norm_mix_pre, nsa_cmp_w, nsa_cmp_pe, lru_conv_w, lru_conv_b, lru_wa, lru_ba,
               lru_wi, lru_bi, lru_lambda, gla_w_gate2, gla_b_gate, gla_norm, w_out, norm_mix_post,
               norm_mlp_pre, norm_mlp_post, w_ple_gate, w_ple)
    w_in_all = _w_in_prep(w_in)
    for i in range(p.shape[0]):
        h2d = _layer(h2d, p[i].reshape(bsz * seq, PLE_DIM), bsz, seq, i, w_in_all, w_up, w_down,
                     *(w[i] for w in weights))
    return h2d.reshape(bsz, seq, D_MODEL)
```

```python
import functools

import jax
import jax.numpy as jnp
from jax import lax
from jax.experimental import pallas as pl
from jax.experimental.pallas import tpu as pltpu

F32 = jnp.float32
BF16 = jnp.bfloat16

D_MODEL = 2048
PLE_DIM = 256
NSA_HEADS = 16
NSA_KV_HEADS = 4
HPG = NSA_HEADS // NSA_KV_HEADS
NSA_WIDTH = 1024
DK = 64
NSA_KV_WIDTH = NSA_KV_HEADS * DK
CMP_LEN = 32
CMP_STRIDE = 16
SLC_BLOCK = 64
SLC_SHIFT = 6
N_SELECT = 16
WINDOW = 512
LRU_WIDTH = 512
LRU_BLOCKS = 8
LRU_BLOCK_DIM = 64
CONV_WIDTH = 4
LRU_C = 8.0
GLA_WIDTH = 512
GLA_HEADS = 4
GLA_HEAD_DIM = 128
GLA_GATE_RANK = 16
GLA_GATE_TAU = 16.0
GLA_CHUNK = 64
CHUNK_SHIFT = 6
D_FF = 4 * D_MODEL
EPS = 1e-6
NEG_INF = -1e30
MASKED = 2.0 * NEG_INF

COL_Q = 0
COL_KV = 1024
COL_LRU_X = 2560
COL_LRU_Y = 3072
COL_GLA_Q = 3584
COL_GLA_K = 4096
COL_GLA_V = 4608
COL_GLA_R = 5120
COL_GZ = 5632
D_IN_PAD = 6144
N_GATES = 3 * NSA_HEADS
LANES = 128
KEY_TILE = 128
VT_ROWS = DK + 16
LOG2E = 1.4426950408889634

SLOPES = tuple(2.0 ** (-8.0 * i / NSA_HEADS) for i in range(1, NSA_HEADS + 1))

VMEM_LIMIT = 56 * 1024 * 1024

NT_DIMS = (((1,), (1,)), ((), ()))
TN_DIMS = (((0,), (0,)), ((), ()))


def _params(sem):
    return pltpu.CompilerParams(dimension_semantics=sem, vmem_limit_bytes=VMEM_LIMIT)


def _dot(a, b, **kw):
    return jnp.dot(a, b, preferred_element_type=F32, **kw)


def _dot_nt(a, b, **kw):
    return lax.dot_general(a, b, NT_DIMS, preferred_element_type=F32, **kw)


def _dot_tn(a, b, **kw):
    return lax.dot_general(a, b, TN_DIMS, preferred_element_type=F32, **kw)


def _sigmoid(x):
    return 1.0 / (1.0 + jnp.exp(-x))


def _rms_scale(x):
    return x * lax.rsqrt(jnp.mean(x * x, axis=-1, keepdims=True) + EPS)


def _half_masks(rows):
    lane = lax.broadcasted_iota(jnp.int32, (rows, LANES), 1)
    return lane < DK, lane >= DK


def _pad_halves(blk):
    lo, hi = _half_masks(blk.shape[0])
    swapped = pltpu.roll(blk, DK, 1)
    zero = jnp.zeros_like(blk)
    return (jnp.where(lo, blk, zero), jnp.where(hi, swapped, zero),
            jnp.where(lo, swapped, zero), jnp.where(hi, blk, zero))


def _untranspose_pairs(o_t, tq):
    sub = lax.broadcasted_iota(jnp.int32, (4 * DK, LANES), 0)
    lane = lax.broadcasted_iota(jnp.int32, (4 * DK, LANES), 1)
    place = jnp.where(lane == (sub & (DK - 1)) + jnp.where(sub >= 2 * DK, DK, 0), 1.0, 0.0).astype(BF16)
    out = []
    for pair in range(o_t.shape[1] // (2 * tq)):
        parts = []
        for half in range(2):
            v = o_t[:, (2 * pair + half) * tq:(2 * pair + half + 1) * tq]
            v_hi = v.astype(BF16)
            parts += [v_hi, (v - v_hi.astype(F32)).astype(BF16)]
        out.append(_dot_tn(jnp.concatenate(parts, axis=0), place))
    return out


IN_TILE = 512
HEAD_TILES = (NSA_WIDTH + 6 * NSA_KV_WIDTH) // IN_TILE
TAIL_TILES = (2 * LRU_WIDTH + 4 * GLA_WIDTH) // IN_TILE
D_IN = NSA_WIDTH + 6 * NSA_KV_WIDTH + N_GATES + 2 * LRU_WIDTH + 4 * GLA_WIDTH + GLA_GATE_RANK


def _in_proj_kernel(x_ref, g_ref, w_ref, wgz_ref, o_ref, xn_ref):
    j = pl.program_id(1)

    @pl.when(j == 0)
    def _():
        xn_ref[...] = (_rms_scale(x_ref[...]) * g_ref[...]).astype(BF16)

    @pl.when(j < HEAD_TILES + TAIL_TILES)
    def _():
        o_ref[...] = _dot_nt(xn_ref[...], w_ref[0].astype(BF16))

    @pl.when(j == HEAD_TILES + TAIL_TILES)
    def _():
        o_ref[...] = _dot_nt(xn_ref[...], wgz_ref[...].astype(BF16))


def _in_proj(h2d, g, w_t, w_gz, layer):
    m, k = h2d.shape
    tm = min(1024, m)
    n_tiles = HEAD_TILES + TAIL_TILES + 1
    assert n_tiles * IN_TILE == D_IN_PAD

    def w_rows(i, j):
        start = jnp.where(j < HEAD_TILES, j * IN_TILE, j * IN_TILE + N_GATES)
        return layer, pl.multiple_of(jnp.minimum(start, D_IN - IN_TILE), 8), 0

    return pl.pallas_call(
        _in_proj_kernel,
        grid=(m // tm, n_tiles),
        in_specs=[pl.BlockSpec((tm, k), lambda i, j: (i, 0)),
                  pl.BlockSpec((1, k), lambda i, j: (0, 0)),
                  pl.BlockSpec((pl.Element(1), pl.Element(IN_TILE), pl.Element(k)), w_rows),
                  pl.BlockSpec((None, IN_TILE, k), lambda i, j: (layer, 0, 0))],
        out_specs=pl.BlockSpec((tm, IN_TILE), lambda i, j: (i, j)),
        out_shape=jax.ShapeDtypeStruct((m, D_IN_PAD), F32),
        scratch_shapes=[pltpu.VMEM((tm, k), BF16)],
        compiler_params=_params(("parallel", "arbitrary")),
        name="in_proj",
    )(h2d, g, w_t, w_gz)


def _compress_kernel(x_ref, pe_ref, w_ref, o_ref, *, nc):
    first = jnp.zeros((nc, LANES), F32)
    second = jnp.zeros((nc, LANES), F32)
    for l in range(CMP_STRIDE):
        x = x_ref[pl.ds(l, nc, stride=CMP_STRIDE), :]
        first = first + _dot((x + pe_ref[l]).astype(BF16), w_ref[l])
        second = second + _dot((x + pe_ref[CMP_STRIDE + l]).astype(BF16), w_ref[CMP_STRIDE + l])
    o_ref[...] = first + pltpu.roll(second, nc - 1, 0)


def _compress(proj, pe, w, bsz, seq):
    nc = seq // CMP_STRIDE
    return pl.pallas_call(
        functools.partial(_compress_kernel, nc=nc),
        grid=(bsz, NSA_KV_HEADS),
        in_specs=[pl.BlockSpec((seq, LANES), lambda i, j: (i, COL_KV // LANES + j)),
                  pl.BlockSpec((None, CMP_LEN, 1, LANES), lambda i, j: (j // 2, 0, 0, 0)),
                  pl.BlockSpec((None, CMP_LEN, LANES, LANES), lambda i, j: (j // 2, 0, 0, 0))],
        out_specs=pl.BlockSpec((None, nc, LANES), lambda i, j: (i, 0, j)),
        out_shape=jax.ShapeDtypeStruct((bsz, nc, 2 * NSA_KV_WIDTH), F32),
        compiler_params=_params(("parallel", "parallel")),
        name="nsa_compress",
    )(proj, pe, w)


def _cmp_attn_kernel(q_ref, kv_ref, gz_ref, ocmp_ref, selt_ref, cnt_ref, kz_ref, vct_ref, bias_ref, s_ref,
                     *, tq, nc, nslc, nsel, nb):
    t0 = pl.program_id(1) * tq

    @pl.when(pl.program_id(1) == 0)
    def _():
        for gp in range(NSA_KV_HEADS // 2):
            padded = _pad_halves(kv_ref[:, gp * LANES:(gp + 1) * LANES])
            for idx in range(4):
                kz_ref[gp * 4 + idx] = padded[idx].astype(BF16)
            vct_ref[gp] = kv_ref[:, NSA_KV_WIDTH + gp * LANES:NSA_KV_WIDTH + (gp + 1) * LANES].T.astype(BF16)
        end_n = (lax.broadcasted_iota(jnp.int32, (nc, tq), 0) * CMP_STRIDE + (CMP_LEN - 1)).astype(F32)
        for hh in range(NSA_HEADS):
            bias_ref[hh] = (SLOPES[hh] * LOG2E) * end_n

    qb = (q_ref[...] * (DK ** -0.5 * LOG2E)).astype(BF16)
    for hh in range(NSA_HEADS):
        pair = hh // 2
        s_ref[hh] = _dot_nt(kz_ref[(hh // HPG) * 2 + hh % 2], qb[:, pair * LANES:(pair + 1) * LANES])

    n_row = lax.broadcasted_iota(jnp.int32, (nc, tq), 0)
    t_col = lax.broadcasted_iota(jnp.int32, (nc, tq), 1) + t0
    valid = t_col >= n_row * CMP_STRIDE + (CMP_LEN - 1)
    any_valid = jnp.where(t_col[0:1, :] >= CMP_LEN - 1, 1.0, 0.0)
    jj = lax.broadcasted_iota(jnp.int32, (nslc, nc), 0) * SLC_BLOCK
    nn = lax.broadcasted_iota(jnp.int32, (nslc, nc), 1) * CMP_STRIDE
    overlap_t = jnp.where((nn < jj + SLC_BLOCK) & (jj < nn + CMP_LEN), 1.0, 0.0)
    blk = lax.broadcasted_iota(jnp.int32, (nslc, tq), 0)
    t_blk = (lax.broadcasted_iota(jnp.int32, (nslc, tq), 1) + t0) >> SLC_SHIFT
    forced = (blk == 0) | (blk == t_blk) | (blk == t_blk - 1)
    future = blk > t_blk
    sub = lax.broadcasted_iota(jnp.int32, (8, tq), 0)
    ngrp = nslc // 8
    gate_t = _sigmoid(gz_ref[...].T)
    ones = jnp.ones((8, tq), BF16)

    outs = []
    for g in range(NSA_KV_HEADS):
        vct = vct_ref[g // 2, (g % 2) * DK:(g % 2 + 1) * DK, :]
        psum = jnp.zeros((nc, tq), F32)
        for h in range(HPG):
            hh = g * HPG + h
            s = jnp.where(valid, s_ref[hh] + bias_ref[hh], NEG_INF)
            e = jnp.exp2(s - jnp.max(s, axis=0, keepdims=True))
            p = e * (any_valid / jnp.sum(e, axis=0, keepdims=True))
            outs.append(_dot(vct, p.astype(BF16)) * gate_t[3 * hh:3 * hh + 1, :])
            psum = psum + p
        imp_t = _dot(overlap_t, psum, precision=lax.Precision.HIGHEST)
        score = jnp.where(future, -1.0, jnp.where(forced, 1e4, imp_t))

        sc = [score[8 * r:8 * r + 8] for r in range(ngrp)]
        rank = [jnp.zeros((8, tq), F32) for _ in range(ngrp)]
        for i in range(nslc):
            si = jnp.broadcast_to(score[i:i + 1, :], (8, tq))
            for r in range(ngrp):
                if r < i // 8:
                    before = si > sc[r]
                elif r > i // 8:
                    before = si >= sc[r]
                else:
                    before = (si > sc[r]) | ((sub > i % 8) & (si >= sc[r]))
                rank[r] = rank[r] + jnp.where(before, 1.0, 0.0)
        for r in range(ngrp):
            selt_ref[g, 8 * r:8 * r + 8, :] = jnp.where(rank[r] < float(nsel), 1.0, 0.0)
        if g % 2:
            picks = None
            for gi in (g - 1, g):
                for c in range(nb):
                    part = selt_ref[gi, pl.ds(c, nslc // nb, stride=nb), :]
                    picks = part if picks is None else picks + part
            cnt = _dot_nt(ones, picks.astype(BF16))
            cnt_ref[g // 2:g // 2 + 1, :] = cnt[0:1, :].astype(jnp.int32)

    for pair, tile in enumerate(_untranspose_pairs(jnp.concatenate(outs, axis=1), tq)):
        ocmp_ref[:, pair * LANES:(pair + 1) * LANES] = tile


def _cmp_attn(proj, cmp_kv, bsz, seq):
    tq = 128
    nc = cmp_kv.shape[1]
    nslc = seq // SLC_BLOCK
    nsel = min(N_SELECT, nslc)
    nq = seq // tq
    return pl.pallas_call(
        functools.partial(_cmp_attn_kernel, tq=tq, nc=nc, nslc=nslc, nsel=nsel, nb=KEY_TILE // SLC_BLOCK),
        grid=(bsz, nq),
        in_specs=[pl.BlockSpec((tq, NSA_WIDTH), lambda b, i: (b * nq + i, 0)),
                  pl.BlockSpec((None, nc, 2 * NSA_KV_WIDTH), lambda b, i: (b, 0, 0)),
                  pl.BlockSpec((tq, LANES), lambda b, i: (b * nq + i, COL_GZ // LANES))],
        out_specs=[pl.BlockSpec((tq, NSA_WIDTH), lambda b, i: (b * nq + i, 0)),
                   pl.BlockSpec((NSA_KV_HEADS, None, nslc, tq), lambda b, i: (0, b, 0, i)),
                   pl.BlockSpec((None, None, NSA_KV_HEADS // 2, seq // KEY_TILE), lambda b, i: (b, i, 0, 0))],
        out_shape=[jax.ShapeDtypeStruct((bsz * seq, NSA_WIDTH), F32),
                   jax.ShapeDtypeStruct((NSA_KV_HEADS, bsz, nslc, seq), F32),
                   jax.ShapeDtypeStruct((bsz, nq, NSA_KV_HEADS // 2, seq // KEY_TILE), jnp.int32)],
        scratch_shapes=[pltpu.VMEM((2 * NSA_KV_HEADS, nc, LANES), BF16),
                        pltpu.VMEM((NSA_KV_HEADS // 2, LANES, nc), BF16),
                        pltpu.VMEM((NSA_HEADS, nc, tq), F32),
                        pltpu.VMEM((NSA_HEADS, nc, tq), F32)],
        compiler_params=_params(("parallel", "arbitrary")),
        name="nsa_cmp_attn",
    )(proj, cmp_kv, proj)


def _stage_kv(k_ref, v_ref, kz_ref, vt_ref, nk, ts):
    for c in range(nk):
        padded = _pad_halves(k_ref[c * ts:(c + 1) * ts, :])
        for idx in range(4):
            kz_ref[idx, c] = padded[idx].astype(BF16)
        v_t = v_ref[c * ts:(c + 1) * ts, :].T
        ones_row = jnp.where(lax.broadcasted_iota(jnp.int32, (VT_ROWS - DK, ts), 0) == 0, 1.0, 0.0)
        for gi in range(2):
            vt_ref[gi, c, 0:DK, :] = v_t[gi * DK:(gi + 1) * DK, :].astype(BF16)
            vt_ref[gi, c, DK:VT_ROWS, :] = ones_row.astype(BF16)


def _attend(mode, nsub, gp, t0, slopes_ref, cnt_ref, selt_ref, qb_ref, bias_ref, kz_ref, vt_ref, s_ref,
            m_ref, acc_ref, idx_ref, *, tq, ts, nk):
    nb = ts // SLC_BLOCK
    ts_shift = ts.bit_length() - 1
    m_ref[...] = jnp.full(m_ref.shape, NEG_INF, F32)
    acc_ref[...] = jnp.zeros(acc_ref.shape, F32)

    last_sub = (t0 + tq - 1) >> ts_shift
    if mode == "sel":
        def build(j, n):
            idx_ref[n] = j
            return n + jnp.where(((cnt_ref[gp, j] > 0) | (j == last_sub)) & (j <= last_sub), 1, 0)

        n_live = lax.fori_loop(0, nk, build, 0, unroll=4)
    else:
        first_sub = jnp.maximum(t0 - (WINDOW - 1), 0) >> ts_shift
        n_live = last_sub + 1 - first_sub
        for u in range(nsub):
            idx_ref[u] = jnp.minimum(first_sub + u, last_sub)

    key_i = lax.broadcasted_iota(jnp.int32, (ts, tq), 0)
    qry_t = lax.broadcasted_iota(jnp.int32, (ts, tq), 1) + t0

    def sub_tile(step, u):
        pos = step * nsub + u
        return idx_ref[jnp.minimum(pos, n_live - 1)], pos < n_live

    def scores(step):
        for u in range(nsub):
            kj, _ = sub_tile(step, u)
            for gi in range(2):
                for h in range(HPG):
                    pair = gi * (HPG // 2) + h // 2
                    s_ref[u * 2 + gi, :, h * tq:(h + 1) * tq] = _dot_nt(
                        kz_ref[gi * 2 + h % 2, kj], qb_ref[:, pair * LANES:(pair + 1) * LANES])

    def softmax_update(step):
        pvs = [[], []]
        for u in range(nsub):
            kj, live = sub_tile(step, u)
            j0 = kj * ts
            rel0 = (jnp.zeros((1, tq), jnp.int32) + (j0 - t0)).astype(F32)
            dd = qry_t - (key_i + j0)
            lowest = jnp.where(live, 0, 1 << 30)
            for gi in range(2):
                if mode == "sel":
                    chosen = selt_ref[gi, pl.ds(kj * nb + nb - 1, 1), :]
                    for c in range(nb - 2, -1, -1):
                        chosen = jnp.where(key_i < (c + 1) * SLC_BLOCK,
                                           selt_ref[gi, pl.ds(kj * nb + c, 1), :], chosen)
                    mask = (chosen > 0.5) & (dd >= lowest)
                else:
                    mask = (dd >= lowest) & (dd < WINDOW)
                ps = []
                alphas = []
                for h in range(HPG):
                    shift = (slopes_ref[(gp * 2 + gi) * HPG + h] * LOG2E) * rel0
                    cols = slice(h * tq, (h + 1) * tq)
                    x = jnp.where(mask, s_ref[u * 2 + gi, :, cols] + bias_ref[gi * HPG + h], MASKED)
                    m_old = m_ref[gi, :, cols]
                    m_new = jnp.maximum(m_old, jnp.max(x, axis=0, keepdims=True) + shift)
                    p = jnp.exp2(x - (m_new - shift))
                    m_ref[gi, :, cols] = m_new
                    ps.append(p.astype(BF16))
                    alphas.append(jnp.exp2(m_old - m_new))
                pv = _dot(vt_ref[gi, kj], jnp.concatenate(ps, axis=1))
                pvs[gi].append((jnp.concatenate(alphas, axis=1), pv))
        for gi in range(2):
            acc = acc_ref[gi]
            for alpha, pv in pvs[gi]:
                acc = acc * alpha + pv
            acc_ref[gi] = acc

    def body(step, carry):
        scores(step)
        softmax_update(step)
        return carry

    if mode == "sel":
        assert nsub & (nsub - 1) == 0
        lax.fori_loop(0, (n_live + nsub - 1) >> (nsub.bit_length() - 1), body, 0)
    else:
        body(0, 0)


def _flash_kernel(slopes_ref, cnt_ref, q_ref, ks_ref, vs_ref, kw_ref, vw_ref, gz_ref, selt_ref, o_ref,
                  kzs_ref, vts_ref, kzw_ref, vtw_ref, qb_ref, bias_ref, gate_ref, s_ref, m_ref, acc_ref, idx_ref,
                  *, tq, ts, nk, nsub_sel, nsub_win):
    gp = pl.program_id(1)
    qi = pl.program_id(2)
    t0 = qi * tq

    @pl.when(qi == 0)
    def _():
        _stage_kv(ks_ref, vs_ref, kzs_ref, vts_ref, nk, ts)
        _stage_kv(kw_ref, vw_ref, kzw_ref, vtw_ref, nk, ts)

    key_f = lax.broadcasted_iota(jnp.int32, (ts, tq), 0).astype(F32)
    qb_ref[...] = (q_ref[...] * (DK ** -0.5 * LOG2E)).astype(BF16)
    gate_ref[...] = _sigmoid(gz_ref[...].T)
    for gh in range(2 * HPG):
        bias_ref[gh] = (slopes_ref[gp * 2 * HPG + gh] * LOG2E) * key_f

    common = dict(tq=tq, ts=ts, nk=nk)
    _attend("sel", nsub_sel, gp, t0, slopes_ref, cnt_ref, selt_ref, qb_ref, bias_ref, kzs_ref, vts_ref, s_ref,
            m_ref.at[0], acc_ref.at[0], idx_ref, **common)
    _attend("win", nsub_win, gp, t0, slopes_ref, None, None, qb_ref, bias_ref, kzw_ref, vtw_ref, s_ref,
            m_ref.at[1], acc_ref.at[1], idx_ref, **common)

    outs = []
    for gi in range(2):
        branch_out = [acc_ref[br, gi, 0:DK, :] / acc_ref[br, gi, DK:DK + 1, :] for br in range(2)]
        for h in range(HPG):
            row = (gp * 2 * HPG + gi * HPG + h) * 3
            cols = slice(h * tq, (h + 1) * tq)
            outs.append(branch_out[0][:, cols] * gate_ref[pl.ds(row + 1, 1), :]
                        + branch_out[1][:, cols] * gate_ref[pl.ds(row + 2, 1), :])
    for pair, tile in enumerate(_untranspose_pairs(jnp.concatenate(outs, axis=1), tq)):
        o_ref[:, pair * LANES:(pair + 1) * LANES] = tile


def _flash(proj, selt, cnt, slopes, bsz, seq):
    tq = ts = KEY_TILE
    nsub_sel = 4
    nsub_win = WINDOW // ts + 1
    nq = seq // tq
    nk = seq // ts
    nslc = seq // SLC_BLOCK
    first_kv = (COL_KV + 2 * NSA_KV_WIDTH) // LANES
    kv_spec = lambda which: pl.BlockSpec((seq, LANES), lambda b, g, i: (b, first_kv + 2 * which + g))
    smem = pltpu.SMEM
    max_units = 2 * max(nsub_sel, nsub_win)
    return pl.pallas_call(
        functools.partial(_flash_kernel, tq=tq, ts=ts, nk=nk, nsub_sel=nsub_sel, nsub_win=nsub_win),
        grid=(bsz, 2, nq),
        in_specs=[pl.BlockSpec(memory_space=smem),
                  pl.BlockSpec((None, None, 2, nk), lambda b, g, i: (b, i, 0, 0), memory_space=smem),
                  pl.BlockSpec((tq, 2 * HPG * DK), lambda b, g, i: (b * nq + i, g)),
                  kv_spec(0), kv_spec(1), kv_spec(2), kv_spec(3),
                  pl.BlockSpec((tq, LANES), lambda b, g, i: (b * nq + i, COL_GZ // LANES)),
                  pl.BlockSpec((2, None, nslc, tq), lambda b, g, i: (g, b, 0, i))],
        out_specs=pl.BlockSpec((tq, 2 * HPG * DK), lambda b, g, i: (b * nq + i, g)),
        out_shape=jax.ShapeDtypeStruct((bsz * seq, NSA_WIDTH), F32),
        scratch_shapes=[pltpu.VMEM((4, nk, ts, LANES), BF16),
                        pltpu.VMEM((2, nk, VT_ROWS, ts), BF16),
                        pltpu.VMEM((4, nk, ts, LANES), BF16),
                        pltpu.VMEM((2, nk, VT_ROWS, ts), BF16),
                        pltpu.VMEM((tq, 2 * HPG * DK), BF16),
                        pltpu.VMEM((2 * HPG, ts, tq), F32),
                        pltpu.VMEM((LANES, tq), F32),
                        pltpu.VMEM((max_units, ts, HPG * tq), F32),
                        pltpu.VMEM((2, 2, 1, HPG * tq), F32),
                        pltpu.VMEM((2, 2, VT_ROWS, HPG * tq), F32),
                        pltpu.SMEM((nk + max_units,), jnp.int32)],
        compiler_params=_params(("parallel", "parallel", "arbitrary")),
        name="nsa_sel_win",
    )(slopes, cnt, proj, proj, proj, proj, proj, proj, selt)


def _lru_kernel(x_ref, y_ref, cw_ref, cb_ref, wa_ref, ba_ref, wi_ref, bi_ref, lam_ref, o_ref,
                tail_ref, h_ref, *, tt):
    @pl.when(pl.program_id(1) == 0)
    def _():
        tail_ref[...] = jnp.zeros_like(tail_ref)
        h_ref[...] = jnp.zeros_like(h_ref)

    x = x_ref[...]
    ext = jnp.concatenate([tail_ref[...], x], axis=0)
    xc = cb_ref[...] + x * cw_ref[CONV_WIDTH - 1:CONV_WIDTH, :]
    for k in range(CONV_WIDTH - 1):
        back = CONV_WIDTH - 1 - k
        xc = xc + ext[8 - back:8 - back + tt, :] * cw_ref[k:k + 1, :]
    tail_ref[...] = x[tt - 8:, :]

    xcb = xc.astype(BF16)
    r = _sigmoid(_dot(xcb, wa_ref[...]) + ba_ref[...])
    gate_i = _sigmoid(_dot(xcb, wi_ref[...]) + bi_ref[...])
    neg_lam = -lam_ref[...]
    softplus = jnp.maximum(neg_lam, 0.0) + jnp.log1p(jnp.exp(-jnp.abs(neg_lam)))
    log_a = -LRU_C * r * softplus
    a = jnp.exp(log_a)
    u = jnp.sqrt(jnp.tanh(-log_a) * (a * a + 1.0)) * (gate_i * xc)

    rows = lax.broadcasted_iota(jnp.int32, (tt, 1), 0)
    step = 1
    while step < tt:
        keep = rows >= step
        a_prev = jnp.where(keep, pltpu.roll(a, step, 0), 1.0)
        u_prev = jnp.where(keep, pltpu.roll(u, step, 0), 0.0)
        u = u + a * u_prev
        a = a * a_prev
        step *= 2
    hs = u + a * h_ref[...]
    h_ref[...] = hs[tt - 1:tt, :]
    o_ref[...] = hs * jax.nn.gelu(y_ref[...], approximate=True)


def _lru(proj, cw, cb, wa, ba, wi, bi, lam, bsz, seq):
    tt = min(512, seq)
    nt = seq // tt
    w = LRU_WIDTH
    xcol = COL_LRU_X // w
    ycol = COL_LRU_Y // w
    vec = pl.BlockSpec((1, w), lambda b, i: (0, 0))
    mat = pl.BlockSpec((w, w), lambda b, i: (0, 0))
    return pl.pallas_call(
        functools.partial(_lru_kernel, tt=tt),
        grid=(bsz, nt),
        in_specs=[pl.BlockSpec((tt, w), lambda b, i: (b * nt + i, xcol)),
                  pl.BlockSpec((tt, w), lambda b, i: (b * nt + i, ycol)),
                  pl.BlockSpec((CONV_WIDTH, w), lambda b, i: (0, 0)),
                  vec, mat, vec, mat, vec, vec],
        out_specs=pl.BlockSpec((tt, w), lambda b, i: (b * nt + i, 0)),
        out_shape=jax.ShapeDtypeStruct((bsz * seq, w), F32),
        scratch_shapes=[pltpu.VMEM((8, w), F32), pltpu.VMEM((1, w), F32)],
        compiler_params=_params(("parallel", "arbitrary")),
        name="rglru",
    )(proj, proj, cw, cb, wa, ba, wi, bi, lam)


def _gla_kernel(q_ref, k_ref, v_ref, r_ref, gz_ref, w2_ref, bg_ref, ng_ref, o_ref, st_ref, *, tt):
    c = GLA_CHUNK
    dh = GLA_HEAD_DIM

    @pl.when(pl.program_id(1) == 0)
    def _():
        st_ref[...] = jnp.zeros_like(st_ref)

    gate = _dot(gz_ref[...].astype(BF16), w2_ref[...]) + bg_ref[...]
    log_alpha = (jnp.minimum(gate, 0.0) - jnp.log1p(jnp.exp(-jnp.abs(gate)))) / GLA_GATE_TAU
    ri = lax.broadcasted_iota(jnp.int32, (tt, tt), 0)
    ci = lax.broadcasted_iota(jnp.int32, (tt, tt), 1)
    tri = jnp.where((ri >> CHUNK_SHIFT == ci >> CHUNK_SHIFT) & (ci <= ri), 1.0, 0.0)
    bcum_all = _dot(tri, log_alpha, precision=lax.Precision.HIGHEST)
    causal = (lax.broadcasted_iota(jnp.int32, (c, c), 1) <= lax.broadcasted_iota(jnp.int32, (c, c), 0))

    for n in range(tt // c):
        rs = slice(n * c, (n + 1) * c)
        bcum = bcum_all[rs]
        b_last = bcum[c - 1:c, :]
        e_pos = jnp.exp(bcum)
        q_t = (q_ref[rs, :] * dh ** -0.5) * e_pos
        kk = k_ref[rs, :]
        k_t = kk * jnp.exp(-bcum)
        k_end = kk * jnp.exp(b_last - bcum)
        decay = jnp.exp(b_last)
        vv = v_ref[rs, :]
        for hh in range(GLA_HEADS):
            cs = slice(hh * dh, (hh + 1) * dh)
            qh = q_t[:, cs].astype(BF16)
            vh = vv[:, cs]
            att = jnp.where(causal, _dot_nt(qh, k_t[:, cs].astype(BF16)), 0.0)
            st = st_ref[hh]
            o = _dot(att.astype(BF16), vh.astype(BF16)) + _dot_nt(qh, st.astype(BF16))
            st_ref[hh] = st * decay[:, cs] + _dot(vh.T.astype(BF16), k_end[:, cs].astype(BF16))
            o = _rms_scale(o) * ng_ref[:, cs]
            rr = r_ref[rs, cs]
            o_ref[rs, cs] = o * (rr * _sigmoid(rr))


def _gla(proj, w2pad, bg, ng, bsz, seq):
    tt = min(256, seq)
    nt = seq // tt
    w = GLA_WIDTH
    cols = [COL_GLA_Q // w, COL_GLA_K // w, COL_GLA_V // w, COL_GLA_R // w]
    seg = [pl.BlockSpec((tt, w), functools.partial(lambda b, i, cc: (b * nt + i, cc), cc=cc)) for cc in cols]
    vec = pl.BlockSpec((1, w), lambda b, i: (0, 0))
    return pl.pallas_call(
        functools.partial(_gla_kernel, tt=tt),
        grid=(bsz, nt),
        in_specs=seg + [pl.BlockSpec((tt, LANES), lambda b, i: (b * nt + i, COL_GZ // LANES)),
                        pl.BlockSpec((LANES, w), lambda b, i: (0, 0)), vec, vec],
        out_specs=pl.BlockSpec((tt, w), lambda b, i: (b * nt + i, 0)),
        out_shape=jax.ShapeDtypeStruct((bsz * seq, w), F32),
        scratch_shapes=[pltpu.VMEM((GLA_HEADS, GLA_HEAD_DIM, GLA_HEAD_DIM), F32)],
        compiler_params=_params(("parallel", "arbitrary")),
        name="gla",
    )(proj, proj, proj, proj, proj, w2pad, bg, ng)


def _out_proj_kernel(ocmp_ref, oselwin_ref, olru_ref, ogla_ref, h_ref, w_ref, g_ref, o_ref, wb_ref):
    @pl.when(pl.program_id(0) == 0)
    def _():
        wb_ref[...] = w_ref[...].astype(BF16)

    nsa = ocmp_ref[...] + oselwin_ref[...]
    y = _dot(nsa.astype(BF16), wb_ref[0:NSA_WIDTH, :])
    y = y + _dot(olru_ref[...].astype(BF16), wb_ref[NSA_WIDTH:NSA_WIDTH + LRU_WIDTH, :])
    y = y + _dot(ogla_ref[...].astype(BF16), wb_ref[NSA_WIDTH + LRU_WIDTH:, :])
    o_ref[...] = h_ref[...] + _rms_scale(y) * g_ref[...]


def _resident(shape, layer):
    zeros = (0,) * len(shape)
    return pl.BlockSpec((None,) + shape, lambda i: (layer,) + zeros, pipeline_mode=pl.Buffered(1))


def _out_proj(ocmp, oselwin, olru, ogla, h2d, w_out, g, layer):
    m = h2d.shape[0]
    tm = min(256, m)
    row = lambda width: pl.BlockSpec((tm, width), lambda i: (i, 0))
    return pl.pallas_call(
        _out_proj_kernel,
        grid=(m // tm,),
        in_specs=[row(NSA_WIDTH), row(NSA_WIDTH), row(LRU_WIDTH), row(GLA_WIDTH), row(D_MODEL),
                  _resident((D_MODEL, D_MODEL), layer),
                  pl.BlockSpec((1, D_MODEL), lambda i: (0, 0))],
        out_specs=row(D_MODEL),
        out_shape=jax.ShapeDtypeStruct((m, D_MODEL), F32),
        scratch_shapes=[pltpu.VMEM((D_MODEL, D_MODEL), BF16)],
        compiler_params=_params(("arbitrary",)),
        name="out_proj",
    )(ocmp, oselwin, olru, ogla, h2d, w_out, g)


def _mlp_kernel(h_ref, gpre_ref, wu_ref, wd_ref, gpost_ref, o_ref, un_ref):
    f = pl.program_id(1)

    @pl.when(f == 0)
    def _():
        un_ref[...] = (_rms_scale(h_ref[...]) * gpre_ref[...]).astype(BF16)
        o_ref[...] = jnp.zeros_like(o_ref)

    a = jnp.maximum(_dot(un_ref[...], wu_ref[...].astype(BF16)), 0.0)
    o_ref[...] += _dot((a * a).astype(BF16), wd_ref[...].astype(BF16))

    @pl.when(f == pl.num_programs(1) - 1)
    def _():
        o_ref[...] = h_ref[...] + _rms_scale(o_ref[...]) * gpost_ref[...]


def _mlp(h2d, gpre, w_up, w_down, gpost, layer):
    m = h2d.shape[0]
    tm = min(1024, m)
    tf = 512
    vec = pl.BlockSpec((1, D_MODEL), lambda i, f: (0, 0))
    return pl.pallas_call(
        _mlp_kernel,
        grid=(m // tm, D_FF // tf),
        in_specs=[pl.BlockSpec((tm, D_MODEL), lambda i, f: (i, 0)), vec,
                  pl.BlockSpec((None, D_MODEL, tf), lambda i, f: (layer, 0, f)),
                  pl.BlockSpec((None, tf, D_MODEL), lambda i, f: (layer, f, 0)), vec],
        out_specs=pl.BlockSpec((tm, D_MODEL), lambda i, f: (i, 0), pipeline_mode=pl.Buffered(1)),
        out_shape=jax.ShapeDtypeStruct((m, D_MODEL), F32),
        scratch_shapes=[pltpu.VMEM((tm, D_MODEL), BF16)],
        compiler_params=_params(("parallel", "arbitrary")),
        name="mlp",
    )(h2d, gpre, w_up, w_down, gpost)


def _ple_kernel(h_ref, p_ref, wg_ref, wp_ref, o_ref, wgb_ref, wpb_ref):
    @pl.when(pl.program_id(0) == 0)
    def _():
        wgb_ref[...] = wg_ref[...].astype(BF16)
        wpb_ref[...] = wp_ref[...].astype(BF16)

    h = h_ref[...]
    gate = _sigmoid(_dot(h.astype(BF16), wgb_ref[...]))
    o_ref[...] = h + gate * _dot(p_ref[...].astype(BF16), wpb_ref[...])


def _ple(h2d, p_all, w_gate, w_ple, layer):
    m = h2d.shape[0]
    tm = min(512, m)
    return pl.pallas_call(
        _ple_kernel,
        grid=(m // tm,),
        in_specs=[pl.BlockSpec((tm, D_MODEL), lambda i: (i, 0)),
                  pl.BlockSpec((None, tm, PLE_DIM), lambda i: (layer, i, 0)),
                  _resident((D_MODEL, D_MODEL), layer),
                  _resident((PLE_DIM, D_MODEL), layer)],
        out_specs=pl.BlockSpec((tm, D_MODEL), lambda i: (i, 0)),
        out_shape=jax.ShapeDtypeStruct((m, D_MODEL), F32),
        scratch_shapes=[pltpu.VMEM((D_MODEL, D_MODEL), BF16), pltpu.VMEM((PLE_DIM, D_MODEL), BF16)],
        compiler_params=_params(("arbitrary",)),
        name="ple",
    )(h2d, p_all, w_gate, w_ple)


def _w_in_views(w_in):
    w_t = jnp.swapaxes(w_in, 1, 2)
    g0 = NSA_WIDTH + 6 * NSA_KV_WIDTH
    z0 = D_IN - GLA_GATE_RANK
    pad = jnp.zeros((w_t.shape[0], IN_TILE - N_GATES - GLA_GATE_RANK, w_t.shape[2]), w_t.dtype)
    w_gz = jnp.concatenate([w_t[:, g0:g0 + N_GATES], w_t[:, z0:], pad], axis=1)
    return w_t, w_gz


def _compress_weights(cmp_w, cmp_pe):
    w4 = cmp_w.reshape(2, CMP_LEN, DK, DK)
    eye = jnp.eye(2, dtype=cmp_w.dtype)
    wbd = jnp.einsum("clde,xy->clxdye", w4, eye).reshape(2, CMP_LEN, LANES, LANES).astype(BF16)
    pe = jnp.tile(cmp_pe, (1, 1, 2)).reshape(2, CMP_LEN, 1, LANES)
    return pe, wbd


def _block_diag(w):
    eye = jnp.eye(LRU_BLOCKS, dtype=w.dtype)
    return jnp.einsum("ncd,nm->ncmd", w, eye).reshape(LRU_WIDTH, LRU_WIDTH).astype(BF16)


def _layer(h2d, bsz, seq, layer, stacked, norm_mix_pre, nsa_cmp_w, nsa_cmp_pe, lru_conv_w, lru_conv_b,
           lru_wa, lru_ba, lru_wi, lru_bi, lru_lambda, gla_w_gate2, gla_b_gate, gla_norm, norm_mix_post,
           norm_mlp_pre, norm_mlp_post):
    w_in_all, w_out_all, w_up_all, w_down_all, w_ple_gate_all, w_ple_all, p_all = stacked
    row = lambda v: v.reshape(1, -1)
    proj = _in_proj(h2d, row(norm_mix_pre), *w_in_all, layer)

    cmp_kv = _compress(proj, *_compress_weights(nsa_cmp_w, nsa_cmp_pe), bsz, seq)
    o_cmp, selt, cnt = _cmp_attn(proj, cmp_kv, bsz, seq)
    slopes = jnp.asarray(SLOPES, F32)
    o_sel_win = _flash(proj, selt, cnt, slopes, bsz, seq)

    o_lru = _lru(proj, lru_conv_w, row(lru_conv_b), _block_diag(lru_wa), row(lru_ba),
                 _block_diag(lru_wi), row(lru_bi), row(lru_lambda), bsz, seq)

    w2pad = jnp.zeros((LANES, GLA_WIDTH), F32).at[N_GATES:N_GATES + GLA_GATE_RANK].set(gla_w_gate2)
    o_gla = _gla(proj, w2pad.astype(BF16), row(gla_b_gate), row(gla_norm), bsz, seq)

    h2d = _out_proj(o_cmp, o_sel_win, o_lru, o_gla, h2d, w_out_all, row(norm_mix_post), layer)
    h2d = _mlp(h2d, row(norm_mlp_pre), w_up_all, w_down_all, row(norm_mlp_post), layer)
    return _ple(h2d, p_all, w_ple_gate_all, w_ple_all, layer)


def kernel(x, p, norm_mix_pre, w_in, nsa_cmp_w, nsa_cmp_pe, lru_conv_w, lru_conv_b, lru_wa, lru_ba,
           lru_wi, lru_bi, lru_lambda, gla_w_gate2, gla_b_gate, gla_norm, w_out, norm_mix_post,
           norm_mlp_pre, w_up, w_down, norm_mlp_post, w_ple_gate, w_ple):
    bsz, seq, _ = x.shape
    h2d = x.reshape(bsz * seq, D_MODEL)
    small = (norm_mix_pre, nsa_cmp_w, nsa_cmp_pe, lru_conv_w, lru_conv_b, lru_wa, lru_ba, lru_wi, lru_bi,
             lru_lambda, gla_w_gate2, gla_b_gate, gla_norm, norm_mix_post, norm_mlp_pre, norm_mlp_post)
    depth = p.shape[0]
    stacked = (_w_in_views(w_in), w_out, w_up, w_down, w_ple_gate, w_ple,
               p.reshape(depth, bsz * seq, PLE_DIM))
    for i in range(depth):
        h2d = _layer(h2d, bsz, seq, i, stacked, *(w[i] for w in small))
    return h2d.reshape(bsz, seq, D_MODEL)
```

```python
import functools

import jax
import jax.numpy as jnp
from jax import lax
from jax.experimental import pallas as pl
from jax.experimental.pallas import tpu as pltpu

F32 = jnp.float32
BF16 = jnp.bfloat16

D_MODEL = 2048
PLE_DIM = 256
NSA_HEADS = 16
NSA_KV_HEADS = 4
HPG = NSA_HEADS // NSA_KV_HEADS
NSA_WIDTH = 1024
DK = 64
NSA_KV_WIDTH = NSA_KV_HEADS * DK
CMP_LEN = 32
CMP_STRIDE = 16
SLC_BLOCK = 64
SLC_SHIFT = 6
N_SELECT = 16
WINDOW = 512
LRU_WIDTH = 512
LRU_BLOCKS = 8
LRU_BLOCK_DIM = 64
CONV_WIDTH = 4
LRU_C = 8.0
GLA_WIDTH = 512
GLA_HEADS = 4
GLA_HEAD_DIM = 128
GLA_GATE_RANK = 16
GLA_GATE_TAU = 16.0
GLA_CHUNK = 64
CHUNK_SHIFT = 6
D_FF = 4 * D_MODEL
EPS = 1e-6
NEG_INF = -1e30
MASKED = 2.0 * NEG_INF

COL_Q = 0
COL_KV = 1024
COL_LRU_X = 2560
COL_LRU_Y = 3072
COL_GLA_Q = 3584
COL_GLA_K = 4096
COL_GLA_V = 4608
COL_GLA_R = 5120
COL_GZ = 5632
D_IN_PAD = 6144
N_GATES = 3 * NSA_HEADS
LANES = 128
KEY_TILE = 128
VT_ROWS = DK + 16
LOG2E = 1.4426950408889634

SLOPES = tuple(2.0 ** (-8.0 * i / NSA_HEADS) for i in range(1, NSA_HEADS + 1))

VMEM_LIMIT = 56 * 1024 * 1024

NT_DIMS = (((1,), (1,)), ((), ()))
TN_DIMS = (((0,), (0,)), ((), ()))


def _params(sem):
    return pltpu.CompilerParams(dimension_semantics=sem, vmem_limit_bytes=VMEM_LIMIT)


def _dot(a, b, **kw):
    return jnp.dot(a, b, preferred_element_type=F32, **kw)


def _dot_nt(a, b, **kw):
    return lax.dot_general(a, b, NT_DIMS, preferred_element_type=F32, **kw)


def _dot_tn(a, b, **kw):
    return lax.dot_general(a, b, TN_DIMS, preferred_element_type=F32, **kw)


def _sigmoid(x):
    return 1.0 / (1.0 + jnp.exp(-x))


def _rms_scale(x):
    return x * lax.rsqrt(jnp.mean(x * x, axis=-1, keepdims=True) + EPS)


def _half_masks(rows):
    lane = lax.broadcasted_iota(jnp.int32, (rows, LANES), 1)
    return lane < DK, lane >= DK


def _pad_halves(blk):
    lo, hi = _half_masks(blk.shape[0])
    swapped = pltpu.roll(blk, DK, 1)
    zero = jnp.zeros_like(blk)
    return (jnp.where(lo, blk, zero), jnp.where(hi, swapped, zero),
            jnp.where(lo, swapped, zero), jnp.where(hi, blk, zero))


def _untranspose_pairs(o_t, tq):
    sub = lax.broadcasted_iota(jnp.int32, (4 * DK, LANES), 0)
    lane = lax.broadcasted_iota(jnp.int32, (4 * DK, LANES), 1)
    place = jnp.where(lane == (sub & (DK - 1)) + jnp.where(sub >= 2 * DK, DK, 0), 1.0, 0.0).astype(BF16)
    out = []
    for pair in range(o_t.shape[1] // (2 * tq)):
        parts = []
        for half in range(2):
            v = o_t[:, (2 * pair + half) * tq:(2 * pair + half + 1) * tq]
            v_hi = v.astype(BF16)
            parts += [v_hi, (v - v_hi.astype(F32)).astype(BF16)]
        out.append(_dot_tn(jnp.concatenate(parts, axis=0), place))
    return out


IN_TILE = 512
HEAD_TILES = (NSA_WIDTH + 6 * NSA_KV_WIDTH) // IN_TILE
TAIL_TILES = (2 * LRU_WIDTH + 4 * GLA_WIDTH) // IN_TILE
D_IN = NSA_WIDTH + 6 * NSA_KV_WIDTH + N_GATES + 2 * LRU_WIDTH + 4 * GLA_WIDTH + GLA_GATE_RANK


def _in_proj_kernel(x_ref, g_ref, w_ref, wgz_ref, o_ref, xn_ref, wres_ref):
    i = pl.program_id(0)
    j = pl.program_id(1)
    last = HEAD_TILES + TAIL_TILES
    n_gz = N_GATES + GLA_GATE_RANK

    @pl.when(j == 0)
    def _():
        xn_ref[...] = (_rms_scale(x_ref[...]) * g_ref[...]).astype(BF16)

    @pl.when((i == 0) & (j < last))
    def _():
        wres_ref[j] = w_ref[0].astype(BF16)

    @pl.when((i == 0) & (j == last))
    def _():
        wres_ref[last, 0:n_gz, :] = wgz_ref[...].astype(BF16)
        wres_ref[last, n_gz:, :] = jnp.zeros((IN_TILE - n_gz, wres_ref.shape[2]), BF16)

    o_ref[...] = _dot_nt(xn_ref[...], wres_ref[j])


def _in_proj(h2d, g, w_t, w_gz, layer):
    m, k = h2d.shape
    tm = min(1024, m)
    n_tiles = HEAD_TILES + TAIL_TILES + 1
    assert n_tiles * IN_TILE == D_IN_PAD

    def w_rows(i, j):
        start = jnp.where(j < HEAD_TILES, j * IN_TILE, j * IN_TILE + N_GATES)
        start = jnp.where(i == 0, jnp.minimum(start, D_IN - IN_TILE), D_IN - IN_TILE)
        return layer, pl.multiple_of(start, 8), 0

    return pl.pallas_call(
        _in_proj_kernel,
        grid=(m // tm, n_tiles),
        in_specs=[pl.BlockSpec((tm, k), lambda i, j: (i, 0), pipeline_mode=pl.Buffered(1)),
                  pl.BlockSpec((1, k), lambda i, j: (0, 0)),
                  pl.BlockSpec((pl.Element(1), pl.Element(IN_TILE), pl.Element(k)), w_rows),
                  pl.BlockSpec((None, N_GATES + GLA_GATE_RANK, k), lambda i, j: (layer, 0, 0))],
        out_specs=pl.BlockSpec((tm, IN_TILE), lambda i, j: (i, j)),
        out_shape=jax.ShapeDtypeStruct((m, D_IN_PAD), F32),
        scratch_shapes=[pltpu.VMEM((tm, k), BF16), pltpu.VMEM((n_tiles, IN_TILE, k), BF16)],
        compiler_params=_params(("arbitrary", "arbitrary")),
        name="in_proj",
    )(h2d, g, w_t, w_gz)


def _compress_kernel(x_ref, pe_ref, w_ref, o_ref, *, nc):
    first = jnp.zeros((nc, LANES), F32)
    second = jnp.zeros((nc, LANES), F32)
    for l in range(CMP_STRIDE):
        x = x_ref[pl.ds(l, nc, stride=CMP_STRIDE), :]
        first = first + _dot((x + pe_ref[l]).astype(BF16), w_ref[l])
        second = second + _dot((x + pe_ref[CMP_STRIDE + l]).astype(BF16), w_ref[CMP_STRIDE + l])
    o_ref[...] = first + pltpu.roll(second, nc - 1, 0)


def _compress(proj, pe, w, bsz, seq):
    nc = seq // CMP_STRIDE
    return pl.pallas_call(
        functools.partial(_compress_kernel, nc=nc),
        grid=(bsz, NSA_KV_HEADS),
        in_specs=[pl.BlockSpec((seq, LANES), lambda i, j: (i, COL_KV // LANES + j)),
                  pl.BlockSpec((None, CMP_LEN, 1, LANES), lambda i, j: (j // 2, 0, 0, 0)),
                  pl.BlockSpec((None, CMP_LEN, LANES, LANES), lambda i, j: (j // 2, 0, 0, 0))],
        out_specs=pl.BlockSpec((None, nc, LANES), lambda i, j: (i, 0, j)),
        out_shape=jax.ShapeDtypeStruct((bsz, nc, 2 * NSA_KV_WIDTH), F32),
        compiler_params=_params(("parallel", "parallel")),
        name="nsa_compress",
    )(proj, pe, w)


def _cmp_attn_kernel(q_ref, kv_ref, gz_ref, ocmp_ref, selt_ref, cnt_ref, kz_ref, vct_ref, bias_ref, s_ref,
                     *, tq, nc, nslc, nsel, nb):
    t0 = pl.program_id(1) * tq

    @pl.when(pl.program_id(1) == 0)
    def _():
        for gp in range(NSA_KV_HEADS // 2):
            padded = _pad_halves(kv_ref[:, gp * LANES:(gp + 1) * LANES])
            for idx in range(4):
                kz_ref[gp * 4 + idx] = padded[idx].astype(BF16)
            vct_ref[gp] = kv_ref[:, NSA_KV_WIDTH + gp * LANES:NSA_KV_WIDTH + (gp + 1) * LANES].T.astype(BF16)
        end_n = (lax.broadcasted_iota(jnp.int32, (nc, tq), 0) * CMP_STRIDE + (CMP_LEN - 1)).astype(F32)
        for hh in range(NSA_HEADS):
            bias_ref[hh] = (SLOPES[hh] * LOG2E) * end_n

    qb = (q_ref[...] * (DK ** -0.5 * LOG2E)).astype(BF16)
    for hh in range(NSA_HEADS):
        pair = hh // 2
        s_ref[hh] = _dot_nt(kz_ref[(hh // HPG) * 2 + hh % 2], qb[:, pair * LANES:(pair + 1) * LANES])

    n_row = lax.broadcasted_iota(jnp.int32, (nc, tq), 0)
    t_col = lax.broadcasted_iota(jnp.int32, (nc, tq), 1) + t0
    valid = t_col >= n_row * CMP_STRIDE + (CMP_LEN - 1)
    any_valid = jnp.where(t_col[0:1, :] >= CMP_LEN - 1, 1.0, 0.0)
    jj = lax.broadcasted_iota(jnp.int32, (nslc, nc), 0) * SLC_BLOCK
    nn = lax.broadcasted_iota(jnp.int32, (nslc, nc), 1) * CMP_STRIDE
    overlap_t = jnp.where((nn < jj + SLC_BLOCK) & (jj < nn + CMP_LEN), 1.0, 0.0)
    blk = lax.broadcasted_iota(jnp.int32, (nslc, tq), 0)
    t_blk = (lax.broadcasted_iota(jnp.int32, (nslc, tq), 1) + t0) >> SLC_SHIFT
    forced = (blk == 0) | (blk == t_blk) | (blk == t_blk - 1)
    future = blk > t_blk
    sub = lax.broadcasted_iota(jnp.int32, (8, tq), 0)
    ngrp = nslc // 8
    gate_t = _sigmoid(gz_ref[...].T)
    ones = jnp.ones((8, tq), BF16)

    outs = []
    for g in range(NSA_KV_HEADS):
        vct = vct_ref[g // 2, (g % 2) * DK:(g % 2 + 1) * DK, :]
        psum = jnp.zeros((nc, tq), F32)
        for h in range(HPG):
            hh = g * HPG + h
            s = jnp.where(valid, s_ref[hh] + bias_ref[hh], NEG_INF)
            e = jnp.exp2(s - jnp.max(s, axis=0, keepdims=True))
            p = e * (any_valid / jnp.sum(e, axis=0, keepdims=True))
            outs.append(_dot(vct, p.astype(BF16)) * gate_t[3 * hh:3 * hh + 1, :])
            psum = psum + p
        imp_t = _dot(overlap_t, psum, precision=lax.Precision.HIGHEST)
        score = jnp.where(future, -1.0, jnp.where(forced, 1e4, imp_t))

        sc = [score[8 * r:8 * r + 8] for r in range(ngrp)]
        rank = [jnp.zeros((8, tq), F32) for _ in range(ngrp)]
        for i in range(nslc):
            si = jnp.broadcast_to(score[i:i + 1, :], (8, tq))
            for r in range(ngrp):
                if r < i // 8:
                    before = si > sc[r]
                elif r > i // 8:
                    before = si >= sc[r]
                else:
                    before = (si > sc[r]) | ((sub > i % 8) & (si >= sc[r]))
                rank[r] = rank[r] + jnp.where(before, 1.0, 0.0)
        for r in range(ngrp):
            selt_ref[g, 8 * r:8 * r + 8, :] = jnp.where(rank[r] < float(nsel), 1.0, 0.0)
        if g % 2:
            picks = None
            for gi in (g - 1, g):
                for c in range(nb):
                    part = selt_ref[gi, pl.ds(c, nslc // nb, stride=nb), :]
                    picks = part if picks is None else picks + part
            cnt = _dot_nt(ones, picks.astype(BF16))
            cnt_ref[g // 2:g // 2 + 1, :] = cnt[0:1, :].astype(jnp.int32)

    for pair, tile in enumerate(_untranspose_pairs(jnp.concatenate(outs, axis=1), tq)):
        ocmp_ref[:, pair * LANES:(pair + 1) * LANES] = tile


def _cmp_attn(proj, cmp_kv, bsz, seq):
    tq = 128
    nc = cmp_kv.shape[1]
    nslc = seq // SLC_BLOCK
    nsel = min(N_SELECT, nslc)
    nq = seq // tq
    return pl.pallas_call(
        functools.partial(_cmp_attn_kernel, tq=tq, nc=nc, nslc=nslc, nsel=nsel, nb=KEY_TILE // SLC_BLOCK),
        grid=(bsz, nq),
        in_specs=[pl.BlockSpec((tq, NSA_WIDTH), lambda b, i: (b * nq + i, 0)),
                  pl.BlockSpec((None, nc, 2 * NSA_KV_WIDTH), lambda b, i: (b, 0, 0)),
                  pl.BlockSpec((tq, LANES), lambda b, i: (b * nq + i, COL_GZ // LANES))],
        out_specs=[pl.BlockSpec((tq, NSA_WIDTH), lambda b, i: (b * nq + i, 0)),
                   pl.BlockSpec((NSA_KV_HEADS, None, nslc, tq), lambda b, i: (0, b, 0, i)),
                   pl.BlockSpec((None, None, NSA_KV_HEADS // 2, seq // KEY_TILE), lambda b, i: (b, i, 0, 0))],
        out_shape=[jax.ShapeDtypeStruct((bsz * seq, NSA_WIDTH), F32),
                   jax.ShapeDtypeStruct((NSA_KV_HEADS, bsz, nslc, seq), F32),
                   jax.ShapeDtypeStruct((bsz, nq, NSA_KV_HEADS // 2, seq // KEY_TILE), jnp.int32)],
        scratch_shapes=[pltpu.VMEM((2 * NSA_KV_HEADS, nc, LANES), BF16),
                        pltpu.VMEM((NSA_KV_HEADS // 2, LANES, nc), BF16),
                        pltpu.VMEM((NSA_HEADS, nc, tq), F32),
                        pltpu.VMEM((NSA_HEADS, nc, tq), F32)],
        compiler_params=_params(("parallel", "arbitrary")),
        name="nsa_cmp_attn",
    )(proj, cmp_kv, proj)


def _stage_kv(k_ref, v_ref, kz_ref, vt_ref, nk, ts):
    for c in range(nk):
        padded = _pad_halves(k_ref[c * ts:(c + 1) * ts, :])
        for idx in range(4):
            kz_ref[idx, c] = padded[idx].astype(BF16)
        v_t = v_ref[c * ts:(c + 1) * ts, :].T
        ones_row = jnp.where(lax.broadcasted_iota(jnp.int32, (VT_ROWS - DK, ts), 0) == 0, 1.0, 0.0)
        for gi in range(2):
            vt_ref[gi, c, 0:DK, :] = v_t[gi * DK:(gi + 1) * DK, :].astype(BF16)
            vt_ref[gi, c, DK:VT_ROWS, :] = ones_row.astype(BF16)


def _attend(mode, nsub, gp, t0, slopes_ref, cnt_ref, selt_ref, qb_ref, bias_ref, kz_ref, vt_ref, s_ref,
            m_ref, acc_ref, idx_ref, *, tq, ts, nk):
    nb = ts // SLC_BLOCK
    ts_shift = ts.bit_length() - 1
    m_ref[...] = jnp.full(m_ref.shape, NEG_INF, F32)
    acc_ref[...] = jnp.zeros(acc_ref.shape, F32)

    last_sub = (t0 + tq - 1) >> ts_shift
    if mode == "sel":
        def build(j, n):
            idx_ref[n] = j
            return n + jnp.where(((cnt_ref[gp, j] > 0) | (j == last_sub)) & (j <= last_sub), 1, 0)

        n_live = lax.fori_loop(0, nk, build, 0, unroll=4)
    else:
        first_sub = jnp.maximum(t0 - (WINDOW - 1), 0) >> ts_shift
        n_live = last_sub + 1 - first_sub
        for u in range(nsub):
            idx_ref[u] = jnp.minimum(first_sub + u, last_sub)

    key_i = lax.broadcasted_iota(jnp.int32, (ts, tq), 0)
    qry_t = lax.broadcasted_iota(jnp.int32, (ts, tq), 1) + t0

    def sub_tile(step, u):
        pos = step * nsub + u
        return idx_ref[jnp.minimum(pos, n_live - 1)], pos < n_live

    def scores(step):
        for u in range(nsub):
            kj, _ = sub_tile(step, u)
            for gi in range(2):
                for h in range(HPG):
                    pair = gi * (HPG // 2) + h // 2
                    s_ref[u * 2 + gi, :, h * tq:(h + 1) * tq] = _dot_nt(
                        kz_ref[gi * 2 + h % 2, kj], qb_ref[:, pair * LANES:(pair + 1) * LANES])

    def softmax_update(step):
        for u in range(nsub):
            kj, live = sub_tile(step, u)
            j0 = kj * ts
            rel0 = (jnp.zeros((1, tq), jnp.int32) + (j0 - t0)).astype(F32)
            dd = qry_t - (key_i + j0)
            lowest = jnp.where(live, 0, 1 << 30)
            for gi in range(2):
                if mode == "sel":
                    chosen = selt_ref[gi, pl.ds(kj * nb + nb - 1, 1), :]
                    for c in range(nb - 2, -1, -1):
                        chosen = jnp.where(key_i < (c + 1) * SLC_BLOCK,
                                           selt_ref[gi, pl.ds(kj * nb + c, 1), :], chosen)
                    mask = (chosen > 0.5) & (dd >= lowest)
                else:
                    mask = (dd >= lowest) & (dd < WINDOW)
                ps = []
                alphas = []
                for h in range(HPG):
                    shift = (slopes_ref[(gp * 2 + gi) * HPG + h] * LOG2E) * rel0
                    cols = slice(h * tq, (h + 1) * tq)
                    x = jnp.where(mask, s_ref[u * 2 + gi, :, cols] + bias_ref[gi * HPG + h], MASKED)
                    m_old = m_ref[gi, :, cols]
                    m_new = jnp.maximum(m_old, jnp.max(x, axis=0, keepdims=True) + shift)
                    p = jnp.exp2(x - (m_new - shift))
                    m_ref[gi, :, cols] = m_new
                    ps.append(p.astype(BF16))
                    alphas.append(jnp.exp2(m_old - m_new))
                pv = _dot(vt_ref[gi, kj], jnp.concatenate(ps, axis=1))
                acc_ref[gi] = acc_ref[gi] * jnp.concatenate(alphas, axis=1) + pv

    def body(step, carry):
        scores(step)
        softmax_update(step)
        return carry

    if mode == "sel":
        assert nsub & (nsub - 1) == 0
        lax.fori_loop(0, (n_live + nsub - 1) >> (nsub.bit_length() - 1), body, 0)
    else:
        body(0, 0)


def _flash_kernel(slopes_ref, cnt_ref, q_ref, ks_ref, vs_ref, kw_ref, vw_ref, gz_ref, selt_ref, o_ref,
                  kzs_ref, vts_ref, kzw_ref, vtw_ref, qb_ref, bias_ref, gate_ref, s_ref, m_ref, acc_ref, idx_ref,
                  *, tq, ts, nk, nsub_sel, nsub_win):
    gp = pl.program_id(1)
    qi = pl.program_id(2)
    t0 = qi * tq

    @pl.when(qi == 0)
    def _():
        _stage_kv(ks_ref, vs_ref, kzs_ref, vts_ref, nk, ts)
        _stage_kv(kw_ref, vw_ref, kzw_ref, vtw_ref, nk, ts)

    key_f = lax.broadcasted_iota(jnp.int32, (ts, tq), 0).astype(F32)
    qb_ref[...] = (q_ref[...] * (DK ** -0.5 * LOG2E)).astype(BF16)
    gate_ref[...] = _sigmoid(gz_ref[...].T)
    for gh in range(2 * HPG):
        bias_ref[gh] = (slopes_ref[gp * 2 * HPG + gh] * LOG2E) * key_f

    common = dict(tq=tq, ts=ts, nk=nk)
    _attend("sel", nsub_sel, gp, t0, slopes_ref, cnt_ref, selt_ref, qb_ref, bias_ref, kzs_ref, vts_ref, s_ref,
            m_ref.at[0], acc_ref.at[0], idx_ref, **common)
    _attend("win", nsub_win, gp, t0, slopes_ref, None, None, qb_ref, bias_ref, kzw_ref, vtw_ref, s_ref,
            m_ref.at[1], acc_ref.at[1], idx_ref, **common)

    outs = []
    for gi in range(2):
        branch_out = [acc_ref[br, gi, 0:DK, :] / acc_ref[br, gi, DK:DK + 1, :] for br in range(2)]
        for h in range(HPG):
            row = (gp * 2 * HPG + gi * HPG + h) * 3
            cols = slice(h * tq, (h + 1) * tq)
            outs.append(branch_out[0][:, cols] * gate_ref[pl.ds(row + 1, 1), :]
                        + branch_out[1][:, cols] * gate_ref[pl.ds(row + 2, 1), :])
    for pair, tile in enumerate(_untranspose_pairs(jnp.concatenate(outs, axis=1), tq)):
        o_ref[:, pair * LANES:(pair + 1) * LANES] = tile


def _flash(proj, selt, cnt, slopes, bsz, seq):
    tq = ts = KEY_TILE
    nsub_sel = 4
    nsub_win = WINDOW // ts + 1
    nq = seq // tq
    nk = seq // ts
    nslc = seq // SLC_BLOCK
    first_kv = (COL_KV + 2 * NSA_KV_WIDTH) // LANES
    kv_spec = lambda which: pl.BlockSpec((seq, LANES), lambda b, g, i: (b, first_kv + 2 * which + g))
    smem = pltpu.SMEM
    max_units = 2 * max(nsub_sel, nsub_win)
    return pl.pallas_call(
        functools.partial(_flash_kernel, tq=tq, ts=ts, nk=nk, nsub_sel=nsub_sel, nsub_win=nsub_win),
        grid=(bsz, 2, nq),
        in_specs=[pl.BlockSpec(memory_space=smem),
                  pl.BlockSpec((None, None, 2, nk), lambda b, g, i: (b, i, 0, 0), memory_space=smem),
                  pl.BlockSpec((tq, 2 * HPG * DK), lambda b, g, i: (b * nq + i, g)),
                  kv_spec(0), kv_spec(1), kv_spec(2), kv_spec(3),
                  pl.BlockSpec((tq, LANES), lambda b, g, i: (b * nq + i, COL_GZ // LANES)),
                  pl.BlockSpec((2, None, nslc, tq), lambda b, g, i: (g, b, 0, i))],
        out_specs=pl.BlockSpec((tq, 2 * HPG * DK), lambda b, g, i: (b * nq + i, g)),
        out_shape=jax.ShapeDtypeStruct((bsz * seq, NSA_WIDTH), F32),
        scratch_shapes=[pltpu.VMEM((4, nk, ts, LANES), BF16),
                        pltpu.VMEM((2, nk, VT_ROWS, ts), BF16),
                        pltpu.VMEM((4, nk, ts, LANES), BF16),
                        pltpu.VMEM((2, nk, VT_ROWS, ts), BF16),
                        pltpu.VMEM((tq, 2 * HPG * DK), BF16),
                        pltpu.VMEM((2 * HPG, ts, tq), F32),
                        pltpu.VMEM((LANES, tq), F32),
                        pltpu.VMEM((max_units, ts, HPG * tq), F32),
                        pltpu.VMEM((2, 2, 1, HPG * tq), F32),
                        pltpu.VMEM((2, 2, VT_ROWS, HPG * tq), F32),
                        pltpu.SMEM((nk + max_units,), jnp.int32)],
        compiler_params=_params(("parallel", "parallel", "arbitrary")),
        name="nsa_sel_win",
    )(slopes, cnt, proj, proj, proj, proj, proj, proj, selt)


def _lru_kernel(x_ref, y_ref, cw_ref, cb_ref, wa_ref, ba_ref, wi_ref, bi_ref, lam_ref, o_ref,
                tail_ref, h_ref, *, tt):
    @pl.when(pl.program_id(1) == 0)
    def _():
        tail_ref[...] = jnp.zeros_like(tail_ref)
        h_ref[...] = jnp.zeros_like(h_ref)

    x = x_ref[...]
    ext = jnp.concatenate([tail_ref[...], x], axis=0)
    xc = cb_ref[...] + x * cw_ref[CONV_WIDTH - 1:CONV_WIDTH, :]
    for k in range(CONV_WIDTH - 1):
        back = CONV_WIDTH - 1 - k
        xc = xc + ext[8 - back:8 - back + tt, :] * cw_ref[k:k + 1, :]
    tail_ref[...] = x[tt - 8:, :]

    xcb = xc.astype(BF16)
    r = _sigmoid(_dot(xcb, wa_ref[...]) + ba_ref[...])
    gate_i = _sigmoid(_dot(xcb, wi_ref[...]) + bi_ref[...])
    neg_lam = -lam_ref[...]
    softplus = jnp.maximum(neg_lam, 0.0) + jnp.log1p(jnp.exp(-jnp.abs(neg_lam)))
    log_a = -LRU_C * r * softplus
    a = jnp.exp(log_a)
    u = jnp.sqrt(jnp.tanh(-log_a) * (a * a + 1.0)) * (gate_i * xc)

    rows = lax.broadcasted_iota(jnp.int32, (tt, 1), 0)
    step = 1
    while step < tt:
        keep = rows >= step
        a_prev = jnp.where(keep, pltpu.roll(a, step, 0), 1.0)
        u_prev = jnp.where(keep, pltpu.roll(u, step, 0), 0.0)
        u = u + a * u_prev
        a = a * a_prev
        step *= 2
    hs = u + a * h_ref[...]
    h_ref[...] = hs[tt - 1:tt, :]
    o_ref[...] = hs * jax.nn.gelu(y_ref[...], approximate=True)


def _lru(proj, cw, cb, wa, ba, wi, bi, lam, bsz, seq):
    tt = min(512, seq)
    nt = seq // tt
    w = LRU_WIDTH
    xcol = COL_LRU_X // w
    ycol = COL_LRU_Y // w
    vec = pl.BlockSpec((1, w), lambda b, i: (0, 0))
    mat = pl.BlockSpec((w, w), lambda b, i: (0, 0))
    return pl.pallas_call(
        functools.partial(_lru_kernel, tt=tt),
        grid=(bsz, nt),
        in_specs=[pl.BlockSpec((tt, w), lambda b, i: (b * nt + i, xcol)),
                  pl.BlockSpec((tt, w), lambda b, i: (b * nt + i, ycol)),
                  pl.BlockSpec((CONV_WIDTH, w), lambda b, i: (0, 0)),
                  vec, mat, vec, mat, vec, vec],
        out_specs=pl.BlockSpec((tt, w), lambda b, i: (b * nt + i, 0)),
        out_shape=jax.ShapeDtypeStruct((bsz * seq, w), F32),
        scratch_shapes=[pltpu.VMEM((8, w), F32), pltpu.VMEM((1, w), F32)],
        compiler_params=_params(("parallel", "arbitrary")),
        name="rglru",
    )(proj, proj, cw, cb, wa, ba, wi, bi, lam)


def _gla_kernel(q_ref, k_ref, v_ref, r_ref, gz_ref, w2_ref, bg_ref, ng_ref, o_ref, st_ref, *, tt):
    c = GLA_CHUNK
    dh = GLA_HEAD_DIM

    @pl.when(pl.program_id(1) == 0)
    def _():
        st_ref[...] = jnp.zeros_like(st_ref)

    gate = _dot(gz_ref[...].astype(BF16), w2_ref[...]) + bg_ref[...]
    log_alpha = (jnp.minimum(gate, 0.0) - jnp.log1p(jnp.exp(-jnp.abs(gate)))) / GLA_GATE_TAU
    ri = lax.broadcasted_iota(jnp.int32, (tt, tt), 0)
    ci = lax.broadcasted_iota(jnp.int32, (tt, tt), 1)
    tri = jnp.where((ri >> CHUNK_SHIFT == ci >> CHUNK_SHIFT) & (ci <= ri), 1.0, 0.0)
    bcum_all = _dot(tri, log_alpha, precision=lax.Precision.HIGHEST)
    causal = (lax.broadcasted_iota(jnp.int32, (c, c), 1) <= lax.broadcasted_iota(jnp.int32, (c, c), 0))

    for n in range(tt // c):
        rs = slice(n * c, (n + 1) * c)
        bcum = bcum_all[rs]
        b_last = bcum[c - 1:c, :]
        e_pos = jnp.exp(bcum)
        q_t = (q_ref[rs, :] * dh ** -0.5) * e_pos
        kk = k_ref[rs, :]
        k_t = kk * jnp.exp(-bcum)
        k_end = kk * jnp.exp(b_last - bcum)
        decay = jnp.exp(b_last)
        vv = v_ref[rs, :]
        for hh in range(GLA_HEADS):
            cs = slice(hh * dh, (hh + 1) * dh)
            qh = q_t[:, cs].astype(BF16)
            vh = vv[:, cs]
            att = jnp.where(causal, _dot_nt(qh, k_t[:, cs].astype(BF16)), 0.0)
            st = st_ref[hh]
            o = _dot(att.astype(BF16), vh.astype(BF16)) + _dot_nt(qh, st.astype(BF16))
            st_ref[hh] = st * decay[:, cs] + _dot(vh.T.astype(BF16), k_end[:, cs].astype(BF16))
            o = _rms_scale(o) * ng_ref[:, cs]
            rr = r_ref[rs, cs]
            o_ref[rs, cs] = o * (rr * _sigmoid(rr))


def _gla(proj, w2pad, bg, ng, bsz, seq):
    tt = min(256, seq)
    nt = seq // tt
    w = GLA_WIDTH
    cols = [COL_GLA_Q // w, COL_GLA_K // w, COL_GLA_V // w, COL_GLA_R // w]
    seg = [pl.BlockSpec((tt, w), functools.partial(lambda b, i, cc: (b * nt + i, cc), cc=cc)) for cc in cols]
    vec = pl.BlockSpec((1, w), lambda b, i: (0, 0))
    return pl.pallas_call(
        functools.partial(_gla_kernel, tt=tt),
        grid=(bsz, nt),
        in_specs=seg + [pl.BlockSpec((tt, LANES), lambda b, i: (b * nt + i, COL_GZ // LANES)),
                        pl.BlockSpec((LANES, w), lambda b, i: (0, 0)), vec, vec],
        out_specs=pl.BlockSpec((tt, w), lambda b, i: (b * nt + i, 0)),
        out_shape=jax.ShapeDtypeStruct((bsz * seq, w), F32),
        scratch_shapes=[pltpu.VMEM((GLA_HEADS, GLA_HEAD_DIM, GLA_HEAD_DIM), F32)],
        compiler_params=_params(("parallel", "arbitrary")),
        name="gla",
    )(proj, proj, proj, proj, proj, w2pad, bg, ng)


def _out_proj_kernel(ocmp_ref, oselwin_ref, olru_ref, ogla_ref, h_ref, w_ref, g_ref, o_ref, wb_ref):
    @pl.when(pl.program_id(0) == 0)
    def _():
        wb_ref[...] = w_ref[...].astype(BF16)

    nsa = ocmp_ref[...] + oselwin_ref[...]
    y = _dot(nsa.astype(BF16), wb_ref[0:NSA_WIDTH, :])
    y = y + _dot(olru_ref[...].astype(BF16), wb_ref[NSA_WIDTH:NSA_WIDTH + LRU_WIDTH, :])
    y = y + _dot(ogla_ref[...].astype(BF16), wb_ref[NSA_WIDTH + LRU_WIDTH:, :])
    o_ref[...] = h_ref[...] + _rms_scale(y) * g_ref[...]


def _resident(shape, layer):
    zeros = (0,) * len(shape)
    return pl.BlockSpec((None,) + shape, lambda i: (layer,) + zeros, pipeline_mode=pl.Buffered(1))


def _out_proj(ocmp, oselwin, olru, ogla, h2d, w_out, g, layer):
    m = h2d.shape[0]
    tm = min(256, m)
    row = lambda width: pl.BlockSpec((tm, width), lambda i: (i, 0))
    return pl.pallas_call(
        _out_proj_kernel,
        grid=(m // tm,),
        in_specs=[row(NSA_WIDTH), row(NSA_WIDTH), row(LRU_WIDTH), row(GLA_WIDTH), row(D_MODEL),
                  _resident((D_MODEL, D_MODEL), layer),
                  pl.BlockSpec((1, D_MODEL), lambda i: (0, 0))],
        out_specs=row(D_MODEL),
        out_shape=jax.ShapeDtypeStruct((m, D_MODEL), F32),
        scratch_shapes=[pltpu.VMEM((D_MODEL, D_MODEL), BF16)],
        compiler_params=_params(("arbitrary",)),
        name="out_proj",
    )(ocmp, oselwin, olru, ogla, h2d, w_out, g)


def _mlp_kernel(h_ref, gpre_ref, wu_ref, wd_ref, gpost_ref, o_ref, un_ref):
    f = pl.program_id(1)

    @pl.when(f == 0)
    def _():
        un_ref[...] = (_rms_scale(h_ref[...]) * gpre_ref[...]).astype(BF16)
        o_ref[...] = jnp.zeros_like(o_ref)

    a = jnp.maximum(_dot(un_ref[...], wu_ref[...].astype(BF16)), 0.0)
    o_ref[...] += _dot((a * a).astype(BF16), wd_ref[...].astype(BF16))

    @pl.when(f == pl.num_programs(1) - 1)
    def _():
        o_ref[...] = h_ref[...] + _rms_scale(o_ref[...]) * gpost_ref[...]


def _mlp(h2d, gpre, w_up, w_down, gpost, layer):
    m = h2d.shape[0]
    tm = min(1024, m)
    tf = 512
    vec = pl.BlockSpec((1, D_MODEL), lambda i, f: (0, 0))
    return pl.pallas_call(
        _mlp_kernel,
        grid=(m // tm, D_FF // tf),
        in_specs=[pl.BlockSpec((tm, D_MODEL), lambda i, f: (i, 0)), vec,
                  pl.BlockSpec((None, D_MODEL, tf), lambda i, f: (layer, 0, f)),
                  pl.BlockSpec((None, tf, D_MODEL), lambda i, f: (layer, f, 0)), vec],
        out_specs=pl.BlockSpec((tm, D_MODEL), lambda i, f: (i, 0), pipeline_mode=pl.Buffered(1)),
        out_shape=jax.ShapeDtypeStruct((m, D_MODEL), F32),
        scratch_shapes=[pltpu.VMEM((tm, D_MODEL), BF16)],
        compiler_params=_params(("parallel", "arbitrary")),
        name="mlp",
    )(h2d, gpre, w_up, w_down, gpost)


def _ple_kernel(h_ref, p_ref, wg_ref, wp_ref, o_ref, wgb_ref, wpb_ref):
    @pl.when(pl.program_id(0) == 0)
    def _():
        wgb_ref[...] = wg_ref[...].astype(BF16)
        wpb_ref[...] = wp_ref[...].astype(BF16)

    h = h_ref[...]
    gate = _sigmoid(_dot(h.astype(BF16), wgb_ref[...]))
    o_ref[...] = h + gate * _dot(p_ref[...].astype(BF16), wpb_ref[...])


def _ple(h2d, p_all, w_gate, w_ple, layer):
    m = h2d.shape[0]
    tm = min(512, m)
    return pl.pallas_call(
        _ple_kernel,
        grid=(m // tm,),
        in_specs=[pl.BlockSpec((tm, D_MODEL), lambda i: (i, 0)),
                  pl.BlockSpec((None, tm, PLE_DIM), lambda i: (layer, i, 0)),
                  _resident((D_MODEL, D_MODEL), layer),
                  _resident((PLE_DIM, D_MODEL), layer)],
        out_specs=pl.BlockSpec((tm, D_MODEL), lambda i: (i, 0)),
        out_shape=jax.ShapeDtypeStruct((m, D_MODEL), F32),
        scratch_shapes=[pltpu.VMEM((D_MODEL, D_MODEL), BF16), pltpu.VMEM((PLE_DIM, D_MODEL), BF16)],
        compiler_params=_params(("arbitrary",)),
        name="ple",
    )(h2d, p_all, w_gate, w_ple)


def _w_in_views(w_in):
    w_t = jnp.swapaxes(w_in, 1, 2)
    g0 = NSA_WIDTH + 6 * NSA_KV_WIDTH
    z0 = D_IN - GLA_GATE_RANK
    w_gz = jnp.concatenate([w_t[:, g0:g0 + N_GATES], w_t[:, z0:]], axis=1)
    return w_t, w_gz


def _compress_weights(cmp_w, cmp_pe):
    w4 = cmp_w.reshape(2, CMP_LEN, DK, DK)
    eye = jnp.eye(2, dtype=cmp_w.dtype)
    wbd = jnp.einsum("clde,xy->clxdye", w4, eye).reshape(2, CMP_LEN, LANES, LANES).astype(BF16)
    pe = jnp.tile(cmp_pe, (1, 1, 2)).reshape(2, CMP_LEN, 1, LANES)
    return pe, wbd


def _block_diag(w):
    eye = jnp.eye(LRU_BLOCKS, dtype=w.dtype)
    return jnp.einsum("ncd,nm->ncmd", w, eye).reshape(LRU_WIDTH, LRU_WIDTH).astype(BF16)


def _layer(h2d, bsz, seq, layer, stacked, norm_mix_pre, nsa_cmp_w, nsa_cmp_pe, lru_conv_w, lru_conv_b,
           lru_wa, lru_ba, lru_wi, lru_bi, lru_lambda, gla_w_gate2, gla_b_gate, gla_norm, norm_mix_post,
           norm_mlp_pre, norm_mlp_post):
    w_in_all, w_out_all, w_up_all, w_down_all, w_ple_gate_all, w_ple_all, p_all = stacked
    row = lambda v: v.reshape(1, -1)
    proj = _in_proj(h2d, row(norm_mix_pre), *w_in_all, layer)

    cmp_kv = _compress(proj, *_compress_weights(nsa_cmp_w, nsa_cmp_pe), bsz, seq)
    o_cmp, selt, cnt = _cmp_attn(proj, cmp_kv, bsz, seq)
    slopes = jnp.asarray(SLOPES, F32)
    o_sel_win = _flash(proj, selt, cnt, slopes, bsz, seq)

    o_lru = _lru(proj, lru_conv_w, row(lru_conv_b), _block_diag(lru_wa), row(lru_ba),
                 _block_diag(lru_wi), row(lru_bi), row(lru_lambda), bsz, seq)

    w2pad = jnp.zeros((LANES, GLA_WIDTH), F32).at[N_GATES:N_GATES + GLA_GATE_RANK].set(gla_w_gate2)
    o_gla = _gla(proj, w2pad.astype(BF16), row(gla_b_gate), row(gla_norm), bsz, seq)

    h2d = _out_proj(o_cmp, o_sel_win, o_lru, o_gla, h2d, w_out_all, row(norm_mix_post), layer)
    h2d = _mlp(h2d, row(norm_mlp_pre), w_up_all, w_down_all, row(norm_mlp_post), layer)
    return _ple(h2d, p_all, w_ple_gate_all, w_ple_all, layer)


def kernel(x, p, norm_mix_pre, w_in, nsa_cmp_w, nsa_cmp_pe, lru_conv_w, lru_conv_b, lru_wa, lru_ba,
           lru_wi, lru_bi, lru_lambda, gla_w_gate2, gla_b_gate, gla_norm, w_out, norm_mix_post,
           norm_mlp_pre, w_up, w_down, norm_mlp_post, w_ple_gate, w_ple):
    bsz, seq, _ = x.shape
    h2d = x.reshape(bsz * seq, D_MODEL)
    small = (norm_mix_pre, nsa_cmp_w, nsa_cmp_pe, lru_conv_w, lru_conv_b, lru_wa, lru_ba, lru_wi, lru_bi,
             lru_lambda, gla_w_gate2, gla_b_gate, gla_norm, norm_mix_post, norm_mlp_pre, norm_mlp_post)
    depth = p.shape[0]
    stacked = (_w_in_views(w_in), w_out, w_up, w_down, w_ple_gate, w_ple,
               p.reshape(depth, bsz * seq, PLE_DIM))
    for i in range(depth):
        h2d = _layer(h2d, bsz, seq, i, stacked, *(w[i] for w in small))
    return h2d.reshape(bsz, seq, D_MODEL)
```

```python
import functools

import jax
import jax.numpy as jnp
from jax import lax
from jax.experimental import pallas as pl
from jax.experimental.pallas import tpu as pltpu

F32 = jnp.float32
BF16 = jnp.bfloat16

D_MODEL = 2048
PLE_DIM = 256
NSA_HEADS = 16
NSA_KV_HEADS = 4
HPG = NSA_HEADS // NSA_KV_HEADS
NSA_WIDTH = 1024
DK = 64
NSA_KV_WIDTH = NSA_KV_HEADS * DK
CMP_LEN = 32
CMP_STRIDE = 16
SLC_BLOCK = 64
SLC_SHIFT = 6
N_SELECT = 16
WINDOW = 512
LRU_WIDTH = 512
LRU_BLOCKS = 8
LRU_BLOCK_DIM = 64
CONV_WIDTH = 4
LRU_C = 8.0
GLA_WIDTH = 512
GLA_HEADS = 4
GLA_HEAD_DIM = 128
GLA_GATE_RANK = 16
GLA_GATE_TAU = 16.0
GLA_CHUNK = 64
CHUNK_SHIFT = 6
D_FF = 4 * D_MODEL
EPS = 1e-6
NEG_INF = -1e30
MASKED = 2.0 * NEG_INF

COL_Q = 0
COL_KV = 1024
COL_LRU_X = 2560
COL_LRU_Y = 3072
COL_GLA_Q = 3584
COL_GLA_K = 4096
COL_GLA_V = 4608
COL_GLA_R = 5120
COL_GZ = 5632
D_IN_PAD = 6144
N_GATES = 3 * NSA_HEADS
LANES = 128
KEY_TILE = 128
VT_ROWS = DK + 16
LOG2E = 1.4426950408889634

SLOPES = tuple(2.0 ** (-8.0 * i / NSA_HEADS) for i in range(1, NSA_HEADS + 1))

VMEM_LIMIT = 56 * 1024 * 1024

NT_DIMS = (((1,), (1,)), ((), ()))
TN_DIMS = (((0,), (0,)), ((), ()))


def _params(sem):
    return pltpu.CompilerParams(dimension_semantics=sem, vmem_limit_bytes=VMEM_LIMIT)


def _dot(a, b, **kw):
    return jnp.dot(a, b, preferred_element_type=F32, **kw)


def _dot_nt(a, b, **kw):
    return lax.dot_general(a, b, NT_DIMS, preferred_element_type=F32, **kw)


def _dot_tn(a, b, **kw):
    return lax.dot_general(a, b, TN_DIMS, preferred_element_type=F32, **kw)


def _sigmoid(x):
    return 1.0 / (1.0 + jnp.exp(-x))


def _rms_scale(x):
    return x * lax.rsqrt(jnp.mean(x * x, axis=-1, keepdims=True) + EPS)


def _half_masks(rows):
    lane = lax.broadcasted_iota(jnp.int32, (rows, LANES), 1)
    return lane < DK, lane >= DK


def _pad_halves(blk):
    lo, hi = _half_masks(blk.shape[0])
    swapped = pltpu.roll(blk, DK, 1)
    zero = jnp.zeros_like(blk)
    return (jnp.where(lo, blk, zero), jnp.where(hi, swapped, zero),
            jnp.where(lo, swapped, zero), jnp.where(hi, blk, zero))


def _untranspose_pairs(o_t, tq):
    sub = lax.broadcasted_iota(jnp.int32, (4 * DK, LANES), 0)
    lane = lax.broadcasted_iota(jnp.int32, (4 * DK, LANES), 1)
    place = jnp.where(lane == (sub & (DK - 1)) + jnp.where(sub >= 2 * DK, DK, 0), 1.0, 0.0).astype(BF16)
    out = []
    for pair in range(o_t.shape[1] // (2 * tq)):
        parts = []
        for half in range(2):
            v = o_t[:, (2 * pair + half) * tq:(2 * pair + half + 1) * tq]
            v_hi = v.astype(BF16)
            parts += [v_hi, (v - v_hi.astype(F32)).astype(BF16)]
        out.append(_dot_tn(jnp.concatenate(parts, axis=0), place))
    return out


IN_TILE = 512
HEAD_TILES = (NSA_WIDTH + 6 * NSA_KV_WIDTH) // IN_TILE
TAIL_TILES = (2 * LRU_WIDTH + 4 * GLA_WIDTH) // IN_TILE
D_IN = NSA_WIDTH + 6 * NSA_KV_WIDTH + N_GATES + 2 * LRU_WIDTH + 4 * GLA_WIDTH + GLA_GATE_RANK


def _in_proj_kernel(x_ref, g_ref, w_ref, wgz_ref, o_ref, xn_ref, wres_ref):
    i = pl.program_id(0)
    j = pl.program_id(1)
    last = HEAD_TILES + TAIL_TILES
    n_gz = N_GATES + GLA_GATE_RANK
    k = wres_ref.shape[1]

    @pl.when(j == 0)
    def _():
        xn_ref[...] = (_rms_scale(x_ref[...]) * g_ref[...]).astype(BF16)

    @pl.when((i == 0) & (j < last))
    def _():
        wres_ref[j] = w_ref[0].T.astype(BF16)

    @pl.when((i == 0) & (j == last))
    def _():
        head = jnp.concatenate([wgz_ref[...], jnp.zeros((LANES - n_gz, k), F32)], axis=0)
        wres_ref[last, :, 0:LANES] = head.T.astype(BF16)
        wres_ref[last, :, LANES:] = jnp.zeros((k, IN_TILE - LANES), BF16)

    o_ref[...] = _dot(xn_ref[...], wres_ref[j])


def _in_proj(h2d, g, w_t, w_gz, layer):
    m, k = h2d.shape
    tm = min(1024, m)
    n_tiles = HEAD_TILES + TAIL_TILES + 1
    assert n_tiles * IN_TILE == D_IN_PAD

    def w_rows(i, j):
        start = jnp.where(j < HEAD_TILES, j * IN_TILE, j * IN_TILE + N_GATES)
        start = jnp.where(i == 0, jnp.minimum(start, D_IN - IN_TILE), D_IN - IN_TILE)
        return layer, pl.multiple_of(start, 8), 0

    return pl.pallas_call(
        _in_proj_kernel,
        grid=(m // tm, n_tiles),
        in_specs=[pl.BlockSpec((tm, k), lambda i, j: (i, 0), pipeline_mode=pl.Buffered(1)),
                  pl.BlockSpec((1, k), lambda i, j: (0, 0)),
                  pl.BlockSpec((pl.Element(1), pl.Element(IN_TILE), pl.Element(k)), w_rows),
                  pl.BlockSpec((None, N_GATES + GLA_GATE_RANK, k), lambda i, j: (layer, 0, 0))],
        out_specs=pl.BlockSpec((tm, IN_TILE), lambda i, j: (i, j)),
        out_shape=jax.ShapeDtypeStruct((m, D_IN_PAD), F32),
        scratch_shapes=[pltpu.VMEM((tm, k), BF16), pltpu.VMEM((n_tiles, k, IN_TILE), BF16)],
        compiler_params=_params(("arbitrary", "arbitrary")),
        name="in_proj",
    )(h2d, g, w_t, w_gz)


def _compress_kernel(x_ref, pe_ref, w_ref, o_ref, *, nc):
    first = jnp.zeros((nc, LANES), F32)
    second = jnp.zeros((nc, LANES), F32)
    for l in range(CMP_STRIDE):
        x = x_ref[pl.ds(l, nc, stride=CMP_STRIDE), :]
        first = first + _dot((x + pe_ref[l]).astype(BF16), w_ref[l])
        second = second + _dot((x + pe_ref[CMP_STRIDE + l]).astype(BF16), w_ref[CMP_STRIDE + l])
    o_ref[...] = first + pltpu.roll(second, nc - 1, 0)


def _compress(proj, pe, w, bsz, seq):
    nc = seq // CMP_STRIDE
    return pl.pallas_call(
        functools.partial(_compress_kernel, nc=nc),
        grid=(bsz, NSA_KV_HEADS),
        in_specs=[pl.BlockSpec((seq, LANES), lambda i, j: (i, COL_KV // LANES + j)),
                  pl.BlockSpec((None, CMP_LEN, 1, LANES), lambda i, j: (j // 2, 0, 0, 0)),
                  pl.BlockSpec((None, CMP_LEN, LANES, LANES), lambda i, j: (j // 2, 0, 0, 0))],
        out_specs=pl.BlockSpec((None, nc, LANES), lambda i, j: (i, 0, j)),
        out_shape=jax.ShapeDtypeStruct((bsz, nc, 2 * NSA_KV_WIDTH), F32),
        compiler_params=_params(("parallel", "parallel")),
        name="nsa_compress",
    )(proj, pe, w)


def _cmp_attn_kernel(q_ref, kv_ref, gz_ref, ocmp_ref, selt_ref, cnt_ref, kz_ref, vct_ref, bias_ref, s_ref,
                     *, tq, nc, nslc, nsel, nb):
    t0 = pl.program_id(1) * tq

    @pl.when(pl.program_id(1) == 0)
    def _():
        for gp in range(NSA_KV_HEADS // 2):
            padded = _pad_halves(kv_ref[:, gp * LANES:(gp + 1) * LANES])
            for idx in range(4):
                kz_ref[gp * 4 + idx] = padded[idx].astype(BF16)
            vct_ref[gp] = kv_ref[:, NSA_KV_WIDTH + gp * LANES:NSA_KV_WIDTH + (gp + 1) * LANES].T.astype(BF16)
        end_n = (lax.broadcasted_iota(jnp.int32, (nc, tq), 0) * CMP_STRIDE + (CMP_LEN - 1)).astype(F32)
        for hh in range(NSA_HEADS):
            bias_ref[hh] = (SLOPES[hh] * LOG2E) * end_n

    q_t = (q_ref[...] * (DK ** -0.5 * LOG2E)).T.astype(BF16)
    for hh in range(NSA_HEADS):
        pair = hh // 2
        s_ref[hh] = _dot(kz_ref[(hh // HPG) * 2 + hh % 2], q_t[pair * LANES:(pair + 1) * LANES, :])

    n_row = lax.broadcasted_iota(jnp.int32, (nc, tq), 0)
    t_col = lax.broadcasted_iota(jnp.int32, (nc, tq), 1) + t0
    valid = t_col >= n_row * CMP_STRIDE + (CMP_LEN - 1)
    any_valid = jnp.where(t_col[0:1, :] >= CMP_LEN - 1, 1.0, 0.0)
    jj = lax.broadcasted_iota(jnp.int32, (nslc, nc), 0) * SLC_BLOCK
    nn = lax.broadcasted_iota(jnp.int32, (nslc, nc), 1) * CMP_STRIDE
    overlap_t = jnp.where((nn < jj + SLC_BLOCK) & (jj < nn + CMP_LEN), 1.0, 0.0)
    blk = lax.broadcasted_iota(jnp.int32, (nslc, tq), 0)
    t_blk = (lax.broadcasted_iota(jnp.int32, (nslc, tq), 1) + t0) >> SLC_SHIFT
    forced = (blk == 0) | (blk == t_blk) | (blk == t_blk - 1)
    future = blk > t_blk
    sub = lax.broadcasted_iota(jnp.int32, (8, tq), 0)
    ngrp = nslc // 8
    gate_t = _sigmoid(gz_ref[...].T)
    ones = jnp.ones((8, tq), BF16)

    outs = []
    for g in range(NSA_KV_HEADS):
        vct = vct_ref[g // 2, (g % 2) * DK:(g % 2 + 1) * DK, :]
        psum = jnp.zeros((nc, tq), F32)
        for h in range(HPG):
            hh = g * HPG + h
            s = jnp.where(valid, s_ref[hh] + bias_ref[hh], NEG_INF)
            e = jnp.exp2(s - jnp.max(s, axis=0, keepdims=True))
            p = e * (any_valid / jnp.sum(e, axis=0, keepdims=True))
            outs.append(_dot(vct, p.astype(BF16)) * gate_t[3 * hh:3 * hh + 1, :])
            psum = psum + p
        imp_t = _dot(overlap_t, psum, precision=lax.Precision.HIGHEST)
        score = jnp.where(future, -1.0, jnp.where(forced, 1e4, imp_t))

        sc = [score[8 * r:8 * r + 8] for r in range(ngrp)]
        rank = [jnp.zeros((8, tq), F32) for _ in range(ngrp)]
        for i in range(nslc):
            si = jnp.broadcast_to(score[i:i + 1, :], (8, tq))
            for r in range(ngrp):
                if r < i // 8:
                    before = si > sc[r]
                elif r > i // 8:
                    before = si >= sc[r]
                else:
                    before = (si > sc[r]) | ((sub > i % 8) & (si >= sc[r]))
                rank[r] = rank[r] + jnp.where(before, 1.0, 0.0)
        for r in range(ngrp):
            selt_ref[g, 8 * r:8 * r + 8, :] = jnp.where(rank[r] < float(nsel), 1.0, 0.0)
        if g % 2:
            picks = None
            for gi in (g - 1, g):
                for c in range(nb):
                    part = selt_ref[gi, pl.ds(c, nslc // nb, stride=nb), :]
                    picks = part if picks is None else picks + part
            cnt = _dot_nt(ones, picks.astype(BF16))
            cnt_ref[g // 2:g // 2 + 1, :] = cnt[0:1, :].astype(jnp.int32)

    for pair, tile in enumerate(_untranspose_pairs(jnp.concatenate(outs, axis=1), tq)):
        ocmp_ref[:, pair * LANES:(pair + 1) * LANES] = tile


def _cmp_attn(proj, cmp_kv, bsz, seq):
    tq = 128
    nc = cmp_kv.shape[1]
    nslc = seq // SLC_BLOCK
    nsel = min(N_SELECT, nslc)
    nq = seq // tq
    return pl.pallas_call(
        functools.partial(_cmp_attn_kernel, tq=tq, nc=nc, nslc=nslc, nsel=nsel, nb=KEY_TILE // SLC_BLOCK),
        grid=(bsz, nq),
        in_specs=[pl.BlockSpec((tq, NSA_WIDTH), lambda b, i: (b * nq + i, 0)),
                  pl.BlockSpec((None, nc, 2 * NSA_KV_WIDTH), lambda b, i: (b, 0, 0)),
                  pl.BlockSpec((tq, LANES), lambda b, i: (b * nq + i, COL_GZ // LANES))],
        out_specs=[pl.BlockSpec((tq, NSA_WIDTH), lambda b, i: (b * nq + i, 0)),
                   pl.BlockSpec((NSA_KV_HEADS, None, nslc, tq), lambda b, i: (0, b, 0, i)),
                   pl.BlockSpec((None, None, NSA_KV_HEADS // 2, seq // KEY_TILE), lambda b, i: (b, i, 0, 0))],
        out_shape=[jax.ShapeDtypeStruct((bsz * seq, NSA_WIDTH), F32),
                   jax.ShapeDtypeStruct((NSA_KV_HEADS, bsz, nslc, seq), F32),
                   jax.ShapeDtypeStruct((bsz, nq, NSA_KV_HEADS // 2, seq // KEY_TILE), jnp.int32)],
        scratch_shapes=[pltpu.VMEM((2 * NSA_KV_HEADS, nc, LANES), BF16),
                        pltpu.VMEM((NSA_KV_HEADS // 2, LANES, nc), BF16),
                        pltpu.VMEM((NSA_HEADS, nc, tq), F32),
                        pltpu.VMEM((NSA_HEADS, nc, tq), F32)],
        compiler_params=_params(("parallel", "arbitrary")),
        name="nsa_cmp_attn",
    )(proj, cmp_kv, proj)


def _stage_kv(k_ref, v_ref, kz_ref, vt_ref, nk, ts):
    for c in range(nk):
        padded = _pad_halves(k_ref[c * ts:(c + 1) * ts, :])
        for idx in range(4):
            kz_ref[idx, c] = padded[idx].astype(BF16)
        v_t = v_ref[c * ts:(c + 1) * ts, :].T
        ones_row = jnp.where(lax.broadcasted_iota(jnp.int32, (VT_ROWS - DK, ts), 0) == 0, 1.0, 0.0)
        for gi in range(2):
            vt_ref[gi, c, 0:DK, :] = v_t[gi * DK:(gi + 1) * DK, :].astype(BF16)
            vt_ref[gi, c, DK:VT_ROWS, :] = ones_row.astype(BF16)


def _attend(mode, nsub, gp, t0, slopes_ref, cnt_ref, selt_ref, qb_ref, bias_ref, kz_ref, vt_ref, s_ref,
            m_ref, acc_ref, idx_ref, *, tq, ts, nk):
    nb = ts // SLC_BLOCK
    ts_shift = ts.bit_length() - 1
    m_ref[...] = jnp.full(m_ref.shape, NEG_INF, F32)
    acc_ref[...] = jnp.zeros(acc_ref.shape, F32)

    last_sub = (t0 + tq - 1) >> ts_shift
    if mode == "sel":
        def build(j, n):
            idx_ref[n] = j
            return n + jnp.where(((cnt_ref[gp, j] > 0) | (j == last_sub)) & (j <= last_sub), 1, 0)

        n_live = lax.fori_loop(0, nk, build, 0, unroll=4)
    else:
        first_sub = jnp.maximum(t0 - (WINDOW - 1), 0) >> ts_shift
        n_live = last_sub + 1 - first_sub
        for u in range(nsub):
            idx_ref[u] = jnp.minimum(first_sub + u, last_sub)

    key_i = lax.broadcasted_iota(jnp.int32, (ts, tq), 0)
    qry_t = lax.broadcasted_iota(jnp.int32, (ts, tq), 1) + t0

    def sub_tile(step, u):
        pos = step * nsub + u
        return idx_ref[jnp.minimum(pos, n_live - 1)], pos < n_live

    def scores(step):
        for u in range(nsub):
            kj, _ = sub_tile(step, u)
            for gi in range(2):
                for h in range(HPG):
                    pair = gi * (HPG // 2) + h // 2
                    s_ref[u * 2 + gi, :, h * tq:(h + 1) * tq] = _dot(
                        kz_ref[gi * 2 + h % 2, kj], qb_ref[pair * LANES:(pair + 1) * LANES, :])

    def softmax_update(step):
        for u in range(nsub):
            kj, live = sub_tile(step, u)
            j0 = kj * ts
            rel0 = (jnp.zeros((1, tq), jnp.int32) + (j0 - t0)).astype(F32)
            dd = qry_t - (key_i + j0)
            lowest = jnp.where(live, 0, 1 << 30)
            for gi in range(2):
                if mode == "sel":
                    chosen = selt_ref[gi, pl.ds(kj * nb + nb - 1, 1), :]
                    for c in range(nb - 2, -1, -1):
                        chosen = jnp.where(key_i < (c + 1) * SLC_BLOCK,
                                           selt_ref[gi, pl.ds(kj * nb + c, 1), :], chosen)
                    mask = (chosen > 0.5) & (dd >= lowest)
                else:
                    mask = (dd >= lowest) & (dd < WINDOW)
                ps = []
                alphas = []
                for h in range(HPG):
                    shift = (slopes_ref[(gp * 2 + gi) * HPG + h] * LOG2E) * rel0
                    cols = slice(h * tq, (h + 1) * tq)
                    x = jnp.where(mask, s_ref[u * 2 + gi, :, cols] + bias_ref[gi * HPG + h], MASKED)
                    m_old = m_ref[gi, :, cols]
                    m_new = jnp.maximum(m_old, jnp.max(x, axis=0, keepdims=True) + shift)
                    p = jnp.exp2(x - (m_new - shift))
                    m_ref[gi, :, cols] = m_new
                    ps.append(p.astype(BF16))
                    alphas.append(jnp.exp2(m_old - m_new))
                pv = _dot(vt_ref[gi, kj], jnp.concatenate(ps, axis=1))
                acc_ref[gi] = acc_ref[gi] * jnp.concatenate(alphas, axis=1) + pv

    def body(step, carry):
        scores(step)
        softmax_update(step)
        return carry

    if mode == "sel":
        assert nsub & (nsub - 1) == 0
        lax.fori_loop(0, (n_live + nsub - 1) >> (nsub.bit_length() - 1), body, 0)
    else:
        body(0, 0)


def _flash_kernel(slopes_ref, cnt_ref, q_ref, ks_ref, vs_ref, kw_ref, vw_ref, gz_ref, selt_ref, o_ref,
                  kzs_ref, vts_ref, kzw_ref, vtw_ref, qb_ref, bias_ref, gate_ref, s_ref, m_ref, acc_ref, idx_ref,
                  *, tq, ts, nk, nsub_sel, nsub_win):
    gp = pl.program_id(1)
    qi = pl.program_id(2)
    t0 = qi * tq

    @pl.when(qi == 0)
    def _():
        _stage_kv(ks_ref, vs_ref, kzs_ref, vts_ref, nk, ts)
        _stage_kv(kw_ref, vw_ref, kzw_ref, vtw_ref, nk, ts)

    key_f = lax.broadcasted_iota(jnp.int32, (ts, tq), 0).astype(F32)
    qb_ref[...] = (q_ref[...] * (DK ** -0.5 * LOG2E)).T.astype(BF16)
    gate_ref[...] = _sigmoid(gz_ref[...].T)
    for gh in range(2 * HPG):
        bias_ref[gh] = (slopes_ref[gp * 2 * HPG + gh] * LOG2E) * key_f

    common = dict(tq=tq, ts=ts, nk=nk)
    _attend("sel", nsub_sel, gp, t0, slopes_ref, cnt_ref, selt_ref, qb_ref, bias_ref, kzs_ref, vts_ref, s_ref,
            m_ref.at[0], acc_ref.at[0], idx_ref, **common)
    _attend("win", nsub_win, gp, t0, slopes_ref, None, None, qb_ref, bias_ref, kzw_ref, vtw_ref, s_ref,
            m_ref.at[1], acc_ref.at[1], idx_ref, **common)

    outs = []
    for gi in range(2):
        branch_out = [acc_ref[br, gi, 0:DK, :] / acc_ref[br, gi, DK:DK + 1, :] for br in range(2)]
        for h in range(HPG):
            row = (gp * 2 * HPG + gi * HPG + h) * 3
            cols = slice(h * tq, (h + 1) * tq)
            outs.append(branch_out[0][:, cols] * gate_ref[pl.ds(row + 1, 1), :]
                        + branch_out[1][:, cols] * gate_ref[pl.ds(row + 2, 1), :])
    for pair, tile in enumerate(_untranspose_pairs(jnp.concatenate(outs, axis=1), tq)):
        o_ref[:, pair * LANES:(pair + 1) * LANES] = tile


def _flash(proj, selt, cnt, slopes, bsz, seq):
    tq = ts = KEY_TILE
    nsub_sel = 4
    nsub_win = WINDOW // ts + 1
    nq = seq // tq
    nk = seq // ts
    nslc = seq // SLC_BLOCK
    first_kv = (COL_KV + 2 * NSA_KV_WIDTH) // LANES
    kv_spec = lambda which: pl.BlockSpec((seq, LANES), lambda b, g, i: (b, first_kv + 2 * which + g))
    smem = pltpu.SMEM
    max_units = 2 * max(nsub_sel, nsub_win)
    return pl.pallas_call(
        functools.partial(_flash_kernel, tq=tq, ts=ts, nk=nk, nsub_sel=nsub_sel, nsub_win=nsub_win),
        grid=(bsz, 2, nq),
        in_specs=[pl.BlockSpec(memory_space=smem),
                  pl.BlockSpec((None, None, 2, nk), lambda b, g, i: (b, i, 0, 0), memory_space=smem),
                  pl.BlockSpec((tq, 2 * HPG * DK), lambda b, g, i: (b * nq + i, g)),
                  kv_spec(0), kv_spec(1), kv_spec(2), kv_spec(3),
                  pl.BlockSpec((tq, LANES), lambda b, g, i: (b * nq + i, COL_GZ // LANES)),
                  pl.BlockSpec((2, None, nslc, tq), lambda b, g, i: (g, b, 0, i))],
        out_specs=pl.BlockSpec((tq, 2 * HPG * DK), lambda b, g, i: (b * nq + i, g)),
        out_shape=jax.ShapeDtypeStruct((bsz * seq, NSA_WIDTH), F32),
        scratch_shapes=[pltpu.VMEM((4, nk, ts, LANES), BF16),
                        pltpu.VMEM((2, nk, VT_ROWS, ts), BF16),
                        pltpu.VMEM((4, nk, ts, LANES), BF16),
                        pltpu.VMEM((2, nk, VT_ROWS, ts), BF16),
                        pltpu.VMEM((2 * HPG * DK, tq), BF16),
                        pltpu.VMEM((2 * HPG, ts, tq), F32),
                        pltpu.VMEM((LANES, tq), F32),
                        pltpu.VMEM((max_units, ts, HPG * tq), F32),
                        pltpu.VMEM((2, 2, 1, HPG * tq), F32),
                        pltpu.VMEM((2, 2, VT_ROWS, HPG * tq), F32),
                        pltpu.SMEM((nk + max_units,), jnp.int32)],
        compiler_params=_params(("parallel", "parallel", "arbitrary")),
        name="nsa_sel_win",
    )(slopes, cnt, proj, proj, proj, proj, proj, proj, selt)


def _lru_kernel(x_ref, y_ref, cw_ref, cb_ref, wa_ref, ba_ref, wi_ref, bi_ref, lam_ref, o_ref,
                tail_ref, h_ref, *, tt):
    @pl.when(pl.program_id(1) == 0)
    def _():
        tail_ref[...] = jnp.zeros_like(tail_ref)
        h_ref[...] = jnp.zeros_like(h_ref)

    x = x_ref[...]
    ext = jnp.concatenate([tail_ref[...], x], axis=0)
    xc = cb_ref[...] + x * cw_ref[CONV_WIDTH - 1:CONV_WIDTH, :]
    for k in range(CONV_WIDTH - 1):
        back = CONV_WIDTH - 1 - k
        xc = xc + ext[8 - back:8 - back + tt, :] * cw_ref[k:k + 1, :]
    tail_ref[...] = x[tt - 8:, :]

    xcb = xc.astype(BF16)
    r = _sigmoid(_dot(xcb, wa_ref[...]) + ba_ref[...])
    gate_i = _sigmoid(_dot(xcb, wi_ref[...]) + bi_ref[...])
    neg_lam = -lam_ref[...]
    softplus = jnp.maximum(neg_lam, 0.0) + jnp.log1p(jnp.exp(-jnp.abs(neg_lam)))
    log_a = -LRU_C * r * softplus
    a = jnp.exp(log_a)
    u = jnp.sqrt(jnp.tanh(-log_a) * (a * a + 1.0)) * (gate_i * xc)

    rows = lax.broadcasted_iota(jnp.int32, (tt, 1), 0)
    step = 1
    while step < tt:
        keep = rows >= step
        a_prev = jnp.where(keep, pltpu.roll(a, step, 0), 1.0)
        u_prev = jnp.where(keep, pltpu.roll(u, step, 0), 0.0)
        u = u + a * u_prev
        a = a * a_prev
        step *= 2
    hs = u + a * h_ref[...]
    h_ref[...] = hs[tt - 1:tt, :]
    o_ref[...] = hs * jax.nn.gelu(y_ref[...], approximate=True)


def _lru(proj, cw, cb, wa, ba, wi, bi, lam, bsz, seq):
    tt = min(512, seq)
    nt = seq // tt
    w = LRU_WIDTH
    xcol = COL_LRU_X // w
    ycol = COL_LRU_Y // w
    vec = pl.BlockSpec((1, w), lambda b, i: (0, 0))
    mat = pl.BlockSpec((w, w), lambda b, i: (0, 0))
    return pl.pallas_call(
        functools.partial(_lru_kernel, tt=tt),
        grid=(bsz, nt),
        in_specs=[pl.BlockSpec((tt, w), lambda b, i: (b * nt + i, xcol)),
                  pl.BlockSpec((tt, w), lambda b, i: (b * nt + i, ycol)),
                  pl.BlockSpec((CONV_WIDTH, w), lambda b, i: (0, 0)),
                  vec, mat, vec, mat, vec, vec],
        out_specs=pl.BlockSpec((tt, w), lambda b, i: (b * nt + i, 0)),
        out_shape=jax.ShapeDtypeStruct((bsz * seq, w), F32),
        scratch_shapes=[pltpu.VMEM((8, w), F32), pltpu.VMEM((1, w), F32)],
        compiler_params=_params(("parallel", "arbitrary")),
        name="rglru",
    )(proj, proj, cw, cb, wa, ba, wi, bi, lam)


def _gla_kernel(q_ref, k_ref, v_ref, r_ref, gz_ref, w2_ref, bg_ref, ng_ref, o_ref, st_ref, *, tt):
    c = GLA_CHUNK
    dh = GLA_HEAD_DIM

    @pl.when(pl.program_id(1) == 0)
    def _():
        st_ref[...] = jnp.zeros_like(st_ref)

    gate = _dot(gz_ref[...].astype(BF16), w2_ref[...]) + bg_ref[...]
    log_alpha = (jnp.minimum(gate, 0.0) - jnp.log1p(jnp.exp(-jnp.abs(gate)))) / GLA_GATE_TAU
    ri = lax.broadcasted_iota(jnp.int32, (tt, tt), 0)
    ci = lax.broadcasted_iota(jnp.int32, (tt, tt), 1)
    tri = jnp.where((ri >> CHUNK_SHIFT == ci >> CHUNK_SHIFT) & (ci <= ri), 1.0, 0.0)
    bcum_all = _dot(tri, log_alpha, precision=lax.Precision.HIGHEST)
    causal = (lax.broadcasted_iota(jnp.int32, (c, c), 1) <= lax.broadcasted_iota(jnp.int32, (c, c), 0))

    for n in range(tt // c):
        rs = slice(n * c, (n + 1) * c)
        bcum = bcum_all[rs]
        b_last = bcum[c - 1:c, :]
        e_pos = jnp.exp(bcum)
        q_t = (q_ref[rs, :] * dh ** -0.5) * e_pos
        kk = k_ref[rs, :]
        k_t = kk * jnp.exp(-bcum)
        k_end = kk * jnp.exp(b_last - bcum)
        decay = jnp.exp(b_last)
        vv = v_ref[rs, :]
        for hh in range(GLA_HEADS):
            cs = slice(hh * dh, (hh + 1) * dh)
            qh = q_t[:, cs].astype(BF16)
            vh = vv[:, cs]
            att = jnp.where(causal, _dot_nt(qh, k_t[:, cs].astype(BF16)), 0.0)
            st = st_ref[hh]
            o = _dot(att.astype(BF16), vh.astype(BF16)) + _dot_nt(qh, st.astype(BF16))
            st_ref[hh] = st * decay[:, cs] + _dot(vh.T.astype(BF16), k_end[:, cs].astype(BF16))
            o = _rms_scale(o) * ng_ref[:, cs]
            rr = r_ref[rs, cs]
            o_ref[rs, cs] = o * (rr * _sigmoid(rr))


def _gla(proj, w2pad, bg, ng, bsz, seq):
    tt = min(256, seq)
    nt = seq // tt
    w = GLA_WIDTH
    cols = [COL_GLA_Q // w, COL_GLA_K // w, COL_GLA_V // w, COL_GLA_R // w]
    seg = [pl.BlockSpec((tt, w), functools.partial(lambda b, i, cc: (b * nt + i, cc), cc=cc)) for cc in cols]
    vec = pl.BlockSpec((1, w), lambda b, i: (0, 0))
    return pl.pallas_call(
        functools.partial(_gla_kernel, tt=tt),
        grid=(bsz, nt),
        in_specs=seg + [pl.BlockSpec((tt, LANES), lambda b, i: (b * nt + i, COL_GZ // LANES)),
                        pl.BlockSpec((LANES, w), lambda b, i: (0, 0)), vec, vec],
        out_specs=pl.BlockSpec((tt, w), lambda b, i: (b * nt + i, 0)),
        out_shape=jax.ShapeDtypeStruct((bsz * seq, w), F32),
        scratch_shapes=[pltpu.VMEM((GLA_HEADS, GLA_HEAD_DIM, GLA_HEAD_DIM), F32)],
        compiler_params=_params(("parallel", "arbitrary")),
        name="gla",
    )(proj, proj, proj, proj, proj, w2pad, bg, ng)


def _out_proj_kernel(ocmp_ref, oselwin_ref, olru_ref, ogla_ref, h_ref, w_ref, g_ref, o_ref, wb_ref):
    @pl.when(pl.program_id(0) == 0)
    def _():
        wb_ref[...] = w_ref[...].astype(BF16)

    nsa = ocmp_ref[...] + oselwin_ref[...]
    y = _dot(nsa.astype(BF16), wb_ref[0:NSA_WIDTH, :])
    y = y + _dot(olru_ref[...].astype(BF16), wb_ref[NSA_WIDTH:NSA_WIDTH + LRU_WIDTH, :])
    y = y + _dot(ogla_ref[...].astype(BF16), wb_ref[NSA_WIDTH + LRU_WIDTH:, :])
    o_ref[...] = h_ref[...] + _rms_scale(y) * g_ref[...]


def _resident(shape, layer):
    zeros = (0,) * len(shape)
    return pl.BlockSpec((None,) + shape, lambda i: (layer,) + zeros, pipeline_mode=pl.Buffered(1))


def _out_proj(ocmp, oselwin, olru, ogla, h2d, w_out, g, layer):
    m = h2d.shape[0]
    tm = min(256, m)
    row = lambda width: pl.BlockSpec((tm, width), lambda i: (i, 0))
    return pl.pallas_call(
        _out_proj_kernel,
        grid=(m // tm,),
        in_specs=[row(NSA_WIDTH), row(NSA_WIDTH), row(LRU_WIDTH), row(GLA_WIDTH), row(D_MODEL),
                  _resident((D_MODEL, D_MODEL), layer),
                  pl.BlockSpec((1, D_MODEL), lambda i: (0, 0))],
        out_specs=row(D_MODEL),
        out_shape=jax.ShapeDtypeStruct((m, D_MODEL), F32),
        scratch_shapes=[pltpu.VMEM((D_MODEL, D_MODEL), BF16)],
        compiler_params=_params(("arbitrary",)),
        name="out_proj",
    )(ocmp, oselwin, olru, ogla, h2d, w_out, g)


def _mlp_kernel(h_ref, gpre_ref, wu_ref, wd_ref, gpost_ref, o_ref, un_ref):
    f = pl.program_id(1)

    @pl.when(f == 0)
    def _():
        un_ref[...] = (_rms_scale(h_ref[...]) * gpre_ref[...]).astype(BF16)
        o_ref[...] = jnp.zeros_like(o_ref)

    a = jnp.maximum(_dot(un_ref[...], wu_ref[...].astype(BF16)), 0.0)
    o_ref[...] += _dot((a * a).astype(BF16), wd_ref[...].astype(BF16))

    @pl.when(f == pl.num_programs(1) - 1)
    def _():
        o_ref[...] = h_ref[...] + _rms_scale(o_ref[...]) * gpost_ref[...]


def _mlp(h2d, gpre, w_up, w_down, gpost, layer):
    m = h2d.shape[0]
    tm = min(1024, m)
    tf = 512
    vec = pl.BlockSpec((1, D_MODEL), lambda i, f: (0, 0))
    return pl.pallas_call(
        _mlp_kernel,
        grid=(m // tm, D_FF // tf),
        in_specs=[pl.BlockSpec((tm, D_MODEL), lambda i, f: (i, 0)), vec,
                  pl.BlockSpec((None, D_MODEL, tf), lambda i, f: (layer, 0, f)),
                  pl.BlockSpec((None, tf, D_MODEL), lambda i, f: (layer, f, 0)), vec],
        out_specs=pl.BlockSpec((tm, D_MODEL), lambda i, f: (i, 0), pipeline_mode=pl.Buffered(1)),
        out_shape=jax.ShapeDtypeStruct((m, D_MODEL), F32),
        scratch_shapes=[pltpu.VMEM((tm, D_MODEL), BF16)],
        compiler_params=_params(("parallel", "arbitrary")),
        name="mlp",
    )(h2d, gpre, w_up, w_down, gpost)


def _ple_kernel(h_ref, p_ref, wg_ref, wp_ref, o_ref, wgb_ref, wpb_ref):
    @pl.when(pl.program_id(0) == 0)
    def _():
        wgb_ref[...] = wg_ref[...].astype(BF16)
        wpb_ref[...] = wp_ref[...].astype(BF16)

    h = h_ref[...]
    gate = _sigmoid(_dot(h.astype(BF16), wgb_ref[...]))
    o_ref[...] = h + gate * _dot(p_ref[...].astype(BF16), wpb_ref[...])


def _ple(h2d, p_all, w_gate, w_ple, layer):
    m = h2d.shape[0]
    tm = min(512, m)
    return pl.pallas_call(
        _ple_kernel,
        grid=(m // tm,),
        in_specs=[pl.BlockSpec((tm, D_MODEL), lambda i: (i, 0)),
                  pl.BlockSpec((None, tm, PLE_DIM), lambda i: (layer, i, 0)),
                  _resident((D_MODEL, D_MODEL), layer),
                  _resident((PLE_DIM, D_MODEL), layer)],
        out_specs=pl.BlockSpec((tm, D_MODEL), lambda i: (i, 0)),
        out_shape=jax.ShapeDtypeStruct((m, D_MODEL), F32),
        scratch_shapes=[pltpu.VMEM((D_MODEL, D_MODEL), BF16), pltpu.VMEM((PLE_DIM, D_MODEL), BF16)],
        compiler_params=_params(("arbitrary",)),
        name="ple",
    )(h2d, p_all, w_gate, w_ple)


def _w_in_views(w_in):
    w_t = jnp.swapaxes(w_in, 1, 2)
    g0 = NSA_WIDTH + 6 * NSA_KV_WIDTH
    z0 = D_IN - GLA_GATE_RANK
    w_gz = jnp.concatenate([w_t[:, g0:g0 + N_GATES], w_t[:, z0:]], axis=1)
    return w_t, w_gz


def _compress_weights(cmp_w, cmp_pe):
    w4 = cmp_w.reshape(2, CMP_LEN, DK, DK)
    eye = jnp.eye(2, dtype=cmp_w.dtype)
    wbd = jnp.einsum("clde,xy->clxdye", w4, eye).reshape(2, CMP_LEN, LANES, LANES).astype(BF16)
    pe = jnp.tile(cmp_pe, (1, 1, 2)).reshape(2, CMP_LEN, 1, LANES)
    return pe, wbd


def _block_diag(w):
    eye = jnp.eye(LRU_BLOCKS, dtype=w.dtype)
    return jnp.einsum("ncd,nm->ncmd", w, eye).reshape(LRU_WIDTH, LRU_WIDTH).astype(BF16)


def _layer(h2d, bsz, seq, layer, stacked, norm_mix_pre, nsa_cmp_w, nsa_cmp_pe, lru_conv_w, lru_conv_b,
           lru_wa, lru_ba, lru_wi, lru_bi, lru_lambda, gla_w_gate2, gla_b_gate, gla_norm, norm_mix_post,
           norm_mlp_pre, norm_mlp_post):
    w_in_all, w_out_all, w_up_all, w_down_all, w_ple_gate_all, w_ple_all, p_all = stacked
    row = lambda v: v.reshape(1, -1)
    proj = _in_proj(h2d, row(norm_mix_pre), *w_in_all, layer)

    cmp_kv = _compress(proj, *_compress_weights(nsa_cmp_w, nsa_cmp_pe), bsz, seq)
    o_cmp, selt, cnt = _cmp_attn(proj, cmp_kv, bsz, seq)
    slopes = jnp.asarray(SLOPES, F32)
    o_sel_win = _flash(proj, selt, cnt, slopes, bsz, seq)

    o_lru = _lru(proj, lru_conv_w, row(lru_conv_b), _block_diag(lru_wa), row(lru_ba),
                 _block_diag(lru_wi), row(lru_bi), row(lru_lambda), bsz, seq)

    w2pad = jnp.zeros((LANES, GLA_WIDTH), F32).at[N_GATES:N_GATES + GLA_GATE_RANK].set(gla_w_gate2)
    o_gla = _gla(proj, w2pad.astype(BF16), row(gla_b_gate), row(gla_norm), bsz, seq)

    h2d = _out_proj(o_cmp, o_sel_win, o_lru, o_gla, h2d, w_out_all, row(norm_mix_post), layer)
    h2d = _mlp(h2d, row(norm_mlp_pre), w_up_all, w_down_all, row(norm_mlp_post), layer)
    return _ple(h2d, p_all, w_ple_gate_all, w_ple_all, layer)


def kernel(x, p, norm_mix_pre, w_in, nsa_cmp_w, nsa_cmp_pe, lru_conv_w, lru_conv_b, lru_wa, lru_ba,
           lru_wi, lru_bi, lru_lambda, gla_w_gate2, gla_b_gate, gla_norm, w_out, norm_mix_post,
           norm_mlp_pre, w_up, w_down, norm_mlp_post, w_ple_gate, w_ple):
    bsz, seq, _ = x.shape
    h2d = x.reshape(bsz * seq, D_MODEL)
    small = (norm_mix_pre, nsa_cmp_w, nsa_cmp_pe, lru_conv_w, lru_conv_b, lru_wa, lru_ba, lru_wi, lru_bi,
             lru_lambda, gla_w_gate2, gla_b_gate, gla_norm, norm_mix_post, norm_mlp_pre, norm_mlp_post)
    depth = p.shape[0]
    stacked = (_w_in_views(w_in), w_out, w_up, w_down, w_ple_gate, w_ple,
               p.reshape(depth, bsz * seq, PLE_DIM))
    for i in range(depth):
        h2d = _layer(h2d, bsz, seq, i, stacked, *(w[i] for w in small))
    return h2d.reshape(bsz, seq, D_MODEL)
```

```python
import functools

import jax
import jax.numpy as jnp
from jax import lax
from jax.experimental import pallas as pl
from jax.experimental.pallas import tpu as pltpu

F32 = jnp.float32
BF16 = jnp.bfloat16

D_MODEL = 2048
PLE_DIM = 256
NSA_HEADS = 16
NSA_KV_HEADS = 4
HPG = NSA_HEADS // NSA_KV_HEADS
NSA_WIDTH = 1024
DK = 64
NSA_KV_WIDTH = NSA_KV_HEADS * DK
CMP_LEN = 32
CMP_STRIDE = 16
SLC_BLOCK = 64
SLC_SHIFT = 6
N_SELECT = 16
WINDOW = 512
LRU_WIDTH = 512
LRU_BLOCKS = 8
LRU_BLOCK_DIM = 64
CONV_WIDTH = 4
LRU_C = 8.0
GLA_WIDTH = 512
GLA_HEADS = 4
GLA_HEAD_DIM = 128
GLA_GATE_RANK = 16
GLA_GATE_TAU = 16.0
GLA_CHUNK = 64
CHUNK_SHIFT = 6
D_FF = 4 * D_MODEL
EPS = 1e-6
NEG_INF = -1e30
MASKED = 2.0 * NEG_INF

COL_Q = 0
COL_KV = 1024
COL_LRU_X = 2560
COL_LRU_Y = 3072
COL_GLA_Q = 3584
COL_GLA_K = 4096
COL_GLA_V = 4608
COL_GLA_R = 5120
COL_GZ = 5632
D_IN_PAD = 6144
N_GATES = 3 * NSA_HEADS
LANES = 128
KEY_TILE = 128
Q_TILE = 256
VT_ROWS = DK + 16
LOG2E = 1.4426950408889634

SLOPES = tuple(2.0 ** (-8.0 * i / NSA_HEADS) for i in range(1, NSA_HEADS + 1))

VMEM_LIMIT = 56 * 1024 * 1024

NT_DIMS = (((1,), (1,)), ((), ()))
TN_DIMS = (((0,), (0,)), ((), ()))


def _params(sem):
    return pltpu.CompilerParams(dimension_semantics=sem, vmem_limit_bytes=VMEM_LIMIT)


def _dot(a, b, **kw):
    return jnp.dot(a, b, preferred_element_type=F32, **kw)


def _dot_nt(a, b, **kw):
    return lax.dot_general(a, b, NT_DIMS, preferred_element_type=F32, **kw)


def _dot_tn(a, b, **kw):
    return lax.dot_general(a, b, TN_DIMS, preferred_element_type=F32, **kw)


def _sigmoid(x):
    return 1.0 / (1.0 + jnp.exp(-x))


def _rms_scale(x):
    return x * lax.rsqrt(jnp.mean(x * x, axis=-1, keepdims=True) + EPS)


def _half_masks(rows):
    lane = lax.broadcasted_iota(jnp.int32, (rows, LANES), 1)
    return lane < DK, lane >= DK


def _pad_halves(blk):
    lo, hi = _half_masks(blk.shape[0])
    swapped = pltpu.roll(blk, DK, 1)
    zero = jnp.zeros_like(blk)
    return (jnp.where(lo, blk, zero), jnp.where(hi, swapped, zero),
            jnp.where(lo, swapped, zero), jnp.where(hi, blk, zero))


def _untranspose_pairs(o_t, tq):
    sub = lax.broadcasted_iota(jnp.int32, (4 * DK, LANES), 0)
    lane = lax.broadcasted_iota(jnp.int32, (4 * DK, LANES), 1)
    place = jnp.where(lane == (sub & (DK - 1)) + jnp.where(sub >= 2 * DK, DK, 0), 1.0, 0.0).astype(BF16)
    out = []
    for pair in range(o_t.shape[1] // (2 * tq)):
        parts = []
        for half in range(2):
            v = o_t[:, (2 * pair + half) * tq:(2 * pair + half + 1) * tq]
            v_hi = v.astype(BF16)
            parts += [v_hi, (v - v_hi.astype(F32)).astype(BF16)]
        out.append(_dot_tn(jnp.concatenate(parts, axis=0), place))
    return out


IN_TILE = 512
HEAD_TILES = (NSA_WIDTH + 6 * NSA_KV_WIDTH) // IN_TILE
TAIL_TILES = (2 * LRU_WIDTH + 4 * GLA_WIDTH) // IN_TILE
D_IN = NSA_WIDTH + 6 * NSA_KV_WIDTH + N_GATES + 2 * LRU_WIDTH + 4 * GLA_WIDTH + GLA_GATE_RANK


def _in_proj_kernel(x_ref, g_ref, w_ref, wgz_ref, o_ref, xn_ref, wres_ref):
    i = pl.program_id(0)
    j = pl.program_id(1)
    last = HEAD_TILES + TAIL_TILES
    n_gz = N_GATES + GLA_GATE_RANK
    k = wres_ref.shape[1]

    @pl.when(j == 0)
    def _():
        xn_ref[...] = (_rms_scale(x_ref[...]) * g_ref[...]).astype(BF16)

    @pl.when((i == 0) & (j < last))
    def _():
        wres_ref[j] = w_ref[0].T.astype(BF16)

    @pl.when((i == 0) & (j == last))
    def _():
        head = jnp.concatenate([wgz_ref[...], jnp.zeros((LANES - n_gz, k), F32)], axis=0)
        wres_ref[last, :, 0:LANES] = head.T.astype(BF16)
        wres_ref[last, :, LANES:] = jnp.zeros((k, IN_TILE - LANES), BF16)

    o_ref[...] = _dot(xn_ref[...], wres_ref[j])


def _in_proj(h2d, g, w_t, w_gz, layer):
    m, k = h2d.shape
    tm = min(1024, m)
    n_tiles = HEAD_TILES + TAIL_TILES + 1
    assert n_tiles * IN_TILE == D_IN_PAD

    def w_rows(i, j):
        start = jnp.where(j < HEAD_TILES, j * IN_TILE, j * IN_TILE + N_GATES)
        start = jnp.where(i == 0, jnp.minimum(start, D_IN - IN_TILE), D_IN - IN_TILE)
        return layer, pl.multiple_of(start, 8), 0

    return pl.pallas_call(
        _in_proj_kernel,
        grid=(m // tm, n_tiles),
        in_specs=[pl.BlockSpec((tm, k), lambda i, j: (i, 0), pipeline_mode=pl.Buffered(1)),
                  pl.BlockSpec((1, k), lambda i, j: (0, 0)),
                  pl.BlockSpec((pl.Element(1), pl.Element(IN_TILE), pl.Element(k)), w_rows),
                  pl.BlockSpec((None, N_GATES + GLA_GATE_RANK, k), lambda i, j: (layer, 0, 0))],
        out_specs=pl.BlockSpec((tm, IN_TILE), lambda i, j: (i, j)),
        out_shape=jax.ShapeDtypeStruct((m, D_IN_PAD), F32),
        scratch_shapes=[pltpu.VMEM((tm, k), BF16), pltpu.VMEM((n_tiles, k, IN_TILE), BF16)],
        compiler_params=_params(("arbitrary", "arbitrary")),
        name="in_proj",
    )(h2d, g, w_t, w_gz)


def _compress_kernel(x_ref, pe_ref, w_ref, o_ref, *, nc):
    first = jnp.zeros((nc, LANES), F32)
    second = jnp.zeros((nc, LANES), F32)
    for l in range(CMP_STRIDE):
        x = x_ref[pl.ds(l, nc, stride=CMP_STRIDE), :]
        first = first + _dot((x + pe_ref[l]).astype(BF16), w_ref[l])
        second = second + _dot((x + pe_ref[CMP_STRIDE + l]).astype(BF16), w_ref[CMP_STRIDE + l])
    o_ref[...] = first + pltpu.roll(second, nc - 1, 0)


def _compress(proj, pe, w, bsz, seq):
    nc = seq // CMP_STRIDE
    return pl.pallas_call(
        functools.partial(_compress_kernel, nc=nc),
        grid=(bsz, NSA_KV_HEADS),
        in_specs=[pl.BlockSpec((seq, LANES), lambda i, j: (i, COL_KV // LANES + j)),
                  pl.BlockSpec((None, CMP_LEN, 1, LANES), lambda i, j: (j // 2, 0, 0, 0)),
                  pl.BlockSpec((None, CMP_LEN, LANES, LANES), lambda i, j: (j // 2, 0, 0, 0))],
        out_specs=pl.BlockSpec((None, nc, LANES), lambda i, j: (i, 0, j)),
        out_shape=jax.ShapeDtypeStruct((bsz, nc, 2 * NSA_KV_WIDTH), F32),
        compiler_params=_params(("parallel", "parallel")),
        name="nsa_compress",
    )(proj, pe, w)


def _cmp_attn_kernel(q_ref, kv_ref, gz_ref, ocmp_ref, selt_ref, cnt_ref, kz_ref, vct_ref, bias_ref, s_ref,
                     *, tq, nc, nslc, nsel, nb):
    t0 = pl.program_id(1) * tq

    @pl.when(pl.program_id(1) == 0)
    def _():
        for gp in range(NSA_KV_HEADS // 2):
            padded = _pad_halves(kv_ref[:, gp * LANES:(gp + 1) * LANES])
            for idx in range(4):
                kz_ref[gp * 4 + idx] = padded[idx].astype(BF16)
            vct_ref[gp] = kv_ref[:, NSA_KV_WIDTH + gp * LANES:NSA_KV_WIDTH + (gp + 1) * LANES].T.astype(BF16)
        end_n = (lax.broadcasted_iota(jnp.int32, (nc, tq), 0) * CMP_STRIDE + (CMP_LEN - 1)).astype(F32)
        for hh in range(NSA_HEADS):
            bias_ref[hh] = (SLOPES[hh] * LOG2E) * end_n

    q_t = (q_ref[...] * (DK ** -0.5 * LOG2E)).T.astype(BF16)
    for hh in range(NSA_HEADS):
        pair = hh // 2
        s_ref[hh] = _dot(kz_ref[(hh // HPG) * 2 + hh % 2], q_t[pair * LANES:(pair + 1) * LANES, :])

    n_row = lax.broadcasted_iota(jnp.int32, (nc, tq), 0)
    t_col = lax.broadcasted_iota(jnp.int32, (nc, tq), 1) + t0
    valid = t_col >= n_row * CMP_STRIDE + (CMP_LEN - 1)
    any_valid = jnp.where(t_col[0:1, :] >= CMP_LEN - 1, 1.0, 0.0)
    jj = lax.broadcasted_iota(jnp.int32, (nslc, nc), 0) * SLC_BLOCK
    nn = lax.broadcasted_iota(jnp.int32, (nslc, nc), 1) * CMP_STRIDE
    overlap_t = jnp.where((nn < jj + SLC_BLOCK) & (jj < nn + CMP_LEN), 1.0, 0.0)
    blk = lax.broadcasted_iota(jnp.int32, (nslc, tq), 0)
    t_blk = (lax.broadcasted_iota(jnp.int32, (nslc, tq), 1) + t0) >> SLC_SHIFT
    forced = (blk == 0) | (blk == t_blk) | (blk == t_blk - 1)
    future = blk > t_blk
    sub = lax.broadcasted_iota(jnp.int32, (8, tq), 0)
    ngrp = nslc // 8
    gate_t = _sigmoid(gz_ref[...].T)
    ones = jnp.ones((8, tq), BF16)

    outs = []
    for g in range(NSA_KV_HEADS):
        vct = vct_ref[g // 2, (g % 2) * DK:(g % 2 + 1) * DK, :]
        psum = jnp.zeros((nc, tq), F32)
        for h in range(HPG):
            hh = g * HPG + h
            s = jnp.where(valid, s_ref[hh] + bias_ref[hh], NEG_INF)
            e = jnp.exp2(s - jnp.max(s, axis=0, keepdims=True))
            p = e * (any_valid / jnp.sum(e, axis=0, keepdims=True))
            outs.append(_dot(vct, p.astype(BF16)) * gate_t[3 * hh:3 * hh + 1, :])
            psum = psum + p
        imp_t = _dot(overlap_t, psum, precision=lax.Precision.HIGHEST)
        score = jnp.where(future, -1.0, jnp.where(forced, 1e4, imp_t))

        sc = [score[8 * r:8 * r + 8] for r in range(ngrp)]
        rank = [jnp.zeros((8, tq), F32) for _ in range(ngrp)]
        for i in range(nslc):
            si = jnp.broadcast_to(score[i:i + 1, :], (8, tq))
            for r in range(ngrp):
                if r < i // 8:
                    before = si > sc[r]
                elif r > i // 8:
                    before = si >= sc[r]
                else:
                    before = (si > sc[r]) | ((sub > i % 8) & (si >= sc[r]))
                rank[r] = rank[r] + jnp.where(before, 1.0, 0.0)
        for r in range(ngrp):
            selt_ref[g, 8 * r:8 * r + 8, :] = jnp.where(rank[r] < float(nsel), 1.0, 0.0)
        if g % 2:
            picks = (selt_ref[g - 1] + selt_ref[g]).astype(BF16)
            per_block = jnp.where(_dot_nt(ones, picks) > 0.0, 1.0, 0.0).astype(BF16)
            blk_i = lax.broadcasted_iota(jnp.int32, (nslc, nslc // nb), 0)
            tile_i = lax.broadcasted_iota(jnp.int32, (nslc, nslc // nb), 1)
            pool = jnp.where(blk_i >> (nb.bit_length() - 1) == tile_i, 1.0, 0.0).astype(BF16)
            cnt_ref[g // 2:g // 2 + 1, :] = _dot(per_block, pool)[0:1, :].astype(jnp.int32)

    for pair, tile in enumerate(_untranspose_pairs(jnp.concatenate(outs, axis=1), tq)):
        ocmp_ref[:, pair * LANES:(pair + 1) * LANES] = tile


def _cmp_attn(proj, cmp_kv, bsz, seq):
    tq = min(Q_TILE, seq)
    nc = cmp_kv.shape[1]
    nslc = seq // SLC_BLOCK
    nsel = min(N_SELECT, nslc)
    nq = seq // tq
    return pl.pallas_call(
        functools.partial(_cmp_attn_kernel, tq=tq, nc=nc, nslc=nslc, nsel=nsel, nb=KEY_TILE // SLC_BLOCK),
        grid=(bsz, nq),
        in_specs=[pl.BlockSpec((tq, NSA_WIDTH), lambda b, i: (b * nq + i, 0)),
                  pl.BlockSpec((None, nc, 2 * NSA_KV_WIDTH), lambda b, i: (b, 0, 0)),
                  pl.BlockSpec((tq, LANES), lambda b, i: (b * nq + i, COL_GZ // LANES))],
        out_specs=[pl.BlockSpec((tq, NSA_WIDTH), lambda b, i: (b * nq + i, 0)),
                   pl.BlockSpec((NSA_KV_HEADS, None, nslc, tq), lambda b, i: (0, b, 0, i)),
                   pl.BlockSpec((None, None, NSA_KV_HEADS // 2, seq // KEY_TILE), lambda b, i: (b, i, 0, 0))],
        out_shape=[jax.ShapeDtypeStruct((bsz * seq, NSA_WIDTH), F32),
                   jax.ShapeDtypeStruct((NSA_KV_HEADS, bsz, nslc, seq), F32),
                   jax.ShapeDtypeStruct((bsz, nq, NSA_KV_HEADS // 2, seq // KEY_TILE), jnp.int32)],
        scratch_shapes=[pltpu.VMEM((2 * NSA_KV_HEADS, nc, LANES), BF16),
                        pltpu.VMEM((NSA_KV_HEADS // 2, LANES, nc), BF16),
                        pltpu.VMEM((NSA_HEADS, nc, tq), F32),
                        pltpu.VMEM((NSA_HEADS, nc, tq), F32)],
        compiler_params=_params(("parallel", "arbitrary")),
        name="nsa_cmp_attn",
    )(proj, cmp_kv, proj)


def _stage_kv(k_ref, v_ref, kz_ref, vt_ref, nk, ts):
    for c in range(nk):
        padded = _pad_halves(k_ref[c * ts:(c + 1) * ts, :])
        for idx in range(4):
            kz_ref[idx, c] = padded[idx].astype(BF16)
        v_t = v_ref[c * ts:(c + 1) * ts, :].T
        ones_row = jnp.where(lax.broadcasted_iota(jnp.int32, (VT_ROWS - DK, ts), 0) == 0, 1.0, 0.0)
        for gi in range(2):
            vt_ref[gi, c, 0:DK, :] = v_t[gi * DK:(gi + 1) * DK, :].astype(BF16)
            vt_ref[gi, c, DK:VT_ROWS, :] = ones_row.astype(BF16)


def _attend(mode, nsub, gp, t0, slopes_ref, cnt_ref, selt_ref, qb_ref, bias_ref, kz_ref, vt_ref, s_ref,
            m_ref, acc_ref, idx_ref, *, tq, ts, nk):
    nb = ts // SLC_BLOCK
    ts_shift = ts.bit_length() - 1
    m_ref[...] = jnp.full(m_ref.shape, NEG_INF, F32)
    acc_ref[...] = jnp.zeros(acc_ref.shape, F32)

    last_sub = (t0 + tq - 1) >> ts_shift
    if mode == "sel":
        def build(j, n):
            idx_ref[n] = j
            return n + jnp.where(((cnt_ref[gp, j] > 0) | (j == last_sub)) & (j <= last_sub), 1, 0)

        n_live = lax.fori_loop(0, nk, build, 0, unroll=4)
    else:
        first_sub = jnp.maximum(t0 - (WINDOW - 1), 0) >> ts_shift
        n_live = last_sub + 1 - first_sub
        for u in range(nsub):
            idx_ref[u] = jnp.minimum(first_sub + u, last_sub)

    key_i = lax.broadcasted_iota(jnp.int32, (ts, tq), 0)
    qry_t = lax.broadcasted_iota(jnp.int32, (ts, tq), 1) + t0

    def sub_tile(step, u):
        pos = step * nsub + u
        return idx_ref[jnp.minimum(pos, n_live - 1)], pos < n_live

    def scores(step):
        for u in range(nsub):
            kj, _ = sub_tile(step, u)
            for gi in range(2):
                for h in range(HPG):
                    pair = gi * (HPG // 2) + h // 2
                    s_ref[u * 2 + gi, :, h * tq:(h + 1) * tq] = _dot(
                        kz_ref[gi * 2 + h % 2, kj], qb_ref[pair * LANES:(pair + 1) * LANES, :])

    def softmax_update(step):
        for u in range(nsub):
            kj, live = sub_tile(step, u)
            j0 = kj * ts
            rel0 = (jnp.zeros((1, tq), jnp.int32) + (j0 - t0)).astype(F32)
            dd = qry_t - (key_i + j0)
            lowest = jnp.where(live, 0, 1 << 30)
            for gi in range(2):
                if mode == "sel":
                    chosen = selt_ref[gi, pl.ds(kj * nb + nb - 1, 1), :]
                    for c in range(nb - 2, -1, -1):
                        chosen = jnp.where(key_i < (c + 1) * SLC_BLOCK,
                                           selt_ref[gi, pl.ds(kj * nb + c, 1), :], chosen)
                    mask = (chosen > 0.5) & (dd >= lowest)
                else:
                    mask = (dd >= lowest) & (dd < WINDOW)
                ps = []
                alphas = []
                for h in range(HPG):
                    shift = (slopes_ref[(gp * 2 + gi) * HPG + h] * LOG2E) * rel0
                    cols = slice(h * tq, (h + 1) * tq)
                    x = jnp.where(mask, s_ref[u * 2 + gi, :, cols] + bias_ref[gi * HPG + h], MASKED)
                    m_old = m_ref[gi, :, cols]
                    m_new = jnp.maximum(m_old, jnp.max(x, axis=0, keepdims=True) + shift)
                    p = jnp.exp2(x - (m_new - shift))
                    m_ref[gi, :, cols] = m_new
                    ps.append(p.astype(BF16))
                    alphas.append(jnp.exp2(m_old - m_new))
                pv = _dot(vt_ref[gi, kj], jnp.concatenate(ps, axis=1))
                acc_ref[gi] = acc_ref[gi] * jnp.concatenate(alphas, axis=1) + pv

    def body(step, carry):
        scores(step)
        softmax_update(step)
        return carry

    if mode == "sel":
        assert nsub & (nsub - 1) == 0
        lax.fori_loop(0, (n_live + nsub - 1) >> (nsub.bit_length() - 1), body, 0)
    else:
        body(0, 0)


def _flash_kernel(slopes_ref, cnt_ref, q_ref, ks_ref, vs_ref, kw_ref, vw_ref, gz_ref, selt_ref, o_ref,
                  kzs_ref, vts_ref, kzw_ref, vtw_ref, qb_ref, bias_ref, gate_ref, s_ref, m_ref, acc_ref, idx_ref,
                  *, tq, ts, nk, nsub_sel, nsub_win):
    gp = pl.program_id(1)
    qi = pl.program_id(2)
    t0 = qi * tq

    @pl.when(qi == 0)
    def _():
        _stage_kv(ks_ref, vs_ref, kzs_ref, vts_ref, nk, ts)
        _stage_kv(kw_ref, vw_ref, kzw_ref, vtw_ref, nk, ts)

    key_f = lax.broadcasted_iota(jnp.int32, (ts, tq), 0).astype(F32)
    qb_ref[...] = (q_ref[...] * (DK ** -0.5 * LOG2E)).T.astype(BF16)
    gate_ref[...] = _sigmoid(gz_ref[...].T)
    for gh in range(2 * HPG):
        bias_ref[gh] = (slopes_ref[gp * 2 * HPG + gh] * LOG2E) * key_f

    common = dict(tq=tq, ts=ts, nk=nk)
    _attend("sel", nsub_sel, gp, t0, slopes_ref, cnt_ref, selt_ref, qb_ref, bias_ref, kzs_ref, vts_ref, s_ref,
            m_ref.at[0], acc_ref.at[0], idx_ref, **common)
    _attend("win", nsub_win, gp, t0, slopes_ref, None, None, qb_ref, bias_ref, kzw_ref, vtw_ref, s_ref,
            m_ref.at[1], acc_ref.at[1], idx_ref, **common)

    outs = []
    for gi in range(2):
        branch_out = [acc_ref[br, gi, 0:DK, :] / acc_ref[br, gi, DK:DK + 1, :] for br in range(2)]
        for h in range(HPG):
            row = (gp * 2 * HPG + gi * HPG + h) * 3
            cols = slice(h * tq, (h + 1) * tq)
            outs.append(branch_out[0][:, cols] * gate_ref[pl.ds(row + 1, 1), :]
                        + branch_out[1][:, cols] * gate_ref[pl.ds(row + 2, 1), :])
    for pair, tile in enumerate(_untranspose_pairs(jnp.concatenate(outs, axis=1), tq)):
        o_ref[:, pair * LANES:(pair + 1) * LANES] = tile


def _flash(proj, selt, cnt, slopes, bsz, seq):
    tq = min(Q_TILE, seq)
    ts = KEY_TILE
    nsub_sel = 4
    nsub_win = (WINDOW + tq) // ts
    nq = seq // tq
    nk = seq // ts
    nslc = seq // SLC_BLOCK
    first_kv = (COL_KV + 2 * NSA_KV_WIDTH) // LANES
    kv_spec = lambda which: pl.BlockSpec((seq, LANES), lambda b, g, i: (b, first_kv + 2 * which + g))
    smem = pltpu.SMEM
    max_units = 2 * max(nsub_sel, nsub_win)
    return pl.pallas_call(
        functools.partial(_flash_kernel, tq=tq, ts=ts, nk=nk, nsub_sel=nsub_sel, nsub_win=nsub_win),
        grid=(bsz, 2, nq),
        in_specs=[pl.BlockSpec(memory_space=smem),
                  pl.BlockSpec((None, None, 2, nk), lambda b, g, i: (b, i, 0, 0), memory_space=smem),
                  pl.BlockSpec((tq, 2 * HPG * DK), lambda b, g, i: (b * nq + i, g)),
                  kv_spec(0), kv_spec(1), kv_spec(2), kv_spec(3),
                  pl.BlockSpec((tq, LANES), lambda b, g, i: (b * nq + i, COL_GZ // LANES)),
                  pl.BlockSpec((2, None, nslc, tq), lambda b, g, i: (g, b, 0, i))],
        out_specs=pl.BlockSpec((tq, 2 * HPG * DK), lambda b, g, i: (b * nq + i, g)),
        out_shape=jax.ShapeDtypeStruct((bsz * seq, NSA_WIDTH), F32),
        scratch_shapes=[pltpu.VMEM((4, nk, ts, LANES), BF16),
                        pltpu.VMEM((2, nk, VT_ROWS, ts), BF16),
                        pltpu.VMEM((4, nk, ts, LANES), BF16),
                        pltpu.VMEM((2, nk, VT_ROWS, ts), BF16),
                        pltpu.VMEM((2 * HPG * DK, tq), BF16),
                        pltpu.VMEM((2 * HPG, ts, tq), F32),
                        pltpu.VMEM((LANES, tq), F32),
                        pltpu.VMEM((max_units, ts, HPG * tq), F32),
                        pltpu.VMEM((2, 2, 1, HPG * tq), F32),
                        pltpu.VMEM((2, 2, VT_ROWS, HPG * tq), F32),
                        pltpu.SMEM((nk + max_units,), jnp.int32)],
        compiler_params=_params(("parallel", "parallel", "arbitrary")),
        name="nsa_sel_win",
    )(slopes, cnt, proj, proj, proj, proj, proj, proj, selt)


def _lru_kernel(x_ref, y_ref, cw_ref, cb_ref, wa_ref, ba_ref, wi_ref, bi_ref, lam_ref, o_ref,
                tail_ref, h_ref, *, tt):
    @pl.when(pl.program_id(1) == 0)
    def _():
        tail_ref[...] = jnp.zeros_like(tail_ref)
        h_ref[...] = jnp.zeros_like(h_ref)

    x = x_ref[...]
    ext = jnp.concatenate([tail_ref[...], x], axis=0)
    xc = cb_ref[...] + x * cw_ref[CONV_WIDTH - 1:CONV_WIDTH, :]
    for k in range(CONV_WIDTH - 1):
        back = CONV_WIDTH - 1 - k
        xc = xc + ext[8 - back:8 - back + tt, :] * cw_ref[k:k + 1, :]
    tail_ref[...] = x[tt - 8:, :]

    xcb = xc.astype(BF16)
    r = _sigmoid(_dot(xcb, wa_ref[...]) + ba_ref[...])
    gate_i = _sigmoid(_dot(xcb, wi_ref[...]) + bi_ref[...])
    neg_lam = -lam_ref[...]
    softplus = jnp.maximum(neg_lam, 0.0) + jnp.log1p(jnp.exp(-jnp.abs(neg_lam)))
    log_a = -LRU_C * r * softplus
    a = jnp.exp(log_a)
    u = jnp.sqrt(jnp.tanh(-log_a) * (a * a + 1.0)) * (gate_i * xc)

    rows = lax.broadcasted_iota(jnp.int32, (tt, 1), 0)
    step = 1
    while step < tt:
        keep = rows >= step
        a_prev = jnp.where(keep, pltpu.roll(a, step, 0), 1.0)
        u_prev = jnp.where(keep, pltpu.roll(u, step, 0), 0.0)
        u = u + a * u_prev
        a = a * a_prev
        step *= 2
    hs = u + a * h_ref[...]
    h_ref[...] = hs[tt - 1:tt, :]
    o_ref[...] = hs * jax.nn.gelu(y_ref[...], approximate=True)


def _lru(proj, cw, cb, wa, ba, wi, bi, lam, bsz, seq):
    tt = min(512, seq)
    nt = seq // tt
    w = LRU_WIDTH
    xcol = COL_LRU_X // w
    ycol = COL_LRU_Y // w
    vec = pl.BlockSpec((1, w), lambda b, i: (0, 0))
    mat = pl.BlockSpec((w, w), lambda b, i: (0, 0))
    return pl.pallas_call(
        functools.partial(_lru_kernel, tt=tt),
        grid=(bsz, nt),
        in_specs=[pl.BlockSpec((tt, w), lambda b, i: (b * nt + i, xcol)),
                  pl.BlockSpec((tt, w), lambda b, i: (b * nt + i, ycol)),
                  pl.BlockSpec((CONV_WIDTH, w), lambda b, i: (0, 0)),
                  vec, mat, vec, mat, vec, vec],
        out_specs=pl.BlockSpec((tt, w), lambda b, i: (b * nt + i, 0)),
        out_shape=jax.ShapeDtypeStruct((bsz * seq, w), F32),
        scratch_shapes=[pltpu.VMEM((8, w), F32), pltpu.VMEM((1, w), F32)],
        compiler_params=_params(("parallel", "arbitrary")),
        name="rglru",
    )(proj, proj, cw, cb, wa, ba, wi, bi, lam)


def _gla_kernel(q_ref, k_ref, v_ref, r_ref, gz_ref, w2_ref, bg_ref, ng_ref, o_ref, st_ref, *, tt):
    c = GLA_CHUNK
    dh = GLA_HEAD_DIM

    @pl.when(pl.program_id(1) == 0)
    def _():
        st_ref[...] = jnp.zeros_like(st_ref)

    gate = _dot(gz_ref[...].astype(BF16), w2_ref[...]) + bg_ref[...]
    log_alpha = (jnp.minimum(gate, 0.0) - jnp.log1p(jnp.exp(-jnp.abs(gate)))) / GLA_GATE_TAU
    ri = lax.broadcasted_iota(jnp.int32, (tt, tt), 0)
    ci = lax.broadcasted_iota(jnp.int32, (tt, tt), 1)
    tri = jnp.where((ri >> CHUNK_SHIFT == ci >> CHUNK_SHIFT) & (ci <= ri), 1.0, 0.0)
    bcum_all = _dot(tri, log_alpha, precision=lax.Precision.HIGHEST)
    causal = (lax.broadcasted_iota(jnp.int32, (c, c), 1) <= lax.broadcasted_iota(jnp.int32, (c, c), 0))

    for n in range(tt // c):
        rs = slice(n * c, (n + 1) * c)
        bcum = bcum_all[rs]
        b_last = bcum[c - 1:c, :]
        e_pos = jnp.exp(bcum)
        q_t = (q_ref[rs, :] * dh ** -0.5) * e_pos
        kk = k_ref[rs, :]
        k_t = kk * jnp.exp(-bcum)
        k_end = kk * jnp.exp(b_last - bcum)
        decay = jnp.exp(b_last)
        vv = v_ref[rs, :]
        for hh in range(GLA_HEADS):
            cs = slice(hh * dh, (hh + 1) * dh)
            qh = q_t[:, cs].astype(BF16)
            vh = vv[:, cs]
            att = jnp.where(causal, _dot_nt(qh, k_t[:, cs].astype(BF16)), 0.0)
            st = st_ref[hh]
            o = _dot(att.astype(BF16), vh.astype(BF16)) + _dot_nt(qh, st.astype(BF16))
            st_ref[hh] = st * decay[:, cs] + _dot(vh.T.astype(BF16), k_end[:, cs].astype(BF16))
            o = _rms_scale(o) * ng_ref[:, cs]
            rr = r_ref[rs, cs]
            o_ref[rs, cs] = o * (rr * _sigmoid(rr))


def _gla(proj, w2pad, bg, ng, bsz, seq):
    tt = min(256, seq)
    nt = seq // tt
    w = GLA_WIDTH
    cols = [COL_GLA_Q // w, COL_GLA_K // w, COL_GLA_V // w, COL_GLA_R // w]
    seg = [pl.BlockSpec((tt, w), functools.partial(lambda b, i, cc: (b * nt + i, cc), cc=cc)) for cc in cols]
    vec = pl.BlockSpec((1, w), lambda b, i: (0, 0))
    return pl.pallas_call(
        functools.partial(_gla_kernel, tt=tt),
        grid=(bsz, nt),
        in_specs=seg + [pl.BlockSpec((tt, LANES), lambda b, i: (b * nt + i, COL_GZ // LANES)),
                        pl.BlockSpec((LANES, w), lambda b, i: (0, 0)), vec, vec],
        out_specs=pl.BlockSpec((tt, w), lambda b, i: (b * nt + i, 0)),
        out_shape=jax.ShapeDtypeStruct((bsz * seq, w), F32),
        scratch_shapes=[pltpu.VMEM((GLA_HEADS, GLA_HEAD_DIM, GLA_HEAD_DIM), F32)],
        compiler_params=_params(("parallel", "arbitrary")),
        name="gla",
    )(proj, proj, proj, proj, proj, w2pad, bg, ng)


def _out_proj_kernel(ocmp_ref, oselwin_ref, olru_ref, ogla_ref, h_ref, w_ref, g_ref, o_ref, wb_ref):
    @pl.when(pl.program_id(0) == 0)
    def _():
        wb_ref[...] = w_ref[...].astype(BF16)

    nsa = ocmp_ref[...] + oselwin_ref[...]
    y = _dot(nsa.astype(BF16), wb_ref[0:NSA_WIDTH, :])
    y = y + _dot(olru_ref[...].astype(BF16), wb_ref[NSA_WIDTH:NSA_WIDTH + LRU_WIDTH, :])
    y = y + _dot(ogla_ref[...].astype(BF16), wb_ref[NSA_WIDTH + LRU_WIDTH:, :])
    o_ref[...] = h_ref[...] + _rms_scale(y) * g_ref[...]


def _resident(shape, layer):
    zeros = (0,) * len(shape)
    return pl.BlockSpec((None,) + shape, lambda i: (layer,) + zeros, pipeline_mode=pl.Buffered(1))


def _out_proj(ocmp, oselwin, olru, ogla, h2d, w_out, g, layer):
    m = h2d.shape[0]
    tm = min(256, m)
    row = lambda width: pl.BlockSpec((tm, width), lambda i: (i, 0))
    return pl.pallas_call(
        _out_proj_kernel,
        grid=(m // tm,),
        in_specs=[row(NSA_WIDTH), row(NSA_WIDTH), row(LRU_WIDTH), row(GLA_WIDTH), row(D_MODEL),
                  _resident((D_MODEL, D_MODEL), layer),
                  pl.BlockSpec((1, D_MODEL), lambda i: (0, 0))],
        out_specs=row(D_MODEL),
        out_shape=jax.ShapeDtypeStruct((m, D_MODEL), F32),
        scratch_shapes=[pltpu.VMEM((D_MODEL, D_MODEL), BF16)],
        compiler_params=_params(("arbitrary",)),
        name="out_proj",
    )(ocmp, oselwin, olru, ogla, h2d, w_out, g)


def _mlp_kernel(h_ref, gpre_ref, wu_ref, wd_ref, gpost_ref, o_ref, un_ref):
    f = pl.program_id(1)

    @pl.when(f == 0)
    def _():
        un_ref[...] = (_rms_scale(h_ref[...]) * gpre_ref[...]).astype(BF16)
        o_ref[...] = jnp.zeros_like(o_ref)

    a = jnp.maximum(_dot(un_ref[...], wu_ref[...].astype(BF16)), 0.0)
    o_ref[...] += _dot((a * a).astype(BF16), wd_ref[...].astype(BF16))

    @pl.when(f == pl.num_programs(1) - 1)
    def _():
        o_ref[...] = h_ref[...] + _rms_scale(o_ref[...]) * gpost_ref[...]


def _mlp(h2d, gpre, w_up, w_down, gpost, layer):
    m = h2d.shape[0]
    tm = min(1024, m)
    tf = 512
    vec = pl.BlockSpec((1, D_MODEL), lambda i, f: (0, 0))
    return pl.pallas_call(
        _mlp_kernel,
        grid=(m // tm, D_FF // tf),
        in_specs=[pl.BlockSpec((tm, D_MODEL), lambda i, f: (i, 0)), vec,
                  pl.BlockSpec((None, D_MODEL, tf), lambda i, f: (layer, 0, f)),
                  pl.BlockSpec((None, tf, D_MODEL), lambda i, f: (layer, f, 0)), vec],
        out_specs=pl.BlockSpec((tm, D_MODEL), lambda i, f: (i, 0), pipeline_mode=pl.Buffered(1)),
        out_shape=jax.ShapeDtypeStruct((m, D_MODEL), F32),
        scratch_shapes=[pltpu.VMEM((tm, D_MODEL), BF16)],
        compiler_params=_params(("parallel", "arbitrary")),
        name="mlp",
    )(h2d, gpre, w_up, w_down, gpost)


def _ple_kernel(h_ref, p_ref, wg_ref, wp_ref, o_ref, wgb_ref, wpb_ref):
    @pl.when(pl.program_id(0) == 0)
    def _():
        wgb_ref[...] = wg_ref[...].astype(BF16)
        wpb_ref[...] = wp_ref[...].astype(BF16)

    h = h_ref[...]
    gate = _sigmoid(_dot(h.astype(BF16), wgb_ref[...]))
    o_ref[...] = h + gate * _dot(p_ref[...].astype(BF16), wpb_ref[...])


def _ple(h2d, p_all, w_gate, w_ple, layer):
    m = h2d.shape[0]
    tm = min(512, m)
    return pl.pallas_call(
        _ple_kernel,
        grid=(m // tm,),
        in_specs=[pl.BlockSpec((tm, D_MODEL), lambda i: (i, 0)),
                  pl.BlockSpec((None, tm, PLE_DIM), lambda i: (layer, i, 0)),
                  _resident((D_MODEL, D_MODEL), layer),
                  _resident((PLE_DIM, D_MODEL), layer)],
        out_specs=pl.BlockSpec((tm, D_MODEL), lambda i: (i, 0)),
        out_shape=jax.ShapeDtypeStruct((m, D_MODEL), F32),
        scratch_shapes=[pltpu.VMEM((D_MODEL, D_MODEL), BF16), pltpu.VMEM((PLE_DIM, D_MODEL), BF16)],
        compiler_params=_params(("arbitrary",)),
        name="ple",
    )(h2d, p_all, w_gate, w_ple)


def _w_in_views(w_in):
    w_t = jnp.swapaxes(w_in, 1, 2)
    g0 = NSA_WIDTH + 6 * NSA_KV_WIDTH
    z0 = D_IN - GLA_GATE_RANK
    w_gz = jnp.concatenate([w_t[:, g0:g0 + N_GATES], w_t[:, z0:]], axis=1)
    return w_t, w_gz


def _compress_weights(cmp_w, cmp_pe):
    w4 = cmp_w.reshape(2, CMP_LEN, DK, DK)
    eye = jnp.eye(2, dtype=cmp_w.dtype)
    wbd = jnp.einsum("clde,xy->clxdye", w4, eye).reshape(2, CMP_LEN, LANES, LANES).astype(BF16)
    pe = jnp.tile(cmp_pe, (1, 1, 2)).reshape(2, CMP_LEN, 1, LANES)
    return pe, wbd


def _block_diag(w):
    eye = jnp.eye(LRU_BLOCKS, dtype=w.dtype)
    return jnp.einsum("ncd,nm->ncmd", w, eye).reshape(LRU_WIDTH, LRU_WIDTH).astype(BF16)


def _layer(h2d, bsz, seq, layer, stacked, norm_mix_pre, nsa_cmp_w, nsa_cmp_pe, lru_conv_w, lru_conv_b,
           lru_wa, lru_ba, lru_wi, lru_bi, lru_lambda, gla_w_gate2, gla_b_gate, gla_norm, norm_mix_post,
           norm_mlp_pre, norm_mlp_post):
    w_in_all, w_out_all, w_up_all, w_down_all, w_ple_gate_all, w_ple_all, p_all = stacked
    row = lambda v: v.reshape(1, -1)
    proj = _in_proj(h2d, row(norm_mix_pre), *w_in_all, layer)

    cmp_kv = _compress(proj, *_compress_weights(nsa_cmp_w, nsa_cmp_pe), bsz, seq)
    o_cmp, selt, cnt = _cmp_attn(proj, cmp_kv, bsz, seq)
    slopes = jnp.asarray(SLOPES, F32)
    o_sel_win = _flash(proj, selt, cnt, slopes, bsz, seq)

    o_lru = _lru(proj, lru_conv_w, row(lru_conv_b), _block_diag(lru_wa), row(lru_ba),
                 _block_diag(lru_wi), row(lru_bi), row(lru_lambda), bsz, seq)

    w2pad = jnp.zeros((LANES, GLA_WIDTH), F32).at[N_GATES:N_GATES + GLA_GATE_RANK].set(gla_w_gate2)
    o_gla = _gla(proj, w2pad.astype(BF16), row(gla_b_gate), row(gla_norm), bsz, seq)

    h2d = _out_proj(o_cmp, o_sel_win, o_lru, o_gla, h2d, w_out_all, row(norm_mix_post), layer)
    h2d = _mlp(h2d, row(norm_mlp_pre), w_up_all, w_down_all, row(norm_mlp_post), layer)
    return _ple(h2d, p_all, w_ple_gate_all, w_ple_all, layer)


def kernel(x, p, norm_mix_pre, w_in, nsa_cmp_w, nsa_cmp_pe, lru_conv_w, lru_conv_b, lru_wa, lru_ba,
           lru_wi, lru_bi, lru_lambda, gla_w_gate2, gla_b_gate, gla_norm, w_out, norm_mix_post,
           norm_mlp_pre, w_up, w_down, norm_mlp_post, w_ple_gate, w_ple):
    bsz, seq, _ = x.shape
    h2d = x.reshape(bsz * seq, D_MODEL)
    small = (norm_mix_pre, nsa_cmp_w, nsa_cmp_pe, lru_conv_w, lru_conv_b, lru_wa, lru_ba, lru_wi, lru_bi,
             lru_lambda, gla_w_gate2, gla_b_gate, gla_norm, norm_mix_post, norm_mlp_pre, norm_mlp_post)
    depth = p.shape[0]
    stacked = (_w_in_views(w_in), w_out, w_up, w_down, w_ple_gate, w_ple,
               p.reshape(depth, bsz * seq, PLE_DIM))
    for i in range(depth):
        h2d = _layer(h2d, bsz, seq, i, stacked, *(w[i] for w in small))
    return h2d.reshape(bsz, seq, D_MODEL)
```

```python
import functools

import jax
import jax.numpy as jnp
from jax import lax
from jax.experimental import pallas as pl
from jax.experimental.pallas import tpu as pltpu

F32 = jnp.float32
BF16 = jnp.bfloat16

D_MODEL = 2048
PLE_DIM = 256
NSA_HEADS = 16
NSA_KV_HEADS = 4
HPG = NSA_HEADS // NSA_KV_HEADS
NSA_WIDTH = 1024
DK = 64
NSA_KV_WIDTH = NSA_KV_HEADS * DK
CMP_LEN = 32
CMP_STRIDE = 16
SLC_BLOCK = 64
SLC_SHIFT = 6
N_SELECT = 16
WINDOW = 512
LRU_WIDTH = 512
LRU_BLOCKS = 8
LRU_BLOCK_DIM = 64
CONV_WIDTH = 4
LRU_C = 8.0
LRU_SCAN_ROWS = 64
GLA_WIDTH = 512
GLA_HEADS = 4
GLA_HEAD_DIM = 128
GLA_GATE_RANK = 16
GLA_GATE_TAU = 16.0
GLA_CHUNK = 64
D_FF = 4 * D_MODEL
EPS = 1e-6
NEG_INF = -1e30
MASKED = 2.0 * NEG_INF

COL_Q = 0
COL_KV = 1024
COL_LRU_X = 2560
COL_LRU_Y = 3072
COL_GLA_Q = 3584
COL_GLA_K = 4096
COL_GLA_V = 4608
COL_GLA_R = 5120
COL_GZ = 5632
D_IN_PAD = 6144
N_GATES = 3 * NSA_HEADS
LANES = 128
KEY_TILE = 128
Q_TILE = 256
VT_ROWS = DK + 16
LOG2E = 1.4426950408889634

SLOPES = tuple(2.0 ** (-8.0 * i / NSA_HEADS) for i in range(1, NSA_HEADS + 1))

VMEM_LIMIT = 56 * 1024 * 1024

NT_DIMS = (((1,), (1,)), ((), ()))
TN_DIMS = (((0,), (0,)), ((), ()))


def _params(sem):
    return pltpu.CompilerParams(dimension_semantics=sem, vmem_limit_bytes=VMEM_LIMIT)


def _dot(a, b, **kw):
    return jnp.dot(a, b, preferred_element_type=F32, **kw)


def _dot_nt(a, b, **kw):
    return lax.dot_general(a, b, NT_DIMS, preferred_element_type=F32, **kw)


def _dot_tn(a, b, **kw):
    return lax.dot_general(a, b, TN_DIMS, preferred_element_type=F32, **kw)


def _sigmoid(x):
    return 1.0 / (1.0 + jnp.exp(-x))


def _rms_scale(x):
    return x * lax.rsqrt(jnp.mean(x * x, axis=-1, keepdims=True) + EPS)


def _half_masks(rows):
    lane = lax.broadcasted_iota(jnp.int32, (rows, LANES), 1)
    return lane < DK, lane >= DK


def _pad_halves(blk):
    lo, hi = _half_masks(blk.shape[0])
    swapped = pltpu.roll(blk, DK, 1)
    zero = jnp.zeros_like(blk)
    return (jnp.where(lo, blk, zero), jnp.where(hi, swapped, zero),
            jnp.where(lo, swapped, zero), jnp.where(hi, blk, zero))


def _untranspose_pairs(o_t, tq):
    sub = lax.broadcasted_iota(jnp.int32, (4 * DK, LANES), 0)
    lane = lax.broadcasted_iota(jnp.int32, (4 * DK, LANES), 1)
    place = jnp.where(lane == (sub & (DK - 1)) + jnp.where(sub >= 2 * DK, DK, 0), 1.0, 0.0).astype(BF16)
    out = []
    for pair in range(o_t.shape[1] // (2 * tq)):
        parts = []
        for half in range(2):
            v = o_t[:, (2 * pair + half) * tq:(2 * pair + half + 1) * tq]
            v_hi = v.astype(BF16)
            parts += [v_hi, (v - v_hi.astype(F32)).astype(BF16)]
        out.append(_dot_tn(jnp.concatenate(parts, axis=0), place))
    return out


IN_TILE = 512
HEAD_TILES = (NSA_WIDTH + 6 * NSA_KV_WIDTH) // IN_TILE
TAIL_TILES = (2 * LRU_WIDTH + 4 * GLA_WIDTH) // IN_TILE
D_IN = NSA_WIDTH + 6 * NSA_KV_WIDTH + N_GATES + 2 * LRU_WIDTH + 4 * GLA_WIDTH + GLA_GATE_RANK


def _in_proj_kernel(x_ref, g_ref, w_ref, wgz_ref, o_ref, xn_ref, wres_ref):
    i = pl.program_id(0)
    j = pl.program_id(1)
    last = HEAD_TILES + TAIL_TILES
    n_gz = N_GATES + GLA_GATE_RANK
    k = wres_ref.shape[1]

    @pl.when(j == 0)
    def _():
        xn_ref[...] = (_rms_scale(x_ref[...]) * g_ref[...]).astype(BF16)

    @pl.when((i == 0) & (j < last))
    def _():
        wres_ref[j] = w_ref[0].T.astype(BF16)

    @pl.when((i == 0) & (j == last))
    def _():
        head = jnp.concatenate([wgz_ref[...], jnp.zeros((LANES - n_gz, k), F32)], axis=0)
        wres_ref[last, :, 0:LANES] = head.T.astype(BF16)
        wres_ref[last, :, LANES:] = jnp.zeros((k, IN_TILE - LANES), BF16)

    o_ref[...] = _dot(xn_ref[...], wres_ref[j])


def _in_proj(h2d, g, w_t, w_gz, layer):
    m, k = h2d.shape
    tm = min(1024, m)
    n_tiles = HEAD_TILES + TAIL_TILES + 1
    assert n_tiles * IN_TILE == D_IN_PAD

    def w_rows(i, j):
        start = jnp.where(j < HEAD_TILES, j * IN_TILE, j * IN_TILE + N_GATES)
        start = jnp.where(i == 0, jnp.minimum(start, D_IN - IN_TILE), D_IN - IN_TILE)
        return layer, pl.multiple_of(start, 8), 0

    return pl.pallas_call(
        _in_proj_kernel,
        grid=(m // tm, n_tiles),
        in_specs=[pl.BlockSpec((tm, k), lambda i, j: (i, 0), pipeline_mode=pl.Buffered(1)),
                  pl.BlockSpec((1, k), lambda i, j: (0, 0)),
                  pl.BlockSpec((pl.Element(1), pl.Element(IN_TILE), pl.Element(k)), w_rows),
                  pl.BlockSpec((None, N_GATES + GLA_GATE_RANK, k), lambda i, j: (layer, 0, 0))],
        out_specs=pl.BlockSpec((tm, IN_TILE), lambda i, j: (i, j)),
        out_shape=jax.ShapeDtypeStruct((m, D_IN_PAD), F32),
        scratch_shapes=[pltpu.VMEM((tm, k), BF16), pltpu.VMEM((n_tiles, k, IN_TILE), BF16)],
        compiler_params=_params(("arbitrary", "arbitrary")),
        name="in_proj",
    )(h2d, g, w_t, w_gz)


def _compress_kernel(x_ref, pe_ref, w_ref, o_ref, *, nc):
    first = jnp.zeros((nc, LANES), F32)
    second = jnp.zeros((nc, LANES), F32)
    for l in range(CMP_STRIDE):
        x = x_ref[pl.ds(l, nc, stride=CMP_STRIDE), :]
        first = first + _dot((x + pe_ref[l]).astype(BF16), w_ref[l])
        second = second + _dot((x + pe_ref[CMP_STRIDE + l]).astype(BF16), w_ref[CMP_STRIDE + l])
    o_ref[...] = first + pltpu.roll(second, nc - 1, 0)


def _compress(proj, pe, w, bsz, seq):
    nc = seq // CMP_STRIDE
    return pl.pallas_call(
        functools.partial(_compress_kernel, nc=nc),
        grid=(bsz, NSA_KV_HEADS),
        in_specs=[pl.BlockSpec((seq, LANES), lambda i, j: (i, COL_KV // LANES + j)),
                  pl.BlockSpec((None, CMP_LEN, 1, LANES), lambda i, j: (j // 2, 0, 0, 0)),
                  pl.BlockSpec((None, CMP_LEN, LANES, LANES), lambda i, j: (j // 2, 0, 0, 0))],
        out_specs=pl.BlockSpec((None, nc, LANES), lambda i, j: (i, 0, j)),
        out_shape=jax.ShapeDtypeStruct((bsz, nc, 2 * NSA_KV_WIDTH), F32),
        compiler_params=_params(("parallel", "parallel")),
        name="nsa_compress",
    )(proj, pe, w)


def _cmp_attn_kernel(q_ref, kv_ref, gz_ref, ocmp_ref, selt_ref, cnt_ref, kz_ref, vct_ref, bias_ref, s_ref,
                     *, tq, nc, nslc, nsel, nb):
    t0 = pl.program_id(1) * tq

    @pl.when(pl.program_id(1) == 0)
    def _():
        for gp in range(NSA_KV_HEADS // 2):
            padded = _pad_halves(kv_ref[:, gp * LANES:(gp + 1) * LANES])
            for idx in range(4):
                kz_ref[gp * 4 + idx] = padded[idx].astype(BF16)
            vct_ref[gp] = kv_ref[:, NSA_KV_WIDTH + gp * LANES:NSA_KV_WIDTH + (gp + 1) * LANES].T.astype(BF16)
        end_n = (lax.broadcasted_iota(jnp.int32, (nc, tq), 0) * CMP_STRIDE + (CMP_LEN - 1)).astype(F32)
        for hh in range(NSA_HEADS):
            bias_ref[hh] = (SLOPES[hh] * LOG2E) * end_n

    q_t = (q_ref[...] * (DK ** -0.5 * LOG2E)).T.astype(BF16)
    for hh in range(NSA_HEADS):
        pair = hh // 2
        s_ref[hh] = _dot(kz_ref[(hh // HPG) * 2 + hh % 2], q_t[pair * LANES:(pair + 1) * LANES, :])

    n_row = lax.broadcasted_iota(jnp.int32, (nc, tq), 0)
    t_col = lax.broadcasted_iota(jnp.int32, (nc, tq), 1) + t0
    valid = t_col >= n_row * CMP_STRIDE + (CMP_LEN - 1)
    any_valid = jnp.where(t_col[0:1, :] >= CMP_LEN - 1, 1.0, 0.0)
    jj = lax.broadcasted_iota(jnp.int32, (nslc, nc), 0) * SLC_BLOCK
    nn = lax.broadcasted_iota(jnp.int32, (nslc, nc), 1) * CMP_STRIDE
    overlap_t = jnp.where((nn < jj + SLC_BLOCK) & (jj < nn + CMP_LEN), 1.0, 0.0)
    blk = lax.broadcasted_iota(jnp.int32, (nslc, tq), 0)
    t_blk = (lax.broadcasted_iota(jnp.int32, (nslc, tq), 1) + t0) >> SLC_SHIFT
    forced = (blk == 0) | (blk == t_blk) | (blk == t_blk - 1)
    future = blk > t_blk
    sub = lax.broadcasted_iota(jnp.int32, (8, tq), 0)
    ngrp = nslc // 8
    gate_t = _sigmoid(gz_ref[...].T)
    ones = jnp.ones((8, tq), BF16)

    outs = []
    for g in range(NSA_KV_HEADS):
        vct = vct_ref[g // 2, (g % 2) * DK:(g % 2 + 1) * DK, :]
        psum = jnp.zeros((nc, tq), F32)
        for h in range(HPG):
            hh = g * HPG + h
            s = jnp.where(valid, s_ref[hh] + bias_ref[hh], NEG_INF)
            e = jnp.exp2(s - jnp.max(s, axis=0, keepdims=True))
            p = e * (any_valid / jnp.sum(e, axis=0, keepdims=True))
            outs.append(_dot(vct, p.astype(BF16)) * gate_t[3 * hh:3 * hh + 1, :])
            psum = psum + p
        imp_t = _dot(overlap_t, psum, precision=lax.Precision.HIGHEST)
        score = jnp.where(future, -1.0, jnp.where(forced, 1e4, imp_t))

        sc = [score[8 * r:8 * r + 8] for r in range(ngrp)]
        rank = [jnp.zeros((8, tq), F32) for _ in range(ngrp)]
        for i in range(nslc):
            si = jnp.broadcast_to(score[i:i + 1, :], (8, tq))
            for r in range(ngrp):
                if r < i // 8:
                    before = si > sc[r]
                elif r > i // 8:
                    before = si >= sc[r]
                else:
                    before = (si > sc[r]) | ((sub > i % 8) & (si >= sc[r]))
                rank[r] = rank[r] + jnp.where(before, 1.0, 0.0)
        for r in range(ngrp):
            selt_ref[g, 8 * r:8 * r + 8, :] = jnp.where(rank[r] < float(nsel), 1.0, 0.0)
        if g % 2:
            picks = (selt_ref[g - 1] + selt_ref[g]).astype(BF16)
            per_block = jnp.where(_dot_nt(ones, picks) > 0.0, 1.0, 0.0).astype(BF16)
            blk_i = lax.broadcasted_iota(jnp.int32, (nslc, nslc // nb), 0)
            tile_i = lax.broadcasted_iota(jnp.int32, (nslc, nslc // nb), 1)
            pool = jnp.where(blk_i >> (nb.bit_length() - 1) == tile_i, 1.0, 0.0).astype(BF16)
            cnt_ref[g // 2:g // 2 + 1, :] = _dot(per_block, pool)[0:1, :].astype(jnp.int32)

    for pair, tile in enumerate(_untranspose_pairs(jnp.concatenate(outs, axis=1), tq)):
        ocmp_ref[:, pair * LANES:(pair + 1) * LANES] = tile


def _cmp_attn(proj, cmp_kv, bsz, seq):
    tq = min(Q_TILE, seq)
    nc = cmp_kv.shape[1]
    nslc = seq // SLC_BLOCK
    nsel = min(N_SELECT, nslc)
    nq = seq // tq
    return pl.pallas_call(
        functools.partial(_cmp_attn_kernel, tq=tq, nc=nc, nslc=nslc, nsel=nsel, nb=KEY_TILE // SLC_BLOCK),
        grid=(bsz, nq),
        in_specs=[pl.BlockSpec((tq, NSA_WIDTH), lambda b, i: (b * nq + i, 0)),
                  pl.BlockSpec((None, nc, 2 * NSA_KV_WIDTH), lambda b, i: (b, 0, 0)),
                  pl.BlockSpec((tq, LANES), lambda b, i: (b * nq + i, COL_GZ // LANES))],
        out_specs=[pl.BlockSpec((tq, NSA_WIDTH), lambda b, i: (b * nq + i, 0)),
                   pl.BlockSpec((NSA_KV_HEADS, None, nslc, tq), lambda b, i: (0, b, 0, i)),
                   pl.BlockSpec((None, None, NSA_KV_HEADS // 2, seq // KEY_TILE), lambda b, i: (b, i, 0, 0))],
        out_shape=[jax.ShapeDtypeStruct((bsz * seq, NSA_WIDTH), F32),
                   jax.ShapeDtypeStruct((NSA_KV_HEADS, bsz, nslc, seq), F32),
                   jax.ShapeDtypeStruct((bsz, nq, NSA_KV_HEADS // 2, seq // KEY_TILE), jnp.int32)],
        scratch_shapes=[pltpu.VMEM((2 * NSA_KV_HEADS, nc, LANES), BF16),
                        pltpu.VMEM((NSA_KV_HEADS // 2, LANES, nc), BF16),
                        pltpu.VMEM((NSA_HEADS, nc, tq), F32),
                        pltpu.VMEM((NSA_HEADS, nc, tq), F32)],
        compiler_params=_params(("parallel", "arbitrary")),
        name="nsa_cmp_attn",
    )(proj, cmp_kv, proj)


def _stage_kv(k_ref, v_ref, kz_ref, vt_ref, nk, ts):
    for c in range(nk):
        padded = _pad_halves(k_ref[c * ts:(c + 1) * ts, :])
        for idx in range(4):
            kz_ref[idx, c] = padded[idx].astype(BF16)
        v_t = v_ref[c * ts:(c + 1) * ts, :].T
        ones_row = jnp.where(lax.broadcasted_iota(jnp.int32, (VT_ROWS - DK, ts), 0) == 0, 1.0, 0.0)
        for gi in range(2):
            vt_ref[gi, c, 0:DK, :] = v_t[gi * DK:(gi + 1) * DK, :].astype(BF16)
            vt_ref[gi, c, DK:VT_ROWS, :] = ones_row.astype(BF16)


def _attend(mode, nsub, gp, t0, slopes_ref, cnt_ref, selt_ref, qb_ref, bias_ref, kz_ref, vt_ref, s_ref,
            m_ref, acc_ref, idx_ref, *, tq, ts, nk):
    nb = ts // SLC_BLOCK
    ts_shift = ts.bit_length() - 1
    m_ref[...] = jnp.full(m_ref.shape, NEG_INF, F32)
    acc_ref[...] = jnp.zeros(acc_ref.shape, F32)

    last_sub = (t0 + tq - 1) >> ts_shift
    if mode == "sel":
        def build(j, n):
            idx_ref[n] = j
            return n + jnp.where(((cnt_ref[gp, j] > 0) | (j == last_sub)) & (j <= last_sub), 1, 0)

        n_live = lax.fori_loop(0, nk, build, 0, unroll=4)
    else:
        first_sub = jnp.maximum(t0 - (WINDOW - 1), 0) >> ts_shift
        n_live = last_sub + 1 - first_sub
        for u in range(nsub):
            idx_ref[u] = jnp.minimum(first_sub + u, last_sub)

    key_i = lax.broadcasted_iota(jnp.int32, (ts, tq), 0)
    qry_t = lax.broadcasted_iota(jnp.int32, (ts, tq), 1) + t0

    def sub_tile(step, u):
        pos = step * nsub + u
        return idx_ref[jnp.minimum(pos, n_live - 1)], pos < n_live

    def scores(step):
        for u in range(nsub):
            kj, _ = sub_tile(step, u)
            for gi in range(2):
                for h in range(HPG):
                    pair = gi * (HPG // 2) + h // 2
                    s_ref[u * 2 + gi, :, h * tq:(h + 1) * tq] = _dot(
                        kz_ref[gi * 2 + h % 2, kj], qb_ref[pair * LANES:(pair + 1) * LANES, :])

    def softmax_update(step):
        for u in range(nsub):
            kj, live = sub_tile(step, u)
            j0 = kj * ts
            rel0 = (jnp.zeros((1, tq), jnp.int32) + (j0 - t0)).astype(F32)
            dd = qry_t - (key_i + j0)
            lowest = jnp.where(live, 0, 1 << 30)
            for gi in range(2):
                if mode == "sel":
                    chosen = selt_ref[gi, pl.ds(kj * nb + nb - 1, 1), :]
                    for c in range(nb - 2, -1, -1):
                        chosen = jnp.where(key_i < (c + 1) * SLC_BLOCK,
                                           selt_ref[gi, pl.ds(kj * nb + c, 1), :], chosen)
                    mask = (chosen > 0.5) & (dd >= lowest)
                else:
                    mask = (dd >= lowest) & (dd < WINDOW)
                ps = []
                alphas = []
                for h in range(HPG):
                    shift = (slopes_ref[(gp * 2 + gi) * HPG + h] * LOG2E) * rel0
                    cols = slice(h * tq, (h + 1) * tq)
                    x = jnp.where(mask, s_ref[u * 2 + gi, :, cols] + bias_ref[gi * HPG + h], MASKED)
                    m_old = m_ref[gi, :, cols]
                    m_new = jnp.maximum(m_old, jnp.max(x, axis=0, keepdims=True) + shift)
                    p = jnp.exp2(x - (m_new - shift))
                    m_ref[gi, :, cols] = m_new
                    ps.append(p.astype(BF16))
                    alphas.append(jnp.exp2(m_old - m_new))
                pv = _dot(vt_ref[gi, kj], jnp.concatenate(ps, axis=1))
                acc_ref[gi] = acc_ref[gi] * jnp.concatenate(alphas, axis=1) + pv

    def body(step, carry):
        scores(step)
        softmax_update(step)
        return carry

    if mode == "sel":
        assert nsub & (nsub - 1) == 0
        lax.fori_loop(0, (n_live + nsub - 1) >> (nsub.bit_length() - 1), body, 0)
    else:
        body(0, 0)


def _flash_kernel(slopes_ref, cnt_ref, q_ref, ks_ref, vs_ref, kw_ref, vw_ref, gz_ref, selt_ref, o_ref,
                  kzs_ref, vts_ref, kzw_ref, vtw_ref, qb_ref, bias_ref, gate_ref, s_ref, m_ref, acc_ref, idx_ref,
                  *, tq, ts, nk, nsub_sel, nsub_win):
    gp = pl.program_id(1)
    qi = pl.program_id(2)
    t0 = qi * tq

    @pl.when(qi == 0)
    def _():
        _stage_kv(ks_ref, vs_ref, kzs_ref, vts_ref, nk, ts)
        _stage_kv(kw_ref, vw_ref, kzw_ref, vtw_ref, nk, ts)

    key_f = lax.broadcasted_iota(jnp.int32, (ts, tq), 0).astype(F32)
    qb_ref[...] = (q_ref[...] * (DK ** -0.5 * LOG2E)).T.astype(BF16)
    gate_ref[...] = _sigmoid(gz_ref[...].T)
    for gh in range(2 * HPG):
        bias_ref[gh] = (slopes_ref[gp * 2 * HPG + gh] * LOG2E) * key_f

    common = dict(tq=tq, ts=ts, nk=nk)
    _attend("sel", nsub_sel, gp, t0, slopes_ref, cnt_ref, selt_ref, qb_ref, bias_ref, kzs_ref, vts_ref, s_ref,
            m_ref.at[0], acc_ref.at[0], idx_ref, **common)
    _attend("win", nsub_win, gp, t0, slopes_ref, None, None, qb_ref, bias_ref, kzw_ref, vtw_ref, s_ref,
            m_ref.at[1], acc_ref.at[1], idx_ref, **common)

    outs = []
    for gi in range(2):
        branch_out = [acc_ref[br, gi, 0:DK, :] / acc_ref[br, gi, DK:DK + 1, :] for br in range(2)]
        for h in range(HPG):
            row = (gp * 2 * HPG + gi * HPG + h) * 3
            cols = slice(h * tq, (h + 1) * tq)
            outs.append(branch_out[0][:, cols] * gate_ref[pl.ds(row + 1, 1), :]
                        + branch_out[1][:, cols] * gate_ref[pl.ds(row + 2, 1), :])
    for pair, tile in enumerate(_untranspose_pairs(jnp.concatenate(outs, axis=1), tq)):
        o_ref[:, pair * LANES:(pair + 1) * LANES] = tile


def _flash(proj, selt, cnt, slopes, bsz, seq):
    tq = min(Q_TILE, seq)
    ts = KEY_TILE
    nsub_sel = 4
    nsub_win = (WINDOW + tq) // ts
    nq = seq // tq
    nk = seq // ts
    nslc = seq // SLC_BLOCK
    first_kv = (COL_KV + 2 * NSA_KV_WIDTH) // LANES
    kv_spec = lambda which: pl.BlockSpec((seq, LANES), lambda b, g, i: (b, first_kv + 2 * which + g))
    smem = pltpu.SMEM
    max_units = 2 * max(nsub_sel, nsub_win)
    return pl.pallas_call(
        functools.partial(_flash_kernel, tq=tq, ts=ts, nk=nk, nsub_sel=nsub_sel, nsub_win=nsub_win),
        grid=(bsz, 2, nq),
        in_specs=[pl.BlockSpec(memory_space=smem),
                  pl.BlockSpec((None, None, 2, nk), lambda b, g, i: (b, i, 0, 0), memory_space=smem),
                  pl.BlockSpec((tq, 2 * HPG * DK), lambda b, g, i: (b * nq + i, g)),
                  kv_spec(0), kv_spec(1), kv_spec(2), kv_spec(3),
                  pl.BlockSpec((tq, LANES), lambda b, g, i: (b * nq + i, COL_GZ // LANES)),
                  pl.BlockSpec((2, None, nslc, tq), lambda b, g, i: (g, b, 0, i))],
        out_specs=pl.BlockSpec((tq, 2 * HPG * DK), lambda b, g, i: (b * nq + i, g)),
        out_shape=jax.ShapeDtypeStruct((bsz * seq, NSA_WIDTH), F32),
        scratch_shapes=[pltpu.VMEM((4, nk, ts, LANES), BF16),
                        pltpu.VMEM((2, nk, VT_ROWS, ts), BF16),
                        pltpu.VMEM((4, nk, ts, LANES), BF16),
                        pltpu.VMEM((2, nk, VT_ROWS, ts), BF16),
                        pltpu.VMEM((2 * HPG * DK, tq), BF16),
                        pltpu.VMEM((2 * HPG, ts, tq), F32),
                        pltpu.VMEM((LANES, tq), F32),
                        pltpu.VMEM((max_units, ts, HPG * tq), F32),
                        pltpu.VMEM((2, 2, 1, HPG * tq), F32),
                        pltpu.VMEM((2, 2, VT_ROWS, HPG * tq), F32),
                        pltpu.SMEM((nk + max_units,), jnp.int32)],
        compiler_params=_params(("parallel", "parallel", "arbitrary")),
        name="nsa_sel_win",
    )(slopes, cnt, proj, proj, proj, proj, proj, proj, selt)


def _lru_kernel(x_ref, y_ref, cw_ref, cb_ref, wa_ref, ba_ref, wi_ref, bi_ref, lam_ref, o_ref,
                tail_ref, h_ref, *, tt):
    @pl.when(pl.program_id(1) == 0)
    def _():
        tail_ref[...] = jnp.zeros_like(tail_ref)
        h_ref[...] = jnp.zeros_like(h_ref)

    x = x_ref[...]
    ext = jnp.concatenate([tail_ref[...], x], axis=0)
    xc = cb_ref[...] + x * cw_ref[CONV_WIDTH - 1:CONV_WIDTH, :]
    for k in range(CONV_WIDTH - 1):
        back = CONV_WIDTH - 1 - k
        xc = xc + ext[8 - back:8 - back + tt, :] * cw_ref[k:k + 1, :]
    tail_ref[...] = x[tt - 8:, :]

    xcb = xc.astype(BF16)
    r = _sigmoid(_dot(xcb, wa_ref[...]) + ba_ref[...])
    gate_i = _sigmoid(_dot(xcb, wi_ref[...]) + bi_ref[...])
    neg_lam = -lam_ref[...]
    softplus = jnp.maximum(neg_lam, 0.0) + jnp.log1p(jnp.exp(-jnp.abs(neg_lam)))
    log_a = -LRU_C * r * softplus
    a = jnp.exp(log_a)
    u = jnp.sqrt(jnp.tanh(-log_a) * (a * a + 1.0)) * (gate_i * xc)

    sub = min(LRU_SCAN_ROWS, tt)
    row_in_sub = lax.broadcasted_iota(jnp.int32, (tt, 1), 0) & (sub - 1)
    step = 1
    while step < sub:
        keep = row_in_sub >= step
        a_prev = jnp.where(keep, pltpu.roll(a, step, 0), 1.0)
        u_prev = jnp.where(keep, pltpu.roll(u, step, 0), 0.0)
        u = u + a * u_prev
        a = a * a_prev
        step *= 2
    carry = h_ref[...]
    y = y_ref[...]
    for k in range(tt // sub):
        rows = slice(k * sub, (k + 1) * sub)
        hs = u[rows] + a[rows] * carry
        carry = hs[sub - 1:sub, :]
        o_ref[rows, :] = hs * jax.nn.gelu(y[rows], approximate=True)
    h_ref[...] = carry


def _lru(proj, cw, cb, wa, ba, wi, bi, lam, bsz, seq):
    tt = min(512, seq)
    nt = seq // tt
    w = LRU_WIDTH
    xcol = COL_LRU_X // w
    ycol = COL_LRU_Y // w
    vec = pl.BlockSpec((1, w), lambda b, i: (0, 0))
    mat = pl.BlockSpec((w, w), lambda b, i: (0, 0))
    return pl.pallas_call(
        functools.partial(_lru_kernel, tt=tt),
        grid=(bsz, nt),
        in_specs=[pl.BlockSpec((tt, w), lambda b, i: (b * nt + i, xcol)),
                  pl.BlockSpec((tt, w), lambda b, i: (b * nt + i, ycol)),
                  pl.BlockSpec((CONV_WIDTH, w), lambda b, i: (0, 0)),
                  vec, mat, vec, mat, vec, vec],
        out_specs=pl.BlockSpec((tt, w), lambda b, i: (b * nt + i, 0)),
        out_shape=jax.ShapeDtypeStruct((bsz * seq, w), F32),
        scratch_shapes=[pltpu.VMEM((8, w), F32), pltpu.VMEM((1, w), F32)],
        compiler_params=_params(("parallel", "arbitrary")),
        name="rglru",
    )(proj, proj, cw, cb, wa, ba, wi, bi, lam)


def _gla_kernel(q_ref, k_ref, v_ref, r_ref, gz_ref, w2_ref, bg_ref, ng_ref, o_ref, st_ref, *, tt):
    c = GLA_CHUNK
    dh = GLA_HEAD_DIM

    @pl.when(pl.program_id(1) == 0)
    def _():
        st_ref[...] = jnp.zeros_like(st_ref)

    gate = _dot(gz_ref[...].astype(BF16), w2_ref[...]) + bg_ref[...]
    log_alpha = (jnp.minimum(gate, 0.0) - jnp.log1p(jnp.exp(-jnp.abs(gate)))) / GLA_GATE_TAU
    causal = (lax.broadcasted_iota(jnp.int32, (c, c), 1) <= lax.broadcasted_iota(jnp.int32, (c, c), 0))
    tri = jnp.where(causal, 1.0, 0.0)

    for n in range(tt // c):
        rs = slice(n * c, (n + 1) * c)
        bcum = _dot(tri, log_alpha[rs], precision=lax.Precision.HIGHEST)
        b_last = bcum[c - 1:c, :]
        e_pos = jnp.exp(bcum)
        q_t = (q_ref[rs, :] * dh ** -0.5) * e_pos
        kk = k_ref[rs, :]
        k_t = kk * jnp.exp(-bcum)
        k_end = kk * jnp.exp(b_last - bcum)
        decay = jnp.exp(b_last)
        vv = v_ref[rs, :]
        for hh in range(GLA_HEADS):
            cs = slice(hh * dh, (hh + 1) * dh)
            qh = q_t[:, cs].astype(BF16)
            vh = vv[:, cs]
            att = jnp.where(causal, _dot_nt(qh, k_t[:, cs].astype(BF16)), 0.0)
            st = st_ref[hh]
            o = _dot(att.astype(BF16), vh.astype(BF16)) + _dot_nt(qh, st.astype(BF16))
            st_ref[hh] = st * decay[:, cs] + _dot(vh.T.astype(BF16), k_end[:, cs].astype(BF16))
            o = _rms_scale(o) * ng_ref[:, cs]
            rr = r_ref[rs, cs]
            o_ref[rs, cs] = o * (rr * _sigmoid(rr))


def _gla(proj, w2pad, bg, ng, bsz, seq):
    tt = min(512, seq)
    nt = seq // tt
    w = GLA_WIDTH
    cols = [COL_GLA_Q // w, COL_GLA_K // w, COL_GLA_V // w, COL_GLA_R // w]
    seg = [pl.BlockSpec((tt, w), functools.partial(lambda b, i, cc: (b * nt + i, cc), cc=cc)) for cc in cols]
    vec = pl.BlockSpec((1, w), lambda b, i: (0, 0))
    return pl.pallas_call(
        functools.partial(_gla_kernel, tt=tt),
        grid=(bsz, nt),
        in_specs=seg + [pl.BlockSpec((tt, LANES), lambda b, i: (b * nt + i, COL_GZ // LANES)),
                        pl.BlockSpec((LANES, w), lambda b, i: (0, 0)), vec, vec],
        out_specs=pl.BlockSpec((tt, w), lambda b, i: (b * nt + i, 0)),
        out_shape=jax.ShapeDtypeStruct((bsz * seq, w), F32),
        scratch_shapes=[pltpu.VMEM((GLA_HEADS, GLA_HEAD_DIM, GLA_HEAD_DIM), F32)],
        compiler_params=_params(("parallel", "arbitrary")),
        name="gla",
    )(proj, proj, proj, proj, proj, w2pad, bg, ng)


def _out_proj_kernel(ocmp_ref, oselwin_ref, olru_ref, ogla_ref, h_ref, w_ref, g_ref, o_ref, wb_ref):
    @pl.when(pl.program_id(0) == 0)
    def _():
        wb_ref[...] = w_ref[...].astype(BF16)

    nsa = ocmp_ref[...] + oselwin_ref[...]
    y = _dot(nsa.astype(BF16), wb_ref[0:NSA_WIDTH, :])
    y = y + _dot(olru_ref[...].astype(BF16), wb_ref[NSA_WIDTH:NSA_WIDTH + LRU_WIDTH, :])
    y = y + _dot(ogla_ref[...].astype(BF16), wb_ref[NSA_WIDTH + LRU_WIDTH:, :])
    o_ref[...] = h_ref[...] + _rms_scale(y) * g_ref[...]


def _resident(shape, layer):
    zeros = (0,) * len(shape)
    return pl.BlockSpec((None,) + shape, lambda i: (layer,) + zeros, pipeline_mode=pl.Buffered(1))


def _out_proj(ocmp, oselwin, olru, ogla, h2d, w_out, g, layer):
    m = h2d.shape[0]
    tm = min(256, m)
    row = lambda width: pl.BlockSpec((tm, width), lambda i: (i, 0))
    return pl.pallas_call(
        _out_proj_kernel,
        grid=(m // tm,),
        in_specs=[row(NSA_WIDTH), row(NSA_WIDTH), row(LRU_WIDTH), row(GLA_WIDTH), row(D_MODEL),
                  _resident((D_MODEL, D_MODEL), layer),
                  pl.BlockSpec((1, D_MODEL), lambda i: (0, 0))],
        out_specs=row(D_MODEL),
        out_shape=jax.ShapeDtypeStruct((m, D_MODEL), F32),
        scratch_shapes=[pltpu.VMEM((D_MODEL, D_MODEL), BF16)],
        compiler_params=_params(("arbitrary",)),
        name="out_proj",
    )(ocmp, oselwin, olru, ogla, h2d, w_out, g)


def _mlp_kernel(h_ref, gpre_ref, wu_ref, wd_ref, gpost_ref, o_ref, un_ref):
    f = pl.program_id(1)

    @pl.when(f == 0)
    def _():
        un_ref[...] = (_rms_scale(h_ref[...]) * gpre_ref[...]).astype(BF16)
        o_ref[...] = jnp.zeros_like(o_ref)

    a = jnp.maximum(_dot(un_ref[...], wu_ref[...].astype(BF16)), 0.0)
    o_ref[...] += _dot((a * a).astype(BF16), wd_ref[...].astype(BF16))

    @pl.when(f == pl.num_programs(1) - 1)
    def _():
        o_ref[...] = h_ref[...] + _rms_scale(o_ref[...]) * gpost_ref[...]


def _mlp(h2d, gpre, w_up, w_down, gpost, layer):
    m = h2d.shape[0]
    tm = min(1024, m)
    tf = 512
    vec = pl.BlockSpec((1, D_MODEL), lambda i, f: (0, 0))
    return pl.pallas_call(
        _mlp_kernel,
        grid=(m // tm, D_FF // tf),
        in_specs=[pl.BlockSpec((tm, D_MODEL), lambda i, f: (i, 0)), vec,
                  pl.BlockSpec((None, D_MODEL, tf), lambda i, f: (layer, 0, f)),
                  pl.BlockSpec((None, tf, D_MODEL), lambda i, f: (layer, f, 0)), vec],
        out_specs=pl.BlockSpec((tm, D_MODEL), lambda i, f: (i, 0), pipeline_mode=pl.Buffered(1)),
        out_shape=jax.ShapeDtypeStruct((m, D_MODEL), F32),
        scratch_shapes=[pltpu.VMEM((tm, D_MODEL), BF16)],
        compiler_params=_params(("parallel", "arbitrary")),
        name="mlp",
    )(h2d, gpre, w_up, w_down, gpost)


def _ple_kernel(h_ref, p_ref, wg_ref, wp_ref, o_ref, wgb_ref, wpb_ref):
    @pl.when(pl.program_id(0) == 0)
    def _():
        wgb_ref[...] = wg_ref[...].astype(BF16)
        wpb_ref[...] = wp_ref[...].astype(BF16)

    h = h_ref[...]
    gate = _sigmoid(_dot(h.astype(BF16), wgb_ref[...]))
    o_ref[...] = h + gate * _dot(p_ref[...].astype(BF16), wpb_ref[...])


def _ple(h2d, p_all, w_gate, w_ple, layer):
    m = h2d.shape[0]
    tm = min(512, m)
    return pl.pallas_call(
        _ple_kernel,
        grid=(m // tm,),
        in_specs=[pl.BlockSpec((tm, D_MODEL), lambda i: (i, 0)),
                  pl.BlockSpec((None, tm, PLE_DIM), lambda i: (layer, i, 0)),
                  _resident((D_MODEL, D_MODEL), layer),
                  _resident((PLE_DIM, D_MODEL), layer)],
        out_specs=pl.BlockSpec((tm, D_MODEL), lambda i: (i, 0)),
        out_shape=jax.ShapeDtypeStruct((m, D_MODEL), F32),
        scratch_shapes=[pltpu.VMEM((D_MODEL, D_MODEL), BF16), pltpu.VMEM((PLE_DIM, D_MODEL), BF16)],
        compiler_params=_params(("arbitrary",)),
        name="ple",
    )(h2d, p_all, w_gate, w_ple)


def _w_in_views(w_in):
    w_t = jnp.swapaxes(w_in, 1, 2)
    g0 = NSA_WIDTH + 6 * NSA_KV_WIDTH
    z0 = D_IN - GLA_GATE_RANK
    w_gz = jnp.concatenate([w_t[:, g0:g0 + N_GATES], w_t[:, z0:]], axis=1)
    return w_t, w_gz


def _compress_weights(cmp_w, cmp_pe):
    w4 = cmp_w.reshape(2, CMP_LEN, DK, DK)
    eye = jnp.eye(2, dtype=cmp_w.dtype)
    wbd = jnp.einsum("clde,xy->clxdye", w4, eye).reshape(2, CMP_LEN, LANES, LANES).astype(BF16)
    pe = jnp.tile(cmp_pe, (1, 1, 2)).reshape(2, CMP_LEN, 1, LANES)
    return pe, wbd


def _block_diag(w):
    eye = jnp.eye(LRU_BLOCKS, dtype=w.dtype)
    return jnp.einsum("ncd,nm->ncmd", w, eye).reshape(LRU_WIDTH, LRU_WIDTH).astype(BF16)


def _layer(h2d, bsz, seq, layer, stacked, norm_mix_pre, nsa_cmp_w, nsa_cmp_pe, lru_conv_w, lru_conv_b,
           lru_wa, lru_ba, lru_wi, lru_bi, lru_lambda, gla_w_gate2, gla_b_gate, gla_norm, norm_mix_post,
           norm_mlp_pre, norm_mlp_post):
    w_in_all, w_out_all, w_up_all, w_down_all, w_ple_gate_all, w_ple_all, p_all = stacked
    row = lambda v: v.reshape(1, -1)
    proj = _in_proj(h2d, row(norm_mix_pre), *w_in_all, layer)

    cmp_kv = _compress(proj, *_compress_weights(nsa_cmp_w, nsa_cmp_pe), bsz, seq)
    o_cmp, selt, cnt = _cmp_attn(proj, cmp_kv, bsz, seq)
    slopes = jnp.asarray(SLOPES, F32)
    o_sel_win = _flash(proj, selt, cnt, slopes, bsz, seq)

    o_lru = _lru(proj, lru_conv_w, row(lru_conv_b), _block_diag(lru_wa), row(lru_ba),
                 _block_diag(lru_wi), row(lru_bi), row(lru_lambda), bsz, seq)

    w2pad = jnp.zeros((LANES, GLA_WIDTH), F32).at[N_GATES:N_GATES + GLA_GATE_RANK].set(gla_w_gate2)
    o_gla = _gla(proj, w2pad.astype(BF16), row(gla_b_gate), row(gla_norm), bsz, seq)

    h2d = _out_proj(o_cmp, o_sel_win, o_lru, o_gla, h2d, w_out_all, row(norm_mix_post), layer)
    h2d = _mlp(h2d, row(norm_mlp_pre), w_up_all, w_down_all, row(norm_mlp_post), layer)
    return _ple(h2d, p_all, w_ple_gate_all, w_ple_all, layer)


def kernel(x, p, norm_mix_pre, w_in, nsa_cmp_w, nsa_cmp_pe, lru_conv_w, lru_conv_b, lru_wa, lru_ba,
           lru_wi, lru_bi, lru_lambda, gla_w_gate2, gla_b_gate, gla_norm, w_out, norm_mix_post,
           norm_mlp_pre, w_up, w_down, norm_mlp_post, w_ple_gate, w_ple):
    bsz, seq, _ = x.shape
    h2d = x.reshape(bsz * seq, D_MODEL)
    small = (norm_mix_pre, nsa_cmp_w, nsa_cmp_pe, lru_conv_w, lru_conv_b, lru_wa, lru_ba, lru_wi, lru_bi,
             lru_lambda, gla_w_gate2, gla_b_gate, gla_norm, norm_mix_post, norm_mlp_pre, norm_mlp_post)
    depth = p.shape[0]
    stacked = (_w_in_views(w_in), w_out, w_up, w_down, w_ple_gate, w_ple,
               p.reshape(depth, bsz * seq, PLE_DIM))
    for i in range(depth):
        h2d = _layer(h2d, bsz, seq, i, stacked, *(w[i] for w in small))
    return h2d.reshape(bsz, seq, D_MODEL)
```

```python
import functools

import jax
import jax.numpy as jnp
from jax import lax
from jax.experimental import pallas as pl
from jax.experimental.pallas import tpu as pltpu

F32 = jnp.float32
BF16 = jnp.bfloat16

D_MODEL = 2048
PLE_DIM = 256
NSA_HEADS = 16
NSA_KV_HEADS = 4
HPG = NSA_HEADS // NSA_KV_HEADS
NSA_WIDTH = 1024
DK = 64
NSA_KV_WIDTH = NSA_KV_HEADS * DK
CMP_LEN = 32
CMP_STRIDE = 16
SLC_BLOCK = 64
SLC_SHIFT = 6
N_SELECT = 16
WINDOW = 512
LRU_WIDTH = 512
LRU_BLOCKS = 8
LRU_BLOCK_DIM = 64
CONV_WIDTH = 4
LRU_C = 8.0
LRU_SCAN_ROWS = 32
GLA_WIDTH = 512
GLA_HEADS = 4
GLA_HEAD_DIM = 128
GLA_GATE_RANK = 16
GLA_GATE_TAU = 16.0
GLA_CHUNK = 64
D_FF = 4 * D_MODEL
EPS = 1e-6
NEG_INF = -1e30
MASKED = 2.0 * NEG_INF

COL_Q = 0
COL_KV = 1024
COL_LRU_X = 2560
COL_LRU_Y = 3072
COL_GLA_Q = 3584
COL_GLA_K = 4096
COL_GLA_V = 4608
COL_GLA_R = 5120
COL_GZ = 5632
D_IN_PAD = 6144
N_GATES = 3 * NSA_HEADS
LANES = 128
KEY_TILE = 128
Q_TILE = 256
VT_ROWS = DK + 16
LOG2E = 1.4426950408889634

SLOPES = tuple(2.0 ** (-8.0 * i / NSA_HEADS) for i in range(1, NSA_HEADS + 1))

V7X_VMEM_BYTES = 64 * 1024 * 1024
VMEM_LIMIT = V7X_VMEM_BYTES - 8 * 1024 * 1024

NT_DIMS = (((1,), (1,)), ((), ()))
TN_DIMS = (((0,), (0,)), ((), ()))


def _params(sem):
    return pltpu.CompilerParams(dimension_semantics=sem, vmem_limit_bytes=VMEM_LIMIT)


def _dot(a, b, **kw):
    return jnp.dot(a, b, preferred_element_type=F32, **kw)


def _dot_nt(a, b, **kw):
    return lax.dot_general(a, b, NT_DIMS, preferred_element_type=F32, **kw)


def _dot_tn(a, b, **kw):
    return lax.dot_general(a, b, TN_DIMS, preferred_element_type=F32, **kw)


def _sigmoid(x):
    return 1.0 / (1.0 + jnp.exp(-x))


def _rms_scale(x):
    return x * lax.rsqrt(jnp.mean(x * x, axis=-1, keepdims=True) + EPS)


def _half_masks(rows):
    lane = lax.broadcasted_iota(jnp.int32, (rows, LANES), 1)
    return lane < DK, lane >= DK


def _pad_halves(blk):
    lo, hi = _half_masks(blk.shape[0])
    swapped = pltpu.roll(blk, DK, 1)
    zero = jnp.zeros_like(blk)
    return (jnp.where(lo, blk, zero), jnp.where(hi, swapped, zero),
            jnp.where(lo, swapped, zero), jnp.where(hi, blk, zero))


def _untranspose_pairs(o_t, tq):
    sub = lax.broadcasted_iota(jnp.int32, (4 * DK, LANES), 0)
    lane = lax.broadcasted_iota(jnp.int32, (4 * DK, LANES), 1)
    place = jnp.where(lane == (sub & (DK - 1)) + jnp.where(sub >= 2 * DK, DK, 0), 1.0, 0.0).astype(BF16)
    out = []
    for pair in range(o_t.shape[1] // (2 * tq)):
        parts = []
        for half in range(2):
            v = o_t[:, (2 * pair + half) * tq:(2 * pair + half + 1) * tq]
            v_hi = v.astype(BF16)
            parts += [v_hi, (v - v_hi.astype(F32)).astype(BF16)]
        out.append(_dot_tn(jnp.concatenate(parts, axis=0), place))
    return out


IN_TILE = 512
HEAD_TILES = (NSA_WIDTH + 6 * NSA_KV_WIDTH) // IN_TILE
TAIL_TILES = (2 * LRU_WIDTH + 4 * GLA_WIDTH) // IN_TILE
D_IN = NSA_WIDTH + 6 * NSA_KV_WIDTH + N_GATES + 2 * LRU_WIDTH + 4 * GLA_WIDTH + GLA_GATE_RANK


def _in_proj_kernel(x_ref, g_ref, w_ref, wgz_ref, o_ref, xn_ref, wres_ref):
    i = pl.program_id(0)
    j = pl.program_id(1)
    last = HEAD_TILES + TAIL_TILES
    n_gz = N_GATES + GLA_GATE_RANK
    k = wres_ref.shape[1]

    @pl.when(j == 0)
    def _():
        xn_ref[...] = (_rms_scale(x_ref[...]) * g_ref[...]).astype(BF16)

    @pl.when((i == 0) & (j < last))
    def _():
        wres_ref[j] = w_ref[0].T.astype(BF16)

    @pl.when((i == 0) & (j == last))
    def _():
        head = jnp.concatenate([wgz_ref[...], jnp.zeros((LANES - n_gz, k), F32)], axis=0)
        wres_ref[last, :, 0:LANES] = head.T.astype(BF16)
        wres_ref[last, :, LANES:] = jnp.zeros((k, IN_TILE - LANES), BF16)

    o_ref[...] = _dot(xn_ref[...], wres_ref[j])


def _in_proj(h2d, g, w_t, w_gz, layer):
    m, k = h2d.shape
    tm = min(1024, m)
    n_tiles = HEAD_TILES + TAIL_TILES + 1
    assert n_tiles * IN_TILE == D_IN_PAD

    def w_rows(i, j):
        start = jnp.where(j < HEAD_TILES, j * IN_TILE, j * IN_TILE + N_GATES)
        start = jnp.where(i == 0, jnp.minimum(start, D_IN - IN_TILE), D_IN - IN_TILE)
        return layer, pl.multiple_of(start, 8), 0

    return pl.pallas_call(
        _in_proj_kernel,
        grid=(m // tm, n_tiles),
        in_specs=[pl.BlockSpec((tm, k), lambda i, j: (i, 0), pipeline_mode=pl.Buffered(1)),
                  pl.BlockSpec((1, k), lambda i, j: (0, 0)),
                  pl.BlockSpec((pl.Element(1), pl.Element(IN_TILE), pl.Element(k)), w_rows),
                  pl.BlockSpec((None, N_GATES + GLA_GATE_RANK, k), lambda i, j: (layer, 0, 0))],
        out_specs=pl.BlockSpec((tm, IN_TILE), lambda i, j: (i, j)),
        out_shape=jax.ShapeDtypeStruct((m, D_IN_PAD), F32),
        scratch_shapes=[pltpu.VMEM((tm, k), BF16), pltpu.VMEM((n_tiles, k, IN_TILE), BF16)],
        compiler_params=_params(("arbitrary", "arbitrary")),
        name="in_proj",
    )(h2d, g, w_t, w_gz)


def _compress_kernel(x_ref, pe_ref, w_ref, o_ref, *, nc):
    first = jnp.zeros((nc, LANES), F32)
    second = jnp.zeros((nc, LANES), F32)
    for l in range(CMP_STRIDE):
        x = x_ref[pl.ds(l, nc, stride=CMP_STRIDE), :]
        first = first + _dot((x + pe_ref[l]).astype(BF16), w_ref[l])
        second = second + _dot((x + pe_ref[CMP_STRIDE + l]).astype(BF16), w_ref[CMP_STRIDE + l])
    o_ref[...] = first + pltpu.roll(second, nc - 1, 0)


def _compress(proj, pe, w, bsz, seq):
    nc = seq // CMP_STRIDE
    return pl.pallas_call(
        functools.partial(_compress_kernel, nc=nc),
        grid=(bsz, NSA_KV_HEADS),
        in_specs=[pl.BlockSpec((seq, LANES), lambda i, j: (i, COL_KV // LANES + j)),
                  pl.BlockSpec((None, CMP_LEN, 1, LANES), lambda i, j: (j // 2, 0, 0, 0)),
                  pl.BlockSpec((None, CMP_LEN, LANES, LANES), lambda i, j: (j // 2, 0, 0, 0))],
        out_specs=pl.BlockSpec((None, nc, LANES), lambda i, j: (i, 0, j)),
        out_shape=jax.ShapeDtypeStruct((bsz, nc, 2 * NSA_KV_WIDTH), F32),
        compiler_params=_params(("parallel", "parallel")),
        name="nsa_compress",
    )(proj, pe, w)


def _cmp_attn_kernel(q_ref, kv_ref, gz_ref, ocmp_ref, selt_ref, cnt_ref, kz_ref, vct_ref, bias_ref, s_ref,
                     *, tq, nc, nslc, nsel, nb):
    t0 = pl.program_id(1) * tq

    @pl.when(pl.program_id(1) == 0)
    def _():
        for gp in range(NSA_KV_HEADS // 2):
            padded = _pad_halves(kv_ref[:, gp * LANES:(gp + 1) * LANES])
            for idx in range(4):
                kz_ref[gp * 4 + idx] = padded[idx].astype(BF16)
            vct_ref[gp] = kv_ref[:, NSA_KV_WIDTH + gp * LANES:NSA_KV_WIDTH + (gp + 1) * LANES].T.astype(BF16)
        end_n = (lax.broadcasted_iota(jnp.int32, (nc, tq), 0) * CMP_STRIDE + (CMP_LEN - 1)).astype(F32)
        for hh in range(NSA_HEADS):
            bias_ref[hh] = (SLOPES[hh] * LOG2E) * end_n

    q_t = (q_ref[...] * (DK ** -0.5 * LOG2E)).T.astype(BF16)
    for hh in range(NSA_HEADS):
        pair = hh // 2
        s_ref[hh] = _dot(kz_ref[(hh // HPG) * 2 + hh % 2], q_t[pair * LANES:(pair + 1) * LANES, :])

    n_row = lax.broadcasted_iota(jnp.int32, (nc, tq), 0)
    t_col = lax.broadcasted_iota(jnp.int32, (nc, tq), 1) + t0
    valid = t_col >= n_row * CMP_STRIDE + (CMP_LEN - 1)
    any_valid = jnp.where(t_col[0:1, :] >= CMP_LEN - 1, 1.0, 0.0)
    jj = lax.broadcasted_iota(jnp.int32, (nslc, nc), 0) * SLC_BLOCK
    nn = lax.broadcasted_iota(jnp.int32, (nslc, nc), 1) * CMP_STRIDE
    overlap_t = jnp.where((nn < jj + SLC_BLOCK) & (jj < nn + CMP_LEN), 1.0, 0.0)
    blk = lax.broadcasted_iota(jnp.int32, (nslc, tq), 0)
    t_blk = (lax.broadcasted_iota(jnp.int32, (nslc, tq), 1) + t0) >> SLC_SHIFT
    forced = (blk == 0) | (blk == t_blk) | (blk == t_blk - 1)
    future = blk > t_blk
    sub = lax.broadcasted_iota(jnp.int32, (8, tq), 0)
    ngrp = nslc // 8
    gate_t = _sigmoid(gz_ref[...].T)
    ones = jnp.ones((8, tq), BF16)

    outs = []
    for g in range(NSA_KV_HEADS):
        vct = vct_ref[g // 2, (g % 2) * DK:(g % 2 + 1) * DK, :]
        psum = jnp.zeros((nc, tq), F32)
        for h in range(HPG):
            hh = g * HPG + h
            s = jnp.where(valid, s_ref[hh] + bias_ref[hh], NEG_INF)
            e = jnp.exp2(s - jnp.max(s, axis=0, keepdims=True))
            p = e * (any_valid / jnp.sum(e, axis=0, keepdims=True))
            outs.append(_dot(vct, p.astype(BF16)) * gate_t[3 * hh:3 * hh + 1, :])
            psum = psum + p
        imp_t = _dot(overlap_t, psum, precision=lax.Precision.HIGHEST)
        score = jnp.where(future, -1.0, jnp.where(forced, 1e4, imp_t))

        sc = [score[8 * r:8 * r + 8] for r in range(ngrp)]
        rank = [jnp.zeros((8, tq), F32) for _ in range(ngrp)]
        for i in range(nslc):
            si = jnp.broadcast_to(score[i:i + 1, :], (8, tq))
            for r in range(ngrp):
                if r < i // 8:
                    before = si > sc[r]
                elif r > i // 8:
                    before = si >= sc[r]
                else:
                    before = (si > sc[r]) | ((sub > i % 8) & (si >= sc[r]))
                rank[r] = rank[r] + jnp.where(before, 1.0, 0.0)
        for r in range(ngrp):
            selt_ref[g, 8 * r:8 * r + 8, :] = jnp.where(rank[r] < float(nsel), 1.0, 0.0)
        if g % 2:
            picks = (selt_ref[g - 1] + selt_ref[g]).astype(BF16)
            per_block = jnp.where(_dot_nt(ones, picks) > 0.0, 1.0, 0.0).astype(BF16)
            blk_i = lax.broadcasted_iota(jnp.int32, (nslc, nslc // nb), 0)
            tile_i = lax.broadcasted_iota(jnp.int32, (nslc, nslc // nb), 1)
            pool = jnp.where(blk_i >> (nb.bit_length() - 1) == tile_i, 1.0, 0.0).astype(BF16)
            cnt_ref[g // 2:g // 2 + 1, :] = _dot(per_block, pool)[0:1, :].astype(jnp.int32)

    for pair, tile in enumerate(_untranspose_pairs(jnp.concatenate(outs, axis=1), tq)):
        ocmp_ref[:, pair * LANES:(pair + 1) * LANES] = tile


def _cmp_attn(proj, cmp_kv, bsz, seq):
    tq = min(Q_TILE, seq)
    nc = cmp_kv.shape[1]
    nslc = seq // SLC_BLOCK
    nsel = min(N_SELECT, nslc)
    nq = seq // tq
    return pl.pallas_call(
        functools.partial(_cmp_attn_kernel, tq=tq, nc=nc, nslc=nslc, nsel=nsel, nb=KEY_TILE // SLC_BLOCK),
        grid=(bsz, nq),
        in_specs=[pl.BlockSpec((tq, NSA_WIDTH), lambda b, i: (b * nq + i, 0)),
                  pl.BlockSpec((None, nc, 2 * NSA_KV_WIDTH), lambda b, i: (b, 0, 0)),
                  pl.BlockSpec((tq, LANES), lambda b, i: (b * nq + i, COL_GZ // LANES))],
        out_specs=[pl.BlockSpec((tq, NSA_WIDTH), lambda b, i: (b * nq + i, 0)),
                   pl.BlockSpec((NSA_KV_HEADS, None, nslc, tq), lambda b, i: (0, b, 0, i)),
                   pl.BlockSpec((None, None, NSA_KV_HEADS // 2, seq // KEY_TILE), lambda b, i: (b, i, 0, 0))],
        out_shape=[jax.ShapeDtypeStruct((bsz * seq, NSA_WIDTH), F32),
                   jax.ShapeDtypeStruct((NSA_KV_HEADS, bsz, nslc, seq), F32),
                   jax.ShapeDtypeStruct((bsz, nq, NSA_KV_HEADS // 2, seq // KEY_TILE), jnp.int32)],
        scratch_shapes=[pltpu.VMEM((2 * NSA_KV_HEADS, nc, LANES), BF16),
                        pltpu.VMEM((NSA_KV_HEADS // 2, LANES, nc), BF16),
                        pltpu.VMEM((NSA_HEADS, nc, tq), F32),
                        pltpu.VMEM((NSA_HEADS, nc, tq), F32)],
        compiler_params=_params(("parallel", "arbitrary")),
        name="nsa_cmp_attn",
    )(proj, cmp_kv, proj)


def _stage_kv(k_ref, v_ref, kz_ref, vt_ref, nk, ts):
    for c in range(nk):
        padded = _pad_halves(k_ref[c * ts:(c + 1) * ts, :])
        for idx in range(4):
            kz_ref[idx, c] = padded[idx].astype(BF16)
        v_t = v_ref[c * ts:(c + 1) * ts, :].T
        ones_row = jnp.where(lax.broadcasted_iota(jnp.int32, (VT_ROWS - DK, ts), 0) == 0, 1.0, 0.0)
        for gi in range(2):
            vt_ref[gi, c, 0:DK, :] = v_t[gi * DK:(gi + 1) * DK, :].astype(BF16)
            vt_ref[gi, c, DK:VT_ROWS, :] = ones_row.astype(BF16)


def _attend(mode, nsub, gp, t0, slopes_ref, cnt_ref, selt_ref, qb_ref, bias_ref, kz_ref, vt_ref, s_ref,
            m_ref, acc_ref, idx_ref, *, tq, ts, nk):
    nb = ts // SLC_BLOCK
    ts_shift = ts.bit_length() - 1
    m_ref[...] = jnp.full(m_ref.shape, NEG_INF, F32)
    acc_ref[...] = jnp.zeros(acc_ref.shape, F32)

    last_sub = (t0 + tq - 1) >> ts_shift
    if mode == "sel":
        def build(j, n):
            idx_ref[n] = j
            return n + jnp.where(((cnt_ref[gp, j] > 0) | (j == last_sub)) & (j <= last_sub), 1, 0)

        n_live = lax.fori_loop(0, nk, build, 0, unroll=4)
    else:
        first_sub = jnp.maximum(t0 - (WINDOW - 1), 0) >> ts_shift
        n_live = last_sub + 1 - first_sub
        for u in range(nsub):
            idx_ref[u] = jnp.minimum(first_sub + u, last_sub)

    key_i = lax.broadcasted_iota(jnp.int32, (ts, tq), 0)
    qry_t = lax.broadcasted_iota(jnp.int32, (ts, tq), 1) + t0

    def sub_tile(step, u):
        pos = step * nsub + u
        return idx_ref[jnp.minimum(pos, n_live - 1)], pos < n_live

    def scores(step):
        for u in range(nsub):
            kj, _ = sub_tile(step, u)
            for gi in range(2):
                for h in range(HPG):
                    pair = gi * (HPG // 2) + h // 2
                    s_ref[u * 2 + gi, :, h * tq:(h + 1) * tq] = _dot(
                        kz_ref[gi * 2 + h % 2, kj], qb_ref[pair * LANES:(pair + 1) * LANES, :])

    def softmax_update(step):
        for u in range(nsub):
            kj, live = sub_tile(step, u)
            j0 = kj * ts
            rel0 = (jnp.zeros((1, tq), jnp.int32) + (j0 - t0)).astype(F32)
            dd = qry_t - (key_i + j0)
            lowest = jnp.where(live, 0, 1 << 30)
            for gi in range(2):
                if mode == "sel":
                    chosen = selt_ref[gi, pl.ds(kj * nb + nb - 1, 1), :]
                    for c in range(nb - 2, -1, -1):
                        chosen = jnp.where(key_i < (c + 1) * SLC_BLOCK,
                                           selt_ref[gi, pl.ds(kj * nb + c, 1), :], chosen)
                    mask = (chosen > 0.5) & (dd >= lowest)
                else:
                    mask = (dd >= lowest) & (dd < WINDOW)
                ps = []
                alphas = []
                for h in range(HPG):
                    shift = (slopes_ref[(gp * 2 + gi) * HPG + h] * LOG2E) * rel0
                    cols = slice(h * tq, (h + 1) * tq)
                    x = jnp.where(mask, s_ref[u * 2 + gi, :, cols] + bias_ref[gi * HPG + h], MASKED)
                    m_old = m_ref[gi, :, cols]
                    m_new = jnp.maximum(m_old, jnp.max(x, axis=0, keepdims=True) + shift)
                    p = jnp.exp2(x - (m_new - shift))
                    m_ref[gi, :, cols] = m_new
                    ps.append(p.astype(BF16))
                    alphas.append(jnp.exp2(m_old - m_new))
                pv = _dot(vt_ref[gi, kj], jnp.concatenate(ps, axis=1))
                acc_ref[gi] = acc_ref[gi] * jnp.concatenate(alphas, axis=1) + pv

    def body(step, carry):
        scores(step)
        softmax_update(step)
        return carry

    if mode == "sel":
        assert nsub & (nsub - 1) == 0
        lax.fori_loop(0, (n_live + nsub - 1) >> (nsub.bit_length() - 1), body, 0)
    else:
        body(0, 0)


def _flash_kernel(slopes_ref, cnt_ref, q_ref, ks_ref, vs_ref, kw_ref, vw_ref, gz_ref, selt_ref, o_ref,
                  kzs_ref, vts_ref, kzw_ref, vtw_ref, qb_ref, bias_ref, gate_ref, s_ref, m_ref, acc_ref, idx_ref,
                  *, tq, ts, nk, nsub_sel, nsub_win):
    gp = pl.program_id(1)
    qi = pl.program_id(2)
    t0 = qi * tq

    @pl.when(qi == 0)
    def _():
        _stage_kv(ks_ref, vs_ref, kzs_ref, vts_ref, nk, ts)
        _stage_kv(kw_ref, vw_ref, kzw_ref, vtw_ref, nk, ts)

    key_f = lax.broadcasted_iota(jnp.int32, (ts, tq), 0).astype(F32)
    qb_ref[...] = (q_ref[...] * (DK ** -0.5 * LOG2E)).T.astype(BF16)
    gate_ref[...] = _sigmoid(gz_ref[...].T)
    for gh in range(2 * HPG):
        bias_ref[gh] = (slopes_ref[gp * 2 * HPG + gh] * LOG2E) * key_f

    common = dict(tq=tq, ts=ts, nk=nk)
    _attend("sel", nsub_sel, gp, t0, slopes_ref, cnt_ref, selt_ref, qb_ref, bias_ref, kzs_ref, vts_ref, s_ref,
            m_ref.at[0], acc_ref.at[0], idx_ref, **common)
    _attend("win", nsub_win, gp, t0, slopes_ref, None, None, qb_ref, bias_ref, kzw_ref, vtw_ref, s_ref,
            m_ref.at[1], acc_ref.at[1], idx_ref, **common)

    outs = []
    for gi in range(2):
        branch_out = [acc_ref[br, gi, 0:DK, :] / acc_ref[br, gi, DK:DK + 1, :] for br in range(2)]
        for h in range(HPG):
            row = (gp * 2 * HPG + gi * HPG + h) * 3
            cols = slice(h * tq, (h + 1) * tq)
            outs.append(branch_out[0][:, cols] * gate_ref[pl.ds(row + 1, 1), :]
                        + branch_out[1][:, cols] * gate_ref[pl.ds(row + 2, 1), :])
    for pair, tile in enumerate(_untranspose_pairs(jnp.concatenate(outs, axis=1), tq)):
        o_ref[:, pair * LANES:(pair + 1) * LANES] = tile


def _flash(proj, selt, cnt, slopes, bsz, seq):
    tq = min(Q_TILE, seq)
    ts = KEY_TILE
    nsub_sel = 4
    nsub_win = (WINDOW + tq) // ts
    nq = seq // tq
    nk = seq // ts
    nslc = seq // SLC_BLOCK
    first_kv = (COL_KV + 2 * NSA_KV_WIDTH) // LANES
    kv_spec = lambda which: pl.BlockSpec((seq, LANES), lambda b, g, i: (b, first_kv + 2 * which + g))
    smem = pltpu.SMEM
    max_units = 2 * max(nsub_sel, nsub_win)
    return pl.pallas_call(
        functools.partial(_flash_kernel, tq=tq, ts=ts, nk=nk, nsub_sel=nsub_sel, nsub_win=nsub_win),
        grid=(bsz, 2, nq),
        in_specs=[pl.BlockSpec(memory_space=smem),
                  pl.BlockSpec((None, None, 2, nk), lambda b, g, i: (b, i, 0, 0), memory_space=smem),
                  pl.BlockSpec((tq, 2 * HPG * DK), lambda b, g, i: (b * nq + i, g)),
                  kv_spec(0), kv_spec(1), kv_spec(2), kv_spec(3),
                  pl.BlockSpec((tq, LANES), lambda b, g, i: (b * nq + i, COL_GZ // LANES)),
                  pl.BlockSpec((2, None, nslc, tq), lambda b, g, i: (g, b, 0, i))],
        out_specs=pl.BlockSpec((tq, 2 * HPG * DK), lambda b, g, i: (b * nq + i, g)),
        out_shape=jax.ShapeDtypeStruct((bsz * seq, NSA_WIDTH), F32),
        scratch_shapes=[pltpu.VMEM((4, nk, ts, LANES), BF16),
                        pltpu.VMEM((2, nk, VT_ROWS, ts), BF16),
                        pltpu.VMEM((4, nk, ts, LANES), BF16),
                        pltpu.VMEM((2, nk, VT_ROWS, ts), BF16),
                        pltpu.VMEM((2 * HPG * DK, tq), BF16),
                        pltpu.VMEM((2 * HPG, ts, tq), F32),
                        pltpu.VMEM((LANES, tq), F32),
                        pltpu.VMEM((max_units, ts, HPG * tq), F32),
                        pltpu.VMEM((2, 2, 1, HPG * tq), F32),
                        pltpu.VMEM((2, 2, VT_ROWS, HPG * tq), F32),
                        pltpu.SMEM((nk + max_units,), jnp.int32)],
        compiler_params=_params(("parallel", "parallel", "arbitrary")),
        name="nsa_sel_win",
    )(slopes, cnt, proj, proj, proj, proj, proj, proj, selt)


def _lru_kernel(x_ref, y_ref, cw_ref, cb_ref, wa_ref, ba_ref, wi_ref, bi_ref, lam_ref, o_ref,
                tail_ref, h_ref, *, tt):
    @pl.when(pl.program_id(1) == 0)
    def _():
        tail_ref[...] = jnp.zeros_like(tail_ref)
        h_ref[...] = jnp.zeros_like(h_ref)

    x = x_ref[...]
    ext = jnp.concatenate([tail_ref[...], x], axis=0)
    xc = cb_ref[...] + x * cw_ref[CONV_WIDTH - 1:CONV_WIDTH, :]
    for k in range(CONV_WIDTH - 1):
        back = CONV_WIDTH - 1 - k
        xc = xc + ext[8 - back:8 - back + tt, :] * cw_ref[k:k + 1, :]
    tail_ref[...] = x[tt - 8:, :]

    xcb = xc.astype(BF16)
    r = _sigmoid(_dot(xcb, wa_ref[...]) + ba_ref[...])
    gate_i = _sigmoid(_dot(xcb, wi_ref[...]) + bi_ref[...])
    neg_lam = -lam_ref[...]
    softplus = jnp.maximum(neg_lam, 0.0) + jnp.log1p(jnp.exp(-jnp.abs(neg_lam)))
    log_a = -LRU_C * r * softplus
    a = jnp.exp(log_a)
    u = jnp.sqrt(jnp.tanh(-log_a) * (a * a + 1.0)) * (gate_i * xc)

    sub = min(LRU_SCAN_ROWS, tt)
    row_in_sub = lax.broadcasted_iota(jnp.int32, (tt, 1), 0) & (sub - 1)
    step = 1
    while step < sub:
        keep = row_in_sub >= step
        a_prev = jnp.where(keep, pltpu.roll(a, step, 0), 1.0)
        u_prev = jnp.where(keep, pltpu.roll(u, step, 0), 0.0)
        u = u + a * u_prev
        a = a * a_prev
        step *= 2
    carry = h_ref[...]
    y = y_ref[...]
    for k in range(tt // sub):
        rows = slice(k * sub, (k + 1) * sub)
        hs = u[rows] + a[rows] * carry
        carry = hs[sub - 1:sub, :]
        o_ref[rows, :] = hs * jax.nn.gelu(y[rows], approximate=True)
    h_ref[...] = carry


def _lru(proj, cw, cb, wa, ba, wi, bi, lam, bsz, seq):
    tt = min(1024, seq)
    nt = seq // tt
    w = LRU_WIDTH
    xcol = COL_LRU_X // w
    ycol = COL_LRU_Y // w
    vec = pl.BlockSpec((1, w), lambda b, i: (0, 0))
    mat = pl.BlockSpec((w, w), lambda b, i: (0, 0))
    return pl.pallas_call(
        functools.partial(_lru_kernel, tt=tt),
        grid=(bsz, nt),
        in_specs=[pl.BlockSpec((tt, w), lambda b, i: (b * nt + i, xcol)),
                  pl.BlockSpec((tt, w), lambda b, i: (b * nt + i, ycol)),
                  pl.BlockSpec((CONV_WIDTH, w), lambda b, i: (0, 0)),
                  vec, mat, vec, mat, vec, vec],
        out_specs=pl.BlockSpec((tt, w), lambda b, i: (b * nt + i, 0)),
        out_shape=jax.ShapeDtypeStruct((bsz * seq, w), F32),
        scratch_shapes=[pltpu.VMEM((8, w), F32), pltpu.VMEM((1, w), F32)],
        compiler_params=_params(("parallel", "arbitrary")),
        name="rglru",
    )(proj, proj, cw, cb, wa, ba, wi, bi, lam)


def _gla_kernel(q_ref, k_ref, v_ref, r_ref, gz_ref, w2_ref, bg_ref, ng_ref, o_ref, st_ref, *, tt):
    c = GLA_CHUNK
    dh = GLA_HEAD_DIM

    @pl.when(pl.program_id(1) == 0)
    def _():
        st_ref[...] = jnp.zeros_like(st_ref)

    gate = _dot(gz_ref[...].astype(BF16), w2_ref[...]) + bg_ref[...]
    log_alpha = (jnp.minimum(gate, 0.0) - jnp.log1p(jnp.exp(-jnp.abs(gate)))) / GLA_GATE_TAU
    causal = (lax.broadcasted_iota(jnp.int32, (c, c), 1) <= lax.broadcasted_iota(jnp.int32, (c, c), 0))
    tri = jnp.where(causal, 1.0, 0.0)

    for n in range(tt // c):
        rs = slice(n * c, (n + 1) * c)
        bcum = _dot(tri, log_alpha[rs], precision=lax.Precision.HIGHEST)
        b_last = bcum[c - 1:c, :]
        e_pos = jnp.exp(bcum)
        q_t = (q_ref[rs, :] * dh ** -0.5) * e_pos
        kk = k_ref[rs, :]
        k_t = kk * jnp.exp(-bcum)
        k_end = kk * jnp.exp(b_last - bcum)
        decay = jnp.exp(b_last)
        vv = v_ref[rs, :]
        for hh in range(GLA_HEADS):
            cs = slice(hh * dh, (hh + 1) * dh)
            qh = q_t[:, cs].astype(BF16)
            vh = vv[:, cs]
            att = jnp.where(causal, _dot_nt(qh, k_t[:, cs].astype(BF16)), 0.0)
            st = st_ref[hh]
            o = _dot(att.astype(BF16), vh.astype(BF16)) + _dot_nt(qh, st.astype(BF16))
            st_ref[hh] = st * decay[:, cs] + _dot(vh.T.astype(BF16), k_end[:, cs].astype(BF16))
            o = _rms_scale(o) * ng_ref[:, cs]
            rr = r_ref[rs, cs]
            o_ref[rs, cs] = o * (rr * _sigmoid(rr))


def _gla(proj, w2pad, bg, ng, bsz, seq):
    tt = min(512, seq)
    nt = seq // tt
    w = GLA_WIDTH
    cols = [COL_GLA_Q // w, COL_GLA_K // w, COL_GLA_V // w, COL_GLA_R // w]
    seg = [pl.BlockSpec((tt, w), functools.partial(lambda b, i, cc: (b * nt + i, cc), cc=cc)) for cc in cols]
    vec = pl.BlockSpec((1, w), lambda b, i: (0, 0))
    return pl.pallas_call(
        functools.partial(_gla_kernel, tt=tt),
        grid=(bsz, nt),
        in_specs=seg + [pl.BlockSpec((tt, LANES), lambda b, i: (b * nt + i, COL_GZ // LANES)),
                        pl.BlockSpec((LANES, w), lambda b, i: (0, 0)), vec, vec],
        out_specs=pl.BlockSpec((tt, w), lambda b, i: (b * nt + i, 0)),
        out_shape=jax.ShapeDtypeStruct((bsz * seq, w), F32),
        scratch_shapes=[pltpu.VMEM((GLA_HEADS, GLA_HEAD_DIM, GLA_HEAD_DIM), F32)],
        compiler_params=_params(("parallel", "arbitrary")),
        name="gla",
    )(proj, proj, proj, proj, proj, w2pad, bg, ng)


def _out_proj_kernel(ocmp_ref, oselwin_ref, olru_ref, ogla_ref, h_ref, w_ref, g_ref, o_ref, wb_ref):
    @pl.when(pl.program_id(0) == 0)
    def _():
        wb_ref[...] = w_ref[...].astype(BF16)

    nsa = ocmp_ref[...] + oselwin_ref[...]
    y = _dot(nsa.astype(BF16), wb_ref[0:NSA_WIDTH, :])
    y = y + _dot(olru_ref[...].astype(BF16), wb_ref[NSA_WIDTH:NSA_WIDTH + LRU_WIDTH, :])
    y = y + _dot(ogla_ref[...].astype(BF16), wb_ref[NSA_WIDTH + LRU_WIDTH:, :])
    o_ref[...] = h_ref[...] + _rms_scale(y) * g_ref[...]


def _resident(shape, layer):
    zeros = (0,) * len(shape)
    return pl.BlockSpec((None,) + shape, lambda i: (layer,) + zeros, pipeline_mode=pl.Buffered(1))


def _out_proj(ocmp, oselwin, olru, ogla, h2d, w_out, g, layer):
    m = h2d.shape[0]
    tm = min(256, m)
    row = lambda width: pl.BlockSpec((tm, width), lambda i: (i, 0))
    return pl.pallas_call(
        _out_proj_kernel,
        grid=(m // tm,),
        in_specs=[row(NSA_WIDTH), row(NSA_WIDTH), row(LRU_WIDTH), row(GLA_WIDTH), row(D_MODEL),
                  _resident((D_MODEL, D_MODEL), layer),
                  pl.BlockSpec((1, D_MODEL), lambda i: (0, 0))],
        out_specs=row(D_MODEL),
        out_shape=jax.ShapeDtypeStruct((m, D_MODEL), F32),
        scratch_shapes=[pltpu.VMEM((D_MODEL, D_MODEL), BF16)],
        compiler_params=_params(("arbitrary",)),
        name="out_proj",
    )(ocmp, oselwin, olru, ogla, h2d, w_out, g)


def _mlp_kernel(h_ref, gpre_ref, wu_ref, wd_ref, gpost_ref, o_ref, un_ref):
    f = pl.program_id(1)

    @pl.when(f == 0)
    def _():
        un_ref[...] = (_rms_scale(h_ref[...]) * gpre_ref[...]).astype(BF16)
        o_ref[...] = jnp.zeros_like(o_ref)

    a = jnp.maximum(_dot(un_ref[...], wu_ref[...].astype(BF16)), 0.0)
    o_ref[...] += _dot((a * a).astype(BF16), wd_ref[...].astype(BF16))

    @pl.when(f == pl.num_programs(1) - 1)
    def _():
        o_ref[...] = h_ref[...] + _rms_scale(o_ref[...]) * gpost_ref[...]


def _mlp(h2d, gpre, w_up, w_down, gpost, layer):
    m = h2d.shape[0]
    tm = min(1024, m)
    tf = 512
    vec = pl.BlockSpec((1, D_MODEL), lambda i, f: (0, 0))
    return pl.pallas_call(
        _mlp_kernel,
        grid=(m // tm, D_FF // tf),
        in_specs=[pl.BlockSpec((tm, D_MODEL), lambda i, f: (i, 0)), vec,
                  pl.BlockSpec((None, D_MODEL, tf), lambda i, f: (layer, 0, f)),
                  pl.BlockSpec((None, tf, D_MODEL), lambda i, f: (layer, f, 0)), vec],
        out_specs=pl.BlockSpec((tm, D_MODEL), lambda i, f: (i, 0), pipeline_mode=pl.Buffered(1)),
        out_shape=jax.ShapeDtypeStruct((m, D_MODEL), F32),
        scratch_shapes=[pltpu.VMEM((tm, D_MODEL), BF16)],
        compiler_params=_params(("parallel", "arbitrary")),
        name="mlp",
    )(h2d, gpre, w_up, w_down, gpost)


def _ple_kernel(h_ref, p_ref, wg_ref, wp_ref, o_ref, wgb_ref, wpb_ref):
    @pl.when(pl.program_id(0) == 0)
    def _():
        wgb_ref[...] = wg_ref[...].astype(BF16)
        wpb_ref[...] = wp_ref[...].astype(BF16)

    h = h_ref[...]
    gate = _sigmoid(_dot(h.astype(BF16), wgb_ref[...]))
    o_ref[...] = h + gate * _dot(p_ref[...].astype(BF16), wpb_ref[...])


def _ple(h2d, p_all, w_gate, w_ple, layer):
    m = h2d.shape[0]
    tm = min(512, m)
    return pl.pallas_call(
        _ple_kernel,
        grid=(m // tm,),
        in_specs=[pl.BlockSpec((tm, D_MODEL), lambda i: (i, 0)),
                  pl.BlockSpec((None, tm, PLE_DIM), lambda i: (layer, i, 0)),
                  _resident((D_MODEL, D_MODEL), layer),
                  _resident((PLE_DIM, D_MODEL), layer)],
        out_specs=pl.BlockSpec((tm, D_MODEL), lambda i: (i, 0)),
        out_shape=jax.ShapeDtypeStruct((m, D_MODEL), F32),
        scratch_shapes=[pltpu.VMEM((D_MODEL, D_MODEL), BF16), pltpu.VMEM((PLE_DIM, D_MODEL), BF16)],
        compiler_params=_params(("arbitrary",)),
        name="ple",
    )(h2d, p_all, w_gate, w_ple)


def _w_in_views(w_in):
    w_t = jnp.swapaxes(w_in, 1, 2)
    g0 = NSA_WIDTH + 6 * NSA_KV_WIDTH
    z0 = D_IN - GLA_GATE_RANK
    w_gz = jnp.concatenate([w_t[:, g0:g0 + N_GATES], w_t[:, z0:]], axis=1)
    return w_t, w_gz


def _compress_weights(cmp_w, cmp_pe):
    w4 = cmp_w.reshape(2, CMP_LEN, DK, DK)
    eye = jnp.eye(2, dtype=cmp_w.dtype)
    wbd = jnp.einsum("clde,xy->clxdye", w4, eye).reshape(2, CMP_LEN, LANES, LANES).astype(BF16)
    pe = jnp.tile(cmp_pe, (1, 1, 2)).reshape(2, CMP_LEN, 1, LANES)
    return pe, wbd


def _block_diag(w):
    eye = jnp.eye(LRU_BLOCKS, dtype=w.dtype)
    return jnp.einsum("ncd,nm->ncmd", w, eye).reshape(LRU_WIDTH, LRU_WIDTH).astype(BF16)


def _layer(h2d, bsz, seq, layer, stacked, norm_mix_pre, nsa_cmp_w, nsa_cmp_pe, lru_conv_w, lru_conv_b,
           lru_wa, lru_ba, lru_wi, lru_bi, lru_lambda, gla_w_gate2, gla_b_gate, gla_norm, norm_mix_post,
           norm_mlp_pre, norm_mlp_post):
    w_in_all, w_out_all, w_up_all, w_down_all, w_ple_gate_all, w_ple_all, p_all = stacked
    row = lambda v: v.reshape(1, -1)
    proj = _in_proj(h2d, row(norm_mix_pre), *w_in_all, layer)

    cmp_kv = _compress(proj, *_compress_weights(nsa_cmp_w, nsa_cmp_pe), bsz, seq)
    o_cmp, selt, cnt = _cmp_attn(proj, cmp_kv, bsz, seq)
    slopes = jnp.asarray(SLOPES, F32)
    o_sel_win = _flash(proj, selt, cnt, slopes, bsz, seq)

    o_lru = _lru(proj, lru_conv_w, row(lru_conv_b), _block_diag(lru_wa), row(lru_ba),
                 _block_diag(lru_wi), row(lru_bi), row(lru_lambda), bsz, seq)

    w2pad = jnp.zeros((LANES, GLA_WIDTH), F32).at[N_GATES:N_GATES + GLA_GATE_RANK].set(gla_w_gate2)
    o_gla = _gla(proj, w2pad.astype(BF16), row(gla_b_gate), row(gla_norm), bsz, seq)

    h2d = _out_proj(o_cmp, o_sel_win, o_lru, o_gla, h2d, w_out_all, row(norm_mix_post), layer)
    h2d = _mlp(h2d, row(norm_mlp_pre), w_up_all, w_down_all, row(norm_mlp_post), layer)
    return _ple(h2d, p_all, w_ple_gate_all, w_ple_all, layer)


def kernel(x, p, norm_mix_pre, w_in, nsa_cmp_w, nsa_cmp_pe, lru_conv_w, lru_conv_b, lru_wa, lru_ba,
           lru_wi, lru_bi, lru_lambda, gla_w_gate2, gla_b_gate, gla_norm, w_out, norm_mix_post,
           norm_mlp_pre, w_up, w_down, norm_mlp_post, w_ple_gate, w_ple):
    bsz, seq, _ = x.shape
    h2d = x.reshape(bsz * seq, D_MODEL)
    small = (norm_mix_pre, nsa_cmp_w, nsa_cmp_pe, lru_conv_w, lru_conv_b, lru_wa, lru_ba, lru_wi, lru_bi,
             lru_lambda, gla_w_gate2, gla_b_gate, gla_norm, norm_mix_post, norm_mlp_pre, norm_mlp_post)
    depth = p.shape[0]
    stacked = (_w_in_views(w_in), w_out, w_up, w_down, w_ple_gate, w_ple,
               p.reshape(depth, bsz * seq, PLE_DIM))
    for i in range(depth):
        h2d = _layer(h2d, bsz, seq, i, stacked, *(w[i] for w in small))
    return h2d.reshape(bsz, seq, D_MODEL)
```

```python
import functools

import jax
import jax.numpy as jnp
from jax import lax
from jax.experimental import pallas as pl
from jax.experimental.pallas import tpu as pltpu

F32 = jnp.float32
BF16 = jnp.bfloat16

D_MODEL = 2048
PLE_DIM = 256
NSA_HEADS = 16
NSA_KV_HEADS = 4
HPG = NSA_HEADS // NSA_KV_HEADS
NSA_WIDTH = 1024
DK = 64
NSA_KV_WIDTH = NSA_KV_HEADS * DK
CMP_LEN = 32
CMP_STRIDE = 16
SLC_BLOCK = 64
SLC_SHIFT = 6
N_SELECT = 16
WINDOW = 512
LRU_WIDTH = 512
LRU_BLOCKS = 8
LRU_BLOCK_DIM = 64
CONV_WIDTH = 4
LRU_C = 8.0
LRU_SCAN_ROWS = 32
GLA_WIDTH = 512
GLA_HEADS = 4
GLA_HEAD_DIM = 128
GLA_GATE_RANK = 16
GLA_GATE_TAU = 16.0
GLA_CHUNK = 64
D_FF = 4 * D_MODEL
EPS = 1e-6
NEG_INF = -1e30
MASKED = 2.0 * NEG_INF

COL_Q = 0
COL_KV = 1024
COL_LRU_X = 2560
COL_LRU_Y = 3072
COL_GLA_Q = 3584
COL_GLA_K = 4096
COL_GLA_V = 4608
COL_GLA_R = 5120
COL_GZ = 5632
D_IN_PAD = 6144
N_GATES = 3 * NSA_HEADS
LANES = 128
KEY_TILE = 256
Q_TILE = 256
VT_ROWS = DK + 16
LOG2E = 1.4426950408889634

SLOPES = tuple(2.0 ** (-8.0 * i / NSA_HEADS) for i in range(1, NSA_HEADS + 1))

V7X_VMEM_BYTES = 64 * 1024 * 1024
VMEM_LIMIT = V7X_VMEM_BYTES - 8 * 1024 * 1024

NT_DIMS = (((1,), (1,)), ((), ()))
TN_DIMS = (((0,), (0,)), ((), ()))


def _params(sem):
    return pltpu.CompilerParams(dimension_semantics=sem, vmem_limit_bytes=VMEM_LIMIT)


def _dot(a, b, **kw):
    return jnp.dot(a, b, preferred_element_type=F32, **kw)


def _dot_nt(a, b, **kw):
    return lax.dot_general(a, b, NT_DIMS, preferred_element_type=F32, **kw)


def _dot_tn(a, b, **kw):
    return lax.dot_general(a, b, TN_DIMS, preferred_element_type=F32, **kw)


def _sigmoid(x):
    return 1.0 / (1.0 + jnp.exp(-x))


def _rms_scale(x):
    return x * lax.rsqrt(jnp.mean(x * x, axis=-1, keepdims=True) + EPS)


def _half_masks(rows):
    lane = lax.broadcasted_iota(jnp.int32, (rows, LANES), 1)
    return lane < DK, lane >= DK


def _pad_halves(blk):
    lo, hi = _half_masks(blk.shape[0])
    swapped = pltpu.roll(blk, DK, 1)
    zero = jnp.zeros_like(blk)
    return (jnp.where(lo, blk, zero), jnp.where(hi, swapped, zero),
            jnp.where(lo, swapped, zero), jnp.where(hi, blk, zero))


def _untranspose_pairs(o_t, tq):
    sub = lax.broadcasted_iota(jnp.int32, (4 * DK, LANES), 0)
    lane = lax.broadcasted_iota(jnp.int32, (4 * DK, LANES), 1)
    place = jnp.where(lane == (sub & (DK - 1)) + jnp.where(sub >= 2 * DK, DK, 0), 1.0, 0.0).astype(BF16)
    out = []
    for pair in range(o_t.shape[1] // (2 * tq)):
        parts = []
        for half in range(2):
            v = o_t[:, (2 * pair + half) * tq:(2 * pair + half + 1) * tq]
            v_hi = v.astype(BF16)
            parts += [v_hi, (v - v_hi.astype(F32)).astype(BF16)]
        out.append(_dot_tn(jnp.concatenate(parts, axis=0), place))
    return out


IN_TILE = 512
HEAD_TILES = (NSA_WIDTH + 6 * NSA_KV_WIDTH) // IN_TILE
TAIL_TILES = (2 * LRU_WIDTH + 4 * GLA_WIDTH) // IN_TILE
D_IN = NSA_WIDTH + 6 * NSA_KV_WIDTH + N_GATES + 2 * LRU_WIDTH + 4 * GLA_WIDTH + GLA_GATE_RANK


def _in_proj_kernel(x_ref, g_ref, w_ref, wgz_ref, o_ref, xn_ref, wres_ref):
    i = pl.program_id(0)
    j = pl.program_id(1)
    last = HEAD_TILES + TAIL_TILES
    n_gz = N_GATES + GLA_GATE_RANK
    k = wres_ref.shape[1]

    @pl.when(j == 0)
    def _():
        xn_ref[...] = (_rms_scale(x_ref[...]) * g_ref[...]).astype(BF16)

    @pl.when((i == 0) & (j < last))
    def _():
        wres_ref[j] = w_ref[0].T.astype(BF16)

    @pl.when((i == 0) & (j == last))
    def _():
        head = jnp.concatenate([wgz_ref[...], jnp.zeros((LANES - n_gz, k), F32)], axis=0)
        wres_ref[last, :, 0:LANES] = head.T.astype(BF16)
        wres_ref[last, :, LANES:] = jnp.zeros((k, IN_TILE - LANES), BF16)

    o_ref[...] = _dot(xn_ref[...], wres_ref[j])


def _in_proj(h2d, g, w_t, w_gz, layer):
    m, k = h2d.shape
    tm = min(1024, m)
    n_tiles = HEAD_TILES + TAIL_TILES + 1
    assert n_tiles * IN_TILE == D_IN_PAD

    def w_rows(i, j):
        start = jnp.where(j < HEAD_TILES, j * IN_TILE, j * IN_TILE + N_GATES)
        start = jnp.where(i == 0, jnp.minimum(start, D_IN - IN_TILE), D_IN - IN_TILE)
        return layer, pl.multiple_of(start, 8), 0

    return pl.pallas_call(
        _in_proj_kernel,
        grid=(m // tm, n_tiles),
        in_specs=[pl.BlockSpec((tm, k), lambda i, j: (i, 0), pipeline_mode=pl.Buffered(1)),
                  pl.BlockSpec((1, k), lambda i, j: (0, 0)),
                  pl.BlockSpec((pl.Element(1), pl.Element(IN_TILE), pl.Element(k)), w_rows),
                  pl.BlockSpec((None, N_GATES + GLA_GATE_RANK, k), lambda i, j: (layer, 0, 0))],
        out_specs=pl.BlockSpec((tm, IN_TILE), lambda i, j: (i, j)),
        out_shape=jax.ShapeDtypeStruct((m, D_IN_PAD), F32),
        scratch_shapes=[pltpu.VMEM((tm, k), BF16), pltpu.VMEM((n_tiles, k, IN_TILE), BF16)],
        compiler_params=_params(("arbitrary", "arbitrary")),
        name="in_proj",
    )(h2d, g, w_t, w_gz)


def _compress_kernel(x_ref, pe_ref, w_ref, o_ref, *, nc):
    first = jnp.zeros((nc, LANES), F32)
    second = jnp.zeros((nc, LANES), F32)
    for l in range(CMP_STRIDE):
        x = x_ref[pl.ds(l, nc, stride=CMP_STRIDE), :]
        first = first + _dot((x + pe_ref[l]).astype(BF16), w_ref[l])
        second = second + _dot((x + pe_ref[CMP_STRIDE + l]).astype(BF16), w_ref[CMP_STRIDE + l])
    o_ref[...] = first + pltpu.roll(second, nc - 1, 0)


def _compress(proj, pe, w, bsz, seq):
    nc = seq // CMP_STRIDE
    return pl.pallas_call(
        functools.partial(_compress_kernel, nc=nc),
        grid=(bsz, NSA_KV_HEADS),
        in_specs=[pl.BlockSpec((seq, LANES), lambda i, j: (i, COL_KV // LANES + j)),
                  pl.BlockSpec((None, CMP_LEN, 1, LANES), lambda i, j: (j // 2, 0, 0, 0)),
                  pl.BlockSpec((None, CMP_LEN, LANES, LANES), lambda i, j: (j // 2, 0, 0, 0))],
        out_specs=pl.BlockSpec((None, nc, LANES), lambda i, j: (i, 0, j)),
        out_shape=jax.ShapeDtypeStruct((bsz, nc, 2 * NSA_KV_WIDTH), F32),
        compiler_params=_params(("parallel", "parallel")),
        name="nsa_compress",
    )(proj, pe, w)


def _cmp_attn_kernel(q_ref, kv_ref, gz_ref, ocmp_ref, selt_ref, cnt_ref, kz_ref, vct_ref, bias_ref, s_ref,
                     *, tq, nc, nslc, nsel, nb):
    t0 = pl.program_id(1) * tq

    @pl.when(pl.program_id(1) == 0)
    def _():
        for gp in range(NSA_KV_HEADS // 2):
            padded = _pad_halves(kv_ref[:, gp * LANES:(gp + 1) * LANES])
            for idx in range(4):
                kz_ref[gp * 4 + idx] = padded[idx].astype(BF16)
            vct_ref[gp] = kv_ref[:, NSA_KV_WIDTH + gp * LANES:NSA_KV_WIDTH + (gp + 1) * LANES].T.astype(BF16)
        end_n = (lax.broadcasted_iota(jnp.int32, (nc, tq), 0) * CMP_STRIDE + (CMP_LEN - 1)).astype(F32)
        for hh in range(NSA_HEADS):
            bias_ref[hh] = (SLOPES[hh] * LOG2E) * end_n

    q_t = (q_ref[...] * (DK ** -0.5 * LOG2E)).T.astype(BF16)
    for hh in range(NSA_HEADS):
        pair = hh // 2
        s_ref[hh] = _dot(kz_ref[(hh // HPG) * 2 + hh % 2], q_t[pair * LANES:(pair + 1) * LANES, :])

    n_row = lax.broadcasted_iota(jnp.int32, (nc, tq), 0)
    t_col = lax.broadcasted_iota(jnp.int32, (nc, tq), 1) + t0
    valid = t_col >= n_row * CMP_STRIDE + (CMP_LEN - 1)
    any_valid = jnp.where(t_col[0:1, :] >= CMP_LEN - 1, 1.0, 0.0)
    jj = lax.broadcasted_iota(jnp.int32, (nslc, nc), 0) * SLC_BLOCK
    nn = lax.broadcasted_iota(jnp.int32, (nslc, nc), 1) * CMP_STRIDE
    overlap_t = jnp.where((nn < jj + SLC_BLOCK) & (jj < nn + CMP_LEN), 1.0, 0.0)
    blk = lax.broadcasted_iota(jnp.int32, (nslc, tq), 0)
    t_blk = (lax.broadcasted_iota(jnp.int32, (nslc, tq), 1) + t0) >> SLC_SHIFT
    forced = (blk == 0) | (blk == t_blk) | (blk == t_blk - 1)
    future = blk > t_blk
    sub = lax.broadcasted_iota(jnp.int32, (8, tq), 0)
    ngrp = nslc // 8
    gate_t = _sigmoid(gz_ref[...].T)
    ones = jnp.ones((8, tq), BF16)

    outs = []
    for g in range(NSA_KV_HEADS):
        vct = vct_ref[g // 2, (g % 2) * DK:(g % 2 + 1) * DK, :]
        psum = jnp.zeros((nc, tq), F32)
        for h in range(HPG):
            hh = g * HPG + h
            s = jnp.where(valid, s_ref[hh] + bias_ref[hh], NEG_INF)
            e = jnp.exp2(s - jnp.max(s, axis=0, keepdims=True))
            p = e * (any_valid / jnp.sum(e, axis=0, keepdims=True))
            outs.append(_dot(vct, p.astype(BF16)) * gate_t[3 * hh:3 * hh + 1, :])
            psum = psum + p
        imp_t = _dot(overlap_t, psum, precision=lax.Precision.HIGHEST)
        score = jnp.where(future, -1.0, jnp.where(forced, 1e4, imp_t))

        sc = [score[8 * r:8 * r + 8] for r in range(ngrp)]
        rank = [jnp.zeros((8, tq), F32) for _ in range(ngrp)]
        for i in range(nslc):
            si = jnp.broadcast_to(score[i:i + 1, :], (8, tq))
            for r in range(ngrp):
                if r < i // 8:
                    before = si > sc[r]
                elif r > i // 8:
                    before = si >= sc[r]
                else:
                    before = (si > sc[r]) | ((sub > i % 8) & (si >= sc[r]))
                rank[r] = rank[r] + jnp.where(before, 1.0, 0.0)
        for r in range(ngrp):
            selt_ref[g, 8 * r:8 * r + 8, :] = jnp.where(rank[r] < float(nsel), 1.0, 0.0)
        if g % 2:
            picks = (selt_ref[g - 1] + selt_ref[g]).astype(BF16)
            per_block = jnp.where(_dot_nt(ones, picks) > 0.0, 1.0, 0.0).astype(BF16)
            blk_i = lax.broadcasted_iota(jnp.int32, (nslc, nslc // nb), 0)
            tile_i = lax.broadcasted_iota(jnp.int32, (nslc, nslc // nb), 1)
            pool = jnp.where(blk_i >> (nb.bit_length() - 1) == tile_i, 1.0, 0.0).astype(BF16)
            cnt_ref[g // 2:g // 2 + 1, :] = _dot(per_block, pool)[0:1, :].astype(jnp.int32)

    for pair, tile in enumerate(_untranspose_pairs(jnp.concatenate(outs, axis=1), tq)):
        ocmp_ref[:, pair * LANES:(pair + 1) * LANES] = tile


def _cmp_attn(proj, cmp_kv, bsz, seq):
    tq = min(Q_TILE, seq)
    nc = cmp_kv.shape[1]
    nslc = seq // SLC_BLOCK
    nsel = min(N_SELECT, nslc)
    nq = seq // tq
    return pl.pallas_call(
        functools.partial(_cmp_attn_kernel, tq=tq, nc=nc, nslc=nslc, nsel=nsel, nb=KEY_TILE // SLC_BLOCK),
        grid=(bsz, nq),
        in_specs=[pl.BlockSpec((tq, NSA_WIDTH), lambda b, i: (b * nq + i, 0)),
                  pl.BlockSpec((None, nc, 2 * NSA_KV_WIDTH), lambda b, i: (b, 0, 0)),
                  pl.BlockSpec((tq, LANES), lambda b, i: (b * nq + i, COL_GZ // LANES))],
        out_specs=[pl.BlockSpec((tq, NSA_WIDTH), lambda b, i: (b * nq + i, 0)),
                   pl.BlockSpec((NSA_KV_HEADS, None, nslc, tq), lambda b, i: (0, b, 0, i)),
                   pl.BlockSpec((None, None, NSA_KV_HEADS // 2, seq // KEY_TILE), lambda b, i: (b, i, 0, 0))],
        out_shape=[jax.ShapeDtypeStruct((bsz * seq, NSA_WIDTH), F32),
                   jax.ShapeDtypeStruct((NSA_KV_HEADS, bsz, nslc, seq), F32),
                   jax.ShapeDtypeStruct((bsz, nq, NSA_KV_HEADS // 2, seq // KEY_TILE), jnp.int32)],
        scratch_shapes=[pltpu.VMEM((2 * NSA_KV_HEADS, nc, LANES), BF16),
                        pltpu.VMEM((NSA_KV_HEADS // 2, LANES, nc), BF16),
                        pltpu.VMEM((NSA_HEADS, nc, tq), F32),
                        pltpu.VMEM((NSA_HEADS, nc, tq), F32)],
        compiler_params=_params(("parallel", "arbitrary")),
        name="nsa_cmp_attn",
    )(proj, cmp_kv, proj)


def _stage_kv(k_ref, v_ref, kz_ref, vt_ref, nk, ts):
    for c in range(nk):
        padded = _pad_halves(k_ref[c * ts:(c + 1) * ts, :])
        for idx in range(4):
            kz_ref[idx, c] = padded[idx].astype(BF16)
        v_t = v_ref[c * ts:(c + 1) * ts, :].T
        ones_row = jnp.where(lax.broadcasted_iota(jnp.int32, (VT_ROWS - DK, ts), 0) == 0, 1.0, 0.0)
        for gi in range(2):
            vt_ref[gi, c, 0:DK, :] = v_t[gi * DK:(gi + 1) * DK, :].astype(BF16)
            vt_ref[gi, c, DK:VT_ROWS, :] = ones_row.astype(BF16)


def _attend(mode, nsub, gp, t0, slopes_ref, cnt_ref, selt_ref, qb_ref, bias_ref, kz_ref, vt_ref, s_ref,
            m_ref, acc_ref, idx_ref, *, tq, ts, nk):
    nb = ts // SLC_BLOCK
    ts_shift = ts.bit_length() - 1
    m_ref[...] = jnp.full(m_ref.shape, NEG_INF, F32)
    acc_ref[...] = jnp.zeros(acc_ref.shape, F32)

    last_sub = (t0 + tq - 1) >> ts_shift
    if mode == "sel":
        def build(j, n):
            idx_ref[n] = j
            return n + jnp.where(((cnt_ref[gp, j] > 0) | (j == last_sub)) & (j <= last_sub), 1, 0)

        n_live = lax.fori_loop(0, nk, build, 0, unroll=4)
    else:
        first_sub = jnp.maximum(t0 - (WINDOW - 1), 0) >> ts_shift
        n_live = last_sub + 1 - first_sub
        for u in range(nsub):
            idx_ref[u] = jnp.minimum(first_sub + u, last_sub)

    key_i = lax.broadcasted_iota(jnp.int32, (ts, tq), 0)
    qry_t = lax.broadcasted_iota(jnp.int32, (ts, tq), 1) + t0

    def sub_tile(step, u):
        pos = step * nsub + u
        return idx_ref[jnp.minimum(pos, n_live - 1)], pos < n_live

    def scores(step):
        for u in range(nsub):
            kj, _ = sub_tile(step, u)
            for gi in range(2):
                for h in range(HPG):
                    pair = gi * (HPG // 2) + h // 2
                    s_ref[u * 2 + gi, :, h * tq:(h + 1) * tq] = _dot(
                        kz_ref[gi * 2 + h % 2, kj], qb_ref[pair * LANES:(pair + 1) * LANES, :])

    def softmax_update(step):
        for u in range(nsub):
            kj, live = sub_tile(step, u)
            j0 = kj * ts
            rel0 = (jnp.zeros((1, tq), jnp.int32) + (j0 - t0)).astype(F32)
            dd = qry_t - (key_i + j0)
            lowest = jnp.where(live, 0, 1 << 30)
            for gi in range(2):
                if mode == "sel":
                    chosen = selt_ref[gi, pl.ds(kj * nb + nb - 1, 1), :]
                    for c in range(nb - 2, -1, -1):
                        chosen = jnp.where(key_i < (c + 1) * SLC_BLOCK,
                                           selt_ref[gi, pl.ds(kj * nb + c, 1), :], chosen)
                    mask = (chosen > 0.5) & (dd >= lowest)
                else:
                    mask = (dd >= lowest) & (dd < WINDOW)
                ps = []
                alphas = []
                for h in range(HPG):
                    shift = (slopes_ref[(gp * 2 + gi) * HPG + h] * LOG2E) * rel0
                    cols = slice(h * tq, (h + 1) * tq)
                    x = jnp.where(mask, s_ref[u * 2 + gi, :, cols] + bias_ref[gi * HPG + h], MASKED)
                    m_old = m_ref[gi, :, cols]
                    m_new = jnp.maximum(m_old, jnp.max(x, axis=0, keepdims=True) + shift)
                    p = jnp.exp2(x - (m_new - shift))
                    m_ref[gi, :, cols] = m_new
                    ps.append(p.astype(BF16))
                    alphas.append(jnp.exp2(m_old - m_new))
                pv = _dot(vt_ref[gi, kj], jnp.concatenate(ps, axis=1))
                acc_ref[gi] = acc_ref[gi] * jnp.concatenate(alphas, axis=1) + pv

    def body(step, carry):
        scores(step)
        softmax_update(step)
        return carry

    if mode == "sel":
        assert nsub & (nsub - 1) == 0
        lax.fori_loop(0, (n_live + nsub - 1) >> (nsub.bit_length() - 1), body, 0)
    else:
        body(0, 0)


def _flash_kernel(slopes_ref, cnt_ref, q_ref, ks_ref, vs_ref, kw_ref, vw_ref, gz_ref, selt_ref, o_ref,
                  kzs_ref, vts_ref, kzw_ref, vtw_ref, qb_ref, bias_ref, gate_ref, s_ref, m_ref, acc_ref, idx_ref,
                  *, tq, ts, nk, nsub_sel, nsub_win):
    gp = pl.program_id(1)
    qi = pl.program_id(2)
    t0 = qi * tq

    @pl.when(qi == 0)
    def _():
        _stage_kv(ks_ref, vs_ref, kzs_ref, vts_ref, nk, ts)
        _stage_kv(kw_ref, vw_ref, kzw_ref, vtw_ref, nk, ts)

    key_f = lax.broadcasted_iota(jnp.int32, (ts, tq), 0).astype(F32)
    qb_ref[...] = (q_ref[...] * (DK ** -0.5 * LOG2E)).T.astype(BF16)
    gate_ref[...] = _sigmoid(gz_ref[...].T)
    for gh in range(2 * HPG):
        bias_ref[gh] = (slopes_ref[gp * 2 * HPG + gh] * LOG2E) * key_f

    common = dict(tq=tq, ts=ts, nk=nk)
    _attend("sel", nsub_sel, gp, t0, slopes_ref, cnt_ref, selt_ref, qb_ref, bias_ref, kzs_ref, vts_ref, s_ref,
            m_ref.at[0], acc_ref.at[0], idx_ref, **common)
    _attend("win", nsub_win, gp, t0, slopes_ref, None, None, qb_ref, bias_ref, kzw_ref, vtw_ref, s_ref,
            m_ref.at[1], acc_ref.at[1], idx_ref, **common)

    outs = []
    for gi in range(2):
        branch_out = [acc_ref[br, gi, 0:DK, :] / acc_ref[br, gi, DK:DK + 1, :] for br in range(2)]
        for h in range(HPG):
            row = (gp * 2 * HPG + gi * HPG + h) * 3
            cols = slice(h * tq, (h + 1) * tq)
            outs.append(branch_out[0][:, cols] * gate_ref[pl.ds(row + 1, 1), :]
                        + branch_out[1][:, cols] * gate_ref[pl.ds(row + 2, 1), :])
    for pair, tile in enumerate(_untranspose_pairs(jnp.concatenate(outs, axis=1), tq)):
        o_ref[:, pair * LANES:(pair + 1) * LANES] = tile


def _flash(proj, selt, cnt, slopes, bsz, seq):
    tq = min(Q_TILE, seq)
    ts = KEY_TILE
    nsub_sel = 2
    nsub_win = (WINDOW + tq) // ts
    nq = seq // tq
    nk = seq // ts
    nslc = seq // SLC_BLOCK
    first_kv = (COL_KV + 2 * NSA_KV_WIDTH) // LANES
    kv_spec = lambda which: pl.BlockSpec((seq, LANES), lambda b, g, i: (b, first_kv + 2 * which + g))
    smem = pltpu.SMEM
    max_units = 2 * max(nsub_sel, nsub_win)
    return pl.pallas_call(
        functools.partial(_flash_kernel, tq=tq, ts=ts, nk=nk, nsub_sel=nsub_sel, nsub_win=nsub_win),
        grid=(bsz, 2, nq),
        in_specs=[pl.BlockSpec(memory_space=smem),
                  pl.BlockSpec((None, None, 2, nk), lambda b, g, i: (b, i, 0, 0), memory_space=smem),
                  pl.BlockSpec((tq, 2 * HPG * DK), lambda b, g, i: (b * nq + i, g)),
                  kv_spec(0), kv_spec(1), kv_spec(2), kv_spec(3),
                  pl.BlockSpec((tq, LANES), lambda b, g, i: (b * nq + i, COL_GZ // LANES)),
                  pl.BlockSpec((2, None, nslc, tq), lambda b, g, i: (g, b, 0, i))],
        out_specs=pl.BlockSpec((tq, 2 * HPG * DK), lambda b, g, i: (b * nq + i, g)),
        out_shape=jax.ShapeDtypeStruct((bsz * seq, NSA_WIDTH), F32),
        scratch_shapes=[pltpu.VMEM((4, nk, ts, LANES), BF16),
                        pltpu.VMEM((2, nk, VT_ROWS, ts), BF16),
                        pltpu.VMEM((4, nk, ts, LANES), BF16),
                        pltpu.VMEM((2, nk, VT_ROWS, ts), BF16),
                        pltpu.VMEM((2 * HPG * DK, tq), BF16),
                        pltpu.VMEM((2 * HPG, ts, tq), F32),
                        pltpu.VMEM((LANES, tq), F32),
                        pltpu.VMEM((max_units, ts, HPG * tq), F32),
                        pltpu.VMEM((2, 2, 1, HPG * tq), F32),
                        pltpu.VMEM((2, 2, VT_ROWS, HPG * tq), F32),
                        pltpu.SMEM((nk + max_units,), jnp.int32)],
        compiler_params=_params(("parallel", "parallel", "arbitrary")),
        name="nsa_sel_win",
    )(slopes, cnt, proj, proj, proj, proj, proj, proj, selt)


def _lru_kernel(x_ref, y_ref, cw_ref, cb_ref, wa_ref, ba_ref, wi_ref, bi_ref, lam_ref, o_ref,
                tail_ref, h_ref, *, tt):
    @pl.when(pl.program_id(1) == 0)
    def _():
        tail_ref[...] = jnp.zeros_like(tail_ref)
        h_ref[...] = jnp.zeros_like(h_ref)

    x = x_ref[...]
    ext = jnp.concatenate([tail_ref[...], x], axis=0)
    xc = cb_ref[...] + x * cw_ref[CONV_WIDTH - 1:CONV_WIDTH, :]
    for k in range(CONV_WIDTH - 1):
        back = CONV_WIDTH - 1 - k
        xc = xc + ext[8 - back:8 - back + tt, :] * cw_ref[k:k + 1, :]
    tail_ref[...] = x[tt - 8:, :]

    xcb = xc.astype(BF16)
    r = _sigmoid(_dot(xcb, wa_ref[...]) + ba_ref[...])
    gate_i = _sigmoid(_dot(xcb, wi_ref[...]) + bi_ref[...])
    neg_lam = -lam_ref[...]
    softplus = jnp.maximum(neg_lam, 0.0) + jnp.log1p(jnp.exp(-jnp.abs(neg_lam)))
    log_a = -LRU_C * r * softplus
    a = jnp.exp(log_a)
    u = jnp.sqrt(jnp.tanh(-log_a) * (a * a + 1.0)) * (gate_i * xc)

    sub = min(LRU_SCAN_ROWS, tt)
    row_in_sub = lax.broadcasted_iota(jnp.int32, (tt, 1), 0) & (sub - 1)
    step = 1
    while step < sub:
        keep = row_in_sub >= step
        a_prev = jnp.where(keep, pltpu.roll(a, step, 0), 1.0)
        u_prev = jnp.where(keep, pltpu.roll(u, step, 0), 0.0)
        u = u + a * u_prev
        a = a * a_prev
        step *= 2
    carry = h_ref[...]
    y = y_ref[...]
    for k in range(tt // sub):
        rows = slice(k * sub, (k + 1) * sub)
        hs = u[rows] + a[rows] * carry
        carry = hs[sub - 1:sub, :]
        o_ref[rows, :] = hs * jax.nn.gelu(y[rows], approximate=True)
    h_ref[...] = carry


def _lru(proj, cw, cb, wa, ba, wi, bi, lam, bsz, seq):
    tt = min(1024, seq)
    nt = seq // tt
    w = LRU_WIDTH
    xcol = COL_LRU_X // w
    ycol = COL_LRU_Y // w
    vec = pl.BlockSpec((1, w), lambda b, i: (0, 0))
    mat = pl.BlockSpec((w, w), lambda b, i: (0, 0))
    return pl.pallas_call(
        functools.partial(_lru_kernel, tt=tt),
        grid=(bsz, nt),
        in_specs=[pl.BlockSpec((tt, w), lambda b, i: (b * nt + i, xcol)),
                  pl.BlockSpec((tt, w), lambda b, i: (b * nt + i, ycol)),
                  pl.BlockSpec((CONV_WIDTH, w), lambda b, i: (0, 0)),
                  vec, mat, vec, mat, vec, vec],
        out_specs=pl.BlockSpec((tt, w), lambda b, i: (b * nt + i, 0)),
        out_shape=jax.ShapeDtypeStruct((bsz * seq, w), F32),
        scratch_shapes=[pltpu.VMEM((8, w), F32), pltpu.VMEM((1, w), F32)],
        compiler_params=_params(("parallel", "arbitrary")),
        name="rglru",
    )(proj, proj, cw, cb, wa, ba, wi, bi, lam)


def _gla_kernel(q_ref, k_ref, v_ref, r_ref, gz_ref, w2_ref, bg_ref, ng_ref, o_ref, st_ref, *, tt):
    c = GLA_CHUNK
    dh = GLA_HEAD_DIM

    @pl.when(pl.program_id(1) == 0)
    def _():
        st_ref[...] = jnp.zeros_like(st_ref)

    gate = _dot(gz_ref[...].astype(BF16), w2_ref[...]) + bg_ref[...]
    log_alpha = (jnp.minimum(gate, 0.0) - jnp.log1p(jnp.exp(-jnp.abs(gate)))) / GLA_GATE_TAU
    causal = (lax.broadcasted_iota(jnp.int32, (c, c), 1) <= lax.broadcasted_iota(jnp.int32, (c, c), 0))
    tri = jnp.where(causal, 1.0, 0.0)

    for n in range(tt // c):
        rs = slice(n * c, (n + 1) * c)
        bcum = _dot(tri, log_alpha[rs], precision=lax.Precision.HIGHEST)
        b_last = bcum[c - 1:c, :]
        e_pos = jnp.exp(bcum)
        q_t = (q_ref[rs, :] * dh ** -0.5) * e_pos
        kk = k_ref[rs, :]
        k_t = kk * jnp.exp(-bcum)
        k_end = kk * jnp.exp(b_last - bcum)
        decay = jnp.exp(b_last)
        vv = v_ref[rs, :]
        for hh in range(GLA_HEADS):
            cs = slice(hh * dh, (hh + 1) * dh)
            qh = q_t[:, cs].astype(BF16)
            vh = vv[:, cs]
            att = jnp.where(causal, _dot_nt(qh, k_t[:, cs].astype(BF16)), 0.0)
            st = st_ref[hh]
            o = _dot(att.astype(BF16), vh.astype(BF16)) + _dot_nt(qh, st.astype(BF16))
            st_ref[hh] = st * decay[:, cs] + _dot(vh.T.astype(BF16), k_end[:, cs].astype(BF16))
            o = _rms_scale(o) * ng_ref[:, cs]
            rr = r_ref[rs, cs]
            o_ref[rs, cs] = o * (rr * _sigmoid(rr))


def _gla(proj, w2pad, bg, ng, bsz, seq):
    tt = min(512, seq)
    nt = seq // tt
    w = GLA_WIDTH
    cols = [COL_GLA_Q // w, COL_GLA_K // w, COL_GLA_V // w, COL_GLA_R // w]
    seg = [pl.BlockSpec((tt, w), functools.partial(lambda b, i, cc: (b * nt + i, cc), cc=cc)) for cc in cols]
    vec = pl.BlockSpec((1, w), lambda b, i: (0, 0))
    return pl.pallas_call(
        functools.partial(_gla_kernel, tt=tt),
        grid=(bsz, nt),
        in_specs=seg + [pl.BlockSpec((tt, LANES), lambda b, i: (b * nt + i, COL_GZ // LANES)),
                        pl.BlockSpec((LANES, w), lambda b, i: (0, 0)), vec, vec],
        out_specs=pl.BlockSpec((tt, w), lambda b, i: (b * nt + i, 0)),
        out_shape=jax.ShapeDtypeStruct((bsz * seq, w), F32),
        scratch_shapes=[pltpu.VMEM((GLA_HEADS, GLA_HEAD_DIM, GLA_HEAD_DIM), F32)],
        compiler_params=_params(("parallel", "arbitrary")),
        name="gla",
    )(proj, proj, proj, proj, proj, w2pad, bg, ng)


def _out_proj_kernel(ocmp_ref, oselwin_ref, olru_ref, ogla_ref, h_ref, w_ref, g_ref, o_ref, wb_ref):
    @pl.when(pl.program_id(0) == 0)
    def _():
        wb_ref[...] = w_ref[...].astype(BF16)

    nsa = ocmp_ref[...] + oselwin_ref[...]
    y = _dot(nsa.astype(BF16), wb_ref[0:NSA_WIDTH, :])
    y = y + _dot(olru_ref[...].astype(BF16), wb_ref[NSA_WIDTH:NSA_WIDTH + LRU_WIDTH, :])
    y = y + _dot(ogla_ref[...].astype(BF16), wb_ref[NSA_WIDTH + LRU_WIDTH:, :])
    o_ref[...] = h_ref[...] + _rms_scale(y) * g_ref[...]


def _resident(shape, layer):
    zeros = (0,) * len(shape)
    return pl.BlockSpec((None,) + shape, lambda i: (layer,) + zeros, pipeline_mode=pl.Buffered(1))


def _out_proj(ocmp, oselwin, olru, ogla, h2d, w_out, g, layer):
    m = h2d.shape[0]
    tm = min(256, m)
    row = lambda width: pl.BlockSpec((tm, width), lambda i: (i, 0))
    return pl.pallas_call(
        _out_proj_kernel,
        grid=(m // tm,),
        in_specs=[row(NSA_WIDTH), row(NSA_WIDTH), row(LRU_WIDTH), row(GLA_WIDTH), row(D_MODEL),
                  _resident((D_MODEL, D_MODEL), layer),
                  pl.BlockSpec((1, D_MODEL), lambda i: (0, 0))],
        out_specs=row(D_MODEL),
        out_shape=jax.ShapeDtypeStruct((m, D_MODEL), F32),
        scratch_shapes=[pltpu.VMEM((D_MODEL, D_MODEL), BF16)],
        compiler_params=_params(("arbitrary",)),
        name="out_proj",
    )(ocmp, oselwin, olru, ogla, h2d, w_out, g)


def _mlp_kernel(h_ref, gpre_ref, wu_ref, wd_ref, gpost_ref, o_ref, un_ref):
    f = pl.program_id(1)

    @pl.when(f == 0)
    def _():
        un_ref[...] = (_rms_scale(h_ref[...]) * gpre_ref[...]).astype(BF16)
        o_ref[...] = jnp.zeros_like(o_ref)

    a = jnp.maximum(_dot(un_ref[...], wu_ref[...].astype(BF16)), 0.0)
    o_ref[...] += _dot((a * a).astype(BF16), wd_ref[...].astype(BF16))

    @pl.when(f == pl.num_programs(1) - 1)
    def _():
        o_ref[...] = h_ref[...] + _rms_scale(o_ref[...]) * gpost_ref[...]


def _mlp(h2d, gpre, w_up, w_down, gpost, layer):
    m = h2d.shape[0]
    tm = min(1024, m)
    tf = 512
    vec = pl.BlockSpec((1, D_MODEL), lambda i, f: (0, 0))
    return pl.pallas_call(
        _mlp_kernel,
        grid=(m // tm, D_FF // tf),
        in_specs=[pl.BlockSpec((tm, D_MODEL), lambda i, f: (i, 0)), vec,
                  pl.BlockSpec((None, D_MODEL, tf), lambda i, f: (layer, 0, f)),
                  pl.BlockSpec((None, tf, D_MODEL), lambda i, f: (layer, f, 0)), vec],
        out_specs=pl.BlockSpec((tm, D_MODEL), lambda i, f: (i, 0), pipeline_mode=pl.Buffered(1)),
        out_shape=jax.ShapeDtypeStruct((m, D_MODEL), F32),
        scratch_shapes=[pltpu.VMEM((tm, D_MODEL), BF16)],
        compiler_params=_params(("parallel", "arbitrary")),
        name="mlp",
    )(h2d, gpre, w_up, w_down, gpost)


def _ple_kernel(h_ref, p_ref, wg_ref, wp_ref, o_ref, wgb_ref, wpb_ref):
    @pl.when(pl.program_id(0) == 0)
    def _():
        wgb_ref[...] = wg_ref[...].astype(BF16)
        wpb_ref[...] = wp_ref[...].astype(BF16)

    h = h_ref[...]
    gate = _sigmoid(_dot(h.astype(BF16), wgb_ref[...]))
    o_ref[...] = h + gate * _dot(p_ref[...].astype(BF16), wpb_ref[...])


def _ple(h2d, p_all, w_gate, w_ple, layer):
    m = h2d.shape[0]
    tm = min(512, m)
    return pl.pallas_call(
        _ple_kernel,
        grid=(m // tm,),
        in_specs=[pl.BlockSpec((tm, D_MODEL), lambda i: (i, 0)),
                  pl.BlockSpec((None, tm, PLE_DIM), lambda i: (layer, i, 0)),
                  _resident((D_MODEL, D_MODEL), layer),
                  _resident((PLE_DIM, D_MODEL), layer)],
        out_specs=pl.BlockSpec((tm, D_MODEL), lambda i: (i, 0)),
        out_shape=jax.ShapeDtypeStruct((m, D_MODEL), F32),
        scratch_shapes=[pltpu.VMEM((D_MODEL, D_MODEL), BF16), pltpu.VMEM((PLE_DIM, D_MODEL), BF16)],
        compiler_params=_params(("arbitrary",)),
        name="ple",
    )(h2d, p_all, w_gate, w_ple)


def _w_in_views(w_in):
    w_t = jnp.swapaxes(w_in, 1, 2)
    g0 = NSA_WIDTH + 6 * NSA_KV_WIDTH
    z0 = D_IN - GLA_GATE_RANK
    w_gz = jnp.concatenate([w_t[:, g0:g0 + N_GATES], w_t[:, z0:]], axis=1)
    return w_t, w_gz


def _compress_weights(cmp_w, cmp_pe):
    w4 = cmp_w.reshape(2, CMP_LEN, DK, DK)
    eye = jnp.eye(2, dtype=cmp_w.dtype)
    wbd = jnp.einsum("clde,xy->clxdye", w4, eye).reshape(2, CMP_LEN, LANES, LANES).astype(BF16)
    pe = jnp.tile(cmp_pe, (1, 1, 2)).reshape(2, CMP_LEN, 1, LANES)
    return pe, wbd


def _block_diag(w):
    eye = jnp.eye(LRU_BLOCKS, dtype=w.dtype)
    return jnp.einsum("ncd,nm->ncmd", w, eye).reshape(LRU_WIDTH, LRU_WIDTH).astype(BF16)


def _layer(h2d, bsz, seq, layer, stacked, norm_mix_pre, nsa_cmp_w, nsa_cmp_pe, lru_conv_w, lru_conv_b,
           lru_wa, lru_ba, lru_wi, lru_bi, lru_lambda, gla_w_gate2, gla_b_gate, gla_norm, norm_mix_post,
           norm_mlp_pre, norm_mlp_post):
    w_in_all, w_out_all, w_up_all, w_down_all, w_ple_gate_all, w_ple_all, p_all = stacked
    row = lambda v: v.reshape(1, -1)
    proj = _in_proj(h2d, row(norm_mix_pre), *w_in_all, layer)

    cmp_kv = _compress(proj, *_compress_weights(nsa_cmp_w, nsa_cmp_pe), bsz, seq)
    o_cmp, selt, cnt = _cmp_attn(proj, cmp_kv, bsz, seq)
    slopes = jnp.asarray(SLOPES, F32)
    o_sel_win = _flash(proj, selt, cnt, slopes, bsz, seq)

    o_lru = _lru(proj, lru_conv_w, row(lru_conv_b), _block_diag(lru_wa), row(lru_ba),
                 _block_diag(lru_wi), row(lru_bi), row(lru_lambda), bsz, seq)

    w2pad = jnp.zeros((LANES, GLA_WIDTH), F32).at[N_GATES:N_GATES + GLA_GATE_RANK].set(gla_w_gate2)
    o_gla = _gla(proj, w2pad.astype(BF16), row(gla_b_gate), row(gla_norm), bsz, seq)

    h2d = _out_proj(o_cmp, o_sel_win, o_lru, o_gla, h2d, w_out_all, row(norm_mix_post), layer)
    h2d = _mlp(h2d, row(norm_mlp_pre), w_up_all, w_down_all, row(norm_mlp_post), layer)
    return _ple(h2d, p_all, w_ple_gate_all, w_ple_all, layer)


def kernel(x, p, norm_mix_pre, w_in, nsa_cmp_w, nsa_cmp_pe, lru_conv_w, lru_conv_b, lru_wa, lru_ba,
           lru_wi, lru_bi, lru_lambda, gla_w_gate2, gla_b_gate, gla_norm, w_out, norm_mix_post,
           norm_mlp_pre, w_up, w_down, norm_mlp_post, w_ple_gate, w_ple):
    bsz, seq, _ = x.shape
    h2d = x.reshape(bsz * seq, D_MODEL)
    small = (norm_mix_pre, nsa_cmp_w, nsa_cmp_pe, lru_conv_w, lru_conv_b, lru_wa, lru_ba, lru_wi, lru_bi,
             lru_lambda, gla_w_gate2, gla_b_gate, gla_norm, norm_mix_post, norm_mlp_pre, norm_mlp_post)
    depth = p.shape[0]
    stacked = (_w_in_views(w_in), w_out, w_up, w_down, w_ple_gate, w_ple,
               p.reshape(depth, bsz * seq, PLE_DIM))
    for i in range(depth):
        h2d = _layer(h2d, bsz, seq, i, stacked, *(w[i] for w in small))
    return h2d.reshape(bsz, seq, D_MODEL)
```
